```python
import math
import jax, jax.numpy as jnp
from jax import lax
import numpy as np

D_MODEL = 1024
BATCH = 32
SEQ = 256
DEPTH = 4
DEC_BATCH = 2
DEC_SEQ = 4096
PAST_LEN = 512

GRID_W = 64
HEAD_DIM = 64
N_HEADS_NA = 8
N_HEADS_RWKV = 8
N_HEADS_DIFF = 8
D_NA = N_HEADS_NA * HEAD_DIM
D_RWKV = N_HEADS_RWKV * HEAD_DIM
D_DIFF = N_HEADS_DIFF * 2 * HEAD_DIM
NA_KH = 8
NA_KW = 16
DECAY_LORA = 64
ICL_LORA = 64
GATE_LORA = 128
P_RWKV = 3 * D_RWKV + DECAY_LORA + ICL_LORA + GATE_LORA
P_EVEN = 3 * D_NA + P_RWKV
RWKV_SPLITS = (D_RWKV, 2 * D_RWKV, 3 * D_RWKV, 3 * D_RWKV + DECAY_LORA, 3 * D_RWKV + DECAY_LORA + ICL_LORA)
D_FF = -(-(8 * D_MODEL) // (3 * 256)) * 256
N_EVEN = (DEPTH + 1) // 2
N_ODD = DEPTH // 2
Q_BLOCK = 128
ROPE_F = HEAD_DIM // 4
ROPE_BASE = 10000.0
NORM_EPS = 1e-6
GN_EPS = 64e-5

kernel_name = "hybrid_diffusion_na_rwkv7_diffattn_step"


def rmsnorm(x, g):
    xf = x.astype(jnp.float32)
    r = lax.rsqrt(jnp.mean(xf * xf, -1, keepdims=True) + NORM_EPS)
    return (xf * r).astype(x.dtype) * g


def adaln(cvec, w, b):
    m = jax.nn.silu(cvec) @ w + b
    return [t[:, None, :] for t in jnp.split(m, 6, axis=-1)]


def modulate(h, shift, scale):
    return h * (1.0 + scale) + shift


def swiglu(h, w1, w3, w2):
    return (jax.nn.silu(h @ w1) * (h @ w3)) @ w2


def centred_shift(y, mu):
    prev = jnp.pad(y[:, :-1], ((0, 0), (1, 0), (0, 0)))
    nxt = jnp.pad(y[:, 1:], ((0, 0), (0, 1), (0, 0)))
    return y + mu[0] * (prev - y) + mu[1] * (nxt - y)


def axial_rope(x):
    T = x.shape[1]
    t = jnp.arange(T)
    pos = jnp.stack([t // GRID_W, t % GRID_W], -1).astype(jnp.float32)
    inv = ROPE_BASE ** (-jnp.arange(ROPE_F, dtype=jnp.float32) / ROPE_F)
    ang = (pos[:, :, None] * inv).reshape((T,) + (1,) * (x.ndim - 3) + (2, ROPE_F))
    cos, sin = jnp.cos(ang).astype(x.dtype), jnp.sin(ang).astype(x.dtype)
    xr = x.reshape(x.shape[:-1] + (2, 2, ROPE_F))
    x1, x2 = xr[..., 0, :], xr[..., 1, :]
    return jnp.stack([x1 * cos - x2 * sin, x1 * sin + x2 * cos], axis=-2).reshape(x.shape)


def softmax_attn_blocked(q, k, v):
    B, T, H, d = q.shape
    qb = jnp.moveaxis(q.reshape(B, T // Q_BLOCK, Q_BLOCK, H, d), 1, 0)

    def blk(q_i):
        s = jnp.einsum('bqhd,bmhd->bhqm', q_i, k) * (d ** -0.5)
        p = jax.nn.softmax(s.astype(jnp.float32), axis=-1).astype(v.dtype)
        return jnp.einsum('bhqm,bmhd->bqhd', p, v)

    o = lax.map(blk, qb)
    return jnp.moveaxis(o, 0, 1).reshape(B, T, H * v.shape[-1])


def na_latent(q, k, v, k_ctx, v_ctx, rpb):
    B, T, H, d = q.shape
    rows = T // GRID_W
    kh = min(NA_KH, rows)
    kg = k.reshape(B, rows, GRID_W, H, d)
    vg = v.reshape(B, rows, GRID_W, H, d)
    cols = jnp.arange(GRID_W)
    c0 = jnp.clip(cols - NA_KW // 2, 0, GRID_W - NA_KW)
    col_idx = c0[:, None] + jnp.arange(NA_KW)[None, :]
    dc = col_idx - cols[:, None] + NA_KW - 1
    scale = d ** -0.5
    qg = jnp.moveaxis(q.reshape(B, rows, GRID_W, H, d), 1, 0)

    def row(args):
        i, q_i = args
        r0 = jnp.clip(i - kh // 2, 0, rows - kh)
        kb = lax.dynamic_slice_in_dim(kg, r0, kh, axis=1)[:, :, col_idx]
        vb = lax.dynamic_slice_in_dim(vg, r0, kh, axis=1)[:, :, col_idx]
        dr = r0 + jnp.arange(kh) - i + NA_KH - 1
        bias = rpb[:, dr[:, None, None], dc[None]].transpose(0, 2, 1, 3)
        s_loc = jnp.einsum('bjhd,brjchd->bhjrc', q_i, kb) * scale + bias[None]
        s_ctx = jnp.einsum('bjhd,bmhd->bhjm', q_i, k_ctx) * scale
        s = jnp.concatenate([s_loc.reshape(B, H, GRID_W, kh * NA_KW), s_ctx], axis=-1)
        p = jax.nn.softmax(s.astype(jnp.float32), axis=-1).astype(v.dtype)
        p_loc = p[..., :kh * NA_KW].reshape(B, H, GRID_W, kh, NA_KW)
        p_ctx = p[..., kh * NA_KW:]
        return (jnp.einsum('bhjrc,brjchd->bjhd', p_loc, vb)
                + jnp.einsum('bhjm,bmhd->bjhd', p_ctx, v_ctx))

    o = lax.map(row, (jnp.arange(rows), qg))
    return jnp.moveaxis(o, 0, 1).reshape(B, T, H * d)


def wkv7_scan(S0, r, decay, kk, a, v, kt, reverse):
    def step(S, inp):
        r_t, w_t, kk_t, a_t, v_t, kt_t = inp
        s_kk = jnp.einsum('bhvk,bhk->bhv', S, kk_t)
        S = (S * w_t[:, :, None, :] - s_kk[..., None] * (kk_t * a_t)[:, :, None, :]
             + v_t[..., None] * kt_t[:, :, None, :])
        return S, jnp.einsum('bhvk,bhk->bhv', S, r_t)

    xs = tuple(jnp.moveaxis(t.astype(jnp.float32), 1, 0) for t in (r, decay, kk, a, v, kt))
    S_fin, ys = lax.scan(step, S0.astype(jnp.float32), xs, reverse=reverse)
    return S_fin, jnp.moveaxis(ys, 0, 1)


def rwkv_mixer(u, S0, mu, w0, w2, a0, a2, k_k, k_a, bonus, g2, ln_w, ln_b):
    B, T, _ = u.shape
    heads = lambda t: t.reshape(B, T, N_HEADS_RWKV, HEAD_DIM)
    u = centred_shift(u, mu)
    r, k, v, w_lo, a_lo, g_lo = jnp.split(u, RWKV_SPLITS, axis=-1)
    rh, vh = heads(r), heads(v)
    kk = heads(k * k_k).astype(jnp.float32)
    kk = kk * lax.rsqrt(jnp.maximum(jnp.sum(kk * kk, -1, keepdims=True), 1e-12))
    ys, bon, finals = [], [], []
    for d in range(2):
        w_raw = (w0[d] + jnp.tanh(w_lo) @ w2[d]).astype(jnp.float32)
        decay = jnp.exp(-jnp.exp(-jax.nn.softplus(-w_raw) - 0.5))
        a = jax.nn.sigmoid(a0[d] + a_lo @ a2[d])
        kt = heads(k * (1.0 + (a - 1.0) * k_a))
        S_fin, y = wkv7_scan(S0[:, d], rh, heads(decay), kk, heads(a), vh, kt, reverse=(d == 1))
        ys.append(y)
        finals.append(S_fin)
        bon.append(jnp.sum(rh * kt * bonus[d], -1, keepdims=True) * vh)
    y = ys[0] + ys[1]
    mean = jnp.mean(y, -1, keepdims=True)
    var = jnp.mean(jnp.square(y - mean), -1, keepdims=True)
    yn = ((y - mean) * lax.rsqrt(var + GN_EPS)).reshape(B, T, D_RWKV).astype(u.dtype) * ln_w + ln_b
    out = (yn + (bon[0] + bon[1]).reshape(B, T, D_RWKV)) * (jax.nn.sigmoid(g_lo) @ g2)
    return out, jnp.stack(finals, axis=1)


def even_mixer(h, w_in, w_out, rpb, rw, na_ctx=None, S0=None):
    B, T, _ = h.shape
    proj = h @ w_in
    qa, ka, va, u = jnp.split(proj, [D_NA, 2 * D_NA, 3 * D_NA], axis=-1)
    heads = lambda t: t.reshape(B, T, N_HEADS_NA, HEAD_DIM)
    qa, ka, va = heads(qa), heads(ka), heads(va)
    if na_ctx is None:
        o_na = softmax_attn_blocked(qa, ka, va)
        S0 = jnp.zeros((B, 2, N_HEADS_RWKV, HEAD_DIM, HEAD_DIM), jnp.float32)
    else:
        o_na = na_latent(qa, ka, va, na_ctx[0], na_ctx[1], rpb)
    o_rw, S_fin = rwkv_mixer(u, S0, *rw)
    y = jnp.concatenate([o_na, o_rw], axis=-1) @ w_out
    return y, ka, va, S_fin


def diff_attn_blocked(q, k, v, lam):
    B, T, H, _, d = q.shape
    qb = jnp.moveaxis(q.reshape(B, T // Q_BLOCK, Q_BLOCK, H, 2, d), 1, 0)

    def blk(q_i):
        s = jnp.einsum('bqhsd,bmhsd->bhsqm', q_i, k) * (d ** -0.5)
        p = jax.nn.softmax(s.astype(jnp.float32), axis=-1)
        att = (p[:, :, 0] - lam * p[:, :, 1]).astype(v.dtype)
        return jnp.einsum('bhqm,bmhe->bqhe', att, v)

    o = lax.map(blk, qb)
    return jnp.moveaxis(o, 0, 1).reshape(B, T, H, v.shape[-1])


def odd_mixer(h, w_qkv, w_out, lam_q, lam_k, subln, lam_init, diff_ctx=None):
    B, T, _ = h.shape
    q, k, v = jnp.split(h @ w_qkv, 3, axis=-1)
    q = q.reshape(B, T, N_HEADS_DIFF, 2, HEAD_DIM)
    k = k.reshape(B, T, N_HEADS_DIFF, 2, HEAD_DIM)
    v = v.reshape(B, T, N_HEADS_DIFF, 2 * HEAD_DIM)
    lq, lk = lam_q.astype(jnp.float32), lam_k.astype(jnp.float32)
    lam = jnp.exp(jnp.sum(lq[0] * lk[0])) - jnp.exp(jnp.sum(lq[1] * lk[1])) + lam_init
    k_cache = k.reshape(B, T, N_HEADS_DIFF, 2 * HEAD_DIM)
    if diff_ctx is None:
        k_all, v_all = k, v
    else:
        M = diff_ctx[0].shape[1]
        q, k = axial_rope(q), axial_rope(k)
        k_all = jnp.concatenate([diff_ctx[0].reshape(B, M, N_HEADS_DIFF, 2, HEAD_DIM), k], axis=1)
        v_all = jnp.concatenate([diff_ctx[1], v], axis=1)
    o = diff_attn_blocked(q, k_all, v_all, lam)
    o = rmsnorm(o, subln) * (1.0 - lam_init)
    y = o.reshape(B, T, D_DIFF) @ w_out
    return y, k_cache, v


def setup_inputs(seed: int = 0) -> dict:
    key = jax.random.key(seed)
    ks = iter(jax.random.split(key, 48))
    nrm = lambda shape, s: jax.random.normal(next(ks), shape, jnp.float32) * s
    gain = lambda shape: 1.0 + nrm(shape, 0.02)
    D = D_MODEL
    return {
        'x_prompt': nrm((BATCH, SEQ, D), 1.0),
        'x_sample': nrm((DEC_BATCH, DEC_SEQ, D), 1.0),
        'c': nrm((DEC_BATCH, D), 1.0),
        'cache_na_k': nrm((DEC_BATCH, N_EVEN, PAST_LEN, N_HEADS_NA, HEAD_DIM), 1.0),
        'cache_na_v': nrm((DEC_BATCH, N_EVEN, PAST_LEN, N_HEADS_NA, HEAD_DIM), 1.0),
        'state_rwkv': nrm((DEC_BATCH, N_EVEN, 2, N_HEADS_RWKV, HEAD_DIM, HEAD_DIM), 0.3),
        'cache_diff_k': nrm((DEC_BATCH, N_ODD, PAST_LEN, N_HEADS_DIFF, 2 * HEAD_DIM), 1.0),
        'cache_diff_v': nrm((DEC_BATCH, N_ODD, PAST_LEN, N_HEADS_DIFF, 2 * HEAD_DIM), 1.0),
        'c_ctx': nrm((D,), 1.0),
        'w_ada': nrm((DEPTH, D, 6 * D), D ** -0.5),
        'b_ada': nrm((DEPTH, 6 * D), 0.02),
        'norm_mix': gain((DEPTH, D)),
        'norm_ffn': gain((DEPTH, D)),
        'norm_final': gain((D,)),
        'w_in_even': nrm((N_EVEN, D, P_EVEN), D ** -0.5),
        'w_out_even': nrm((N_EVEN, D_NA + D_RWKV, D), (D_NA + D_RWKV) ** -0.5),
        'na_rpb': nrm((N_EVEN, N_HEADS_NA, 2 * NA_KH - 1, 2 * NA_KW - 1), 0.1),
        'rw_mu': jax.random.uniform(next(ks), (N_EVEN, 2, P_RWKV), jnp.float32, 0.0, 0.5),
        'rw_w0': nrm((N_EVEN, 2, D_RWKV), 0.5),
        'rw_w2': nrm((N_EVEN, 2, DECAY_LORA, D_RWKV), 0.1),
        'rw_a0': nrm((N_EVEN, 2, D_RWKV), 0.1),
        'rw_a2': nrm((N_EVEN, 2, ICL_LORA, D_RWKV), 0.1),
        'rw_kk': 0.85 + nrm((N_EVEN, D_RWKV), 0.02),
        'rw_ka': gain((N_EVEN, D_RWKV)),
        'rw_bonus': nrm((N_EVEN, 2, N_HEADS_RWKV, HEAD_DIM), 0.1),
        'rw_g2': nrm((N_EVEN, GATE_LORA, D_RWKV), GATE_LORA ** -0.5),
        'rw_lnw': gain((N_EVEN, D_RWKV)),
        'rw_lnb': nrm((N_EVEN, D_RWKV), 0.02),
        'w_qkv_diff': nrm((N_ODD, D, 3 * D_DIFF), D ** -0.5),
        'w_out_diff': nrm((N_ODD, D_DIFF, D), D_DIFF ** -0.5),
        'diff_lam_q': nrm((N_ODD, 2, HEAD_DIM), 0.1),
        'diff_lam_k': nrm((N_ODD, 2, HEAD_DIM), 0.1),
        'diff_subln': gain((N_ODD, 2 * HEAD_DIM)),
        'ffn_w1': nrm((DEPTH, D, D_FF), D ** -0.5),
        'ffn_w3': nrm((DEPTH, D, D_FF), D ** -0.5),
        'ffn_w2': nrm((DEPTH, D_FF, D), D_FF ** -0.5),
    }


def reference(x_prompt, x_sample, c, cache_na_k, cache_na_v, state_rwkv, cache_diff_k, cache_diff_v,
              c_ctx, w_ada, b_ada, norm_mix, norm_ffn, norm_final, w_in_even, w_out_even, na_rpb,
              rw_mu, rw_w0, rw_w2, rw_a0, rw_a2, rw_kk, rw_ka, rw_bonus, rw_g2, rw_lnw, rw_lnb,
              w_qkv_diff, w_out_diff, diff_lam_q, diff_lam_k, diff_subln, ffn_w1, ffn_w3, ffn_w2):
    xp, xs = x_prompt, x_sample
    na_k_out, na_v_out, rw_out, dk_out, dv_out = [], [], [], [], []
    for l in range(DEPTH):
        mp = adaln(c_ctx[None, :], w_ada[l], b_ada[l])
        ms = adaln(c, w_ada[l], b_ada[l])
        hp = modulate(rmsnorm(xp, norm_mix[l]), mp[0], mp[1])
        hs = modulate(rmsnorm(xs, norm_mix[l]), ms[0], ms[1])
        if l % 2 == 0:
            e = l // 2
            rw = (rw_mu[e], rw_w0[e], rw_w2[e], rw_a0[e], rw_a2[e], rw_kk[e], rw_ka[e],
                  rw_bonus[e], rw_g2[e], rw_lnw[e], rw_lnb[e])
            yp, kp, vp, Sp = even_mixer(hp, w_in_even[e], w_out_even[e], na_rpb[e], rw)
            ys, _, _, _ = even_mixer(hs, w_in_even[e], w_out_even[e], na_rpb[e], rw,
                                     (cache_na_k[:, e], cache_na_v[:, e]), state_rwkv[:, e])
            na_k_out.append(kp)
            na_v_out.append(vp)
            rw_out.append(Sp)
        else:
            o = l // 2
            lam_init = 0.8 - 0.6 * math.exp(-0.3 * l)
            dp = (w_qkv_diff[o], w_out_diff[o], diff_lam_q[o], diff_lam_k[o], diff_subln[o], lam_init)
            yp, kp, vp = odd_mixer(hp, *dp)
            ys, _, _ = odd_mixer(hs, *dp, (cache_diff_k[:, o], cache_diff_v[:, o]))
            dk_out.append(kp)
            dv_out.append(vp)
        xp = xp + mp[2] * yp
        xs = xs + ms[2] * ys
        xp = xp + mp[5] * swiglu(modulate(rmsnorm(xp, norm_ffn[l]), mp[3], mp[4]), ffn_w1[l], ffn_w3[l], ffn_w2[l])
        xs = xs + ms[5] * swiglu(modulate(rmsnorm(xs, norm_ffn[l]), ms[3], ms[4]), ffn_w1[l], ffn_w3[l], ffn_w2[l])
    y_prompt = rmsnorm(xp, norm_final)
    y_sample = rmsnorm(xs, norm_final)
    new_na_k = jnp.stack(na_k_out, axis=1)
    new_na_v = jnp.stack(na_v_out, axis=1)
    new_state_rwkv = jnp.stack(rw_out, axis=1)
    new_diff_k = jnp.stack(dk_out, axis=1)
    new_diff_v = jnp.stack(dv_out, axis=1)
    return (y_prompt, y_sample, new_na_k, new_na_v, new_state_rwkv, new_diff_k, new_diff_v)
```

```python
import functools
import math

import jax
import jax.numpy as jnp
from jax import lax
from jax.experimental import pallas as pl
from jax.experimental.pallas import tpu as pltpu

F32 = jnp.float32
BF16 = jnp.bfloat16

D_MODEL = 1024
BATCH = 32
SEQ = 256
DEPTH = 4
DEC_BATCH = 2
DEC_SEQ = 4096
PAST_LEN = 512
GRID_W = 64
GRID_ROWS = DEC_SEQ // GRID_W
HEAD_DIM = 64
N_HEADS = 8
D_NA = 512
D_RWKV = 512
D_DIFF = 1024
NA_KH = 8
NA_KW = 16
DECAY_LORA = 64
ICL_LORA = 64
GATE_LORA = 128
P_RWKV = 3 * D_RWKV + DECAY_LORA + ICL_LORA + GATE_LORA
P_EVEN = 3 * D_NA + P_RWKV
D_FF = 2816
ROPE_F = HEAD_DIM // 4
ROPE_BASE = 10000.0
NORM_EPS = 1e-6
GN_EPS = 64e-5
QK_SCALE = HEAD_DIM ** -0.5

N_CTX_ROWS = BATCH * SEQ
N_LAT_ROWS = DEC_BATCH * DEC_SEQ
M_ALL = N_CTX_ROWS + N_LAT_ROWS

LANES = 128
CHUNK = 64
INV_BLOCK = 16
NEG_BIG = -1e30
VMEM_LIMIT = 56 * 1024 * 1024


def _cparams(sem):
    return pltpu.CompilerParams(dimension_semantics=sem, vmem_limit_bytes=VMEM_LIMIT)


def _bdot(a, b):
    return jnp.dot(a.astype(BF16), b.astype(BF16), preferred_element_type=F32)


def _bdot_nt(a, b):
    return lax.dot_general(a.astype(BF16), b.astype(BF16), (((1,), (1,)), ((), ())),
                           preferred_element_type=F32)


def _bdot_tn(a, b):
    return lax.dot_general(a.astype(BF16), b.astype(BF16), (((0,), (0,)), ((), ())),
                           preferred_element_type=F32)


def _split3(x):
    hi = x.astype(BF16)
    r1 = x - hi.astype(F32)
    mid = r1.astype(BF16)
    lo = (r1 - mid.astype(F32)).astype(BF16)
    return hi, mid, lo


def _dot3_rhs_exact(x, e):
    hi, mid, lo = _split3(x)
    d = functools.partial(jnp.dot, preferred_element_type=F32)
    return d(hi, e) + d(mid, e) + d(lo, e)


def _dot3_lhs_exact(e, x):
    hi, mid, lo = _split3(x)
    d = functools.partial(jnp.dot, preferred_element_type=F32)
    return d(e, hi) + d(e, mid) + d(e, lo)


def _sigmoid(x):
    return 1.0 / (1.0 + jnp.exp(-x))


def _softplus(x):
    return jnp.maximum(x, 0.0) + jnp.log(1.0 + jnp.exp(-jnp.abs(x)))


def _norm_mod(x, g, shift, scale):
    r = lax.rsqrt(jnp.mean(x * x, axis=-1, keepdims=True) + NORM_EPS)
    return ((x * r) * g) * (1.0 + scale) + shift


def _group_of_tile(i, tm):
    n_ctx_tiles = N_CTX_ROWS // tm
    tiles_per_lat = DEC_SEQ // tm
    return jnp.where(i < n_ctx_tiles, 0, 1 + (i - n_ctx_tiles) // tiles_per_lat)


def _mod_spec(tm):
    return pl.BlockSpec((1, 1, D_MODEL), lambda i, *_: (_group_of_tile(i, tm), 0, 0))


def _full_spec(shape):
    n = len(shape)
    return pl.BlockSpec(shape, lambda *_: (0,) * n)


def _ada_kernel(cv_ref, w_ref, b_ref, o_ref):
    cv = cv_ref[...]
    s = cv * _sigmoid(cv)
    o_ref[0] = _bdot(s, w_ref[0]) + b_ref[0]


def _ada_all(cv8, w_ada, b_ada):
    tn = 512
    return pl.pallas_call(
        _ada_kernel,
        grid=(DEPTH, 6 * D_MODEL // tn),
        in_specs=[
            pl.BlockSpec((8, D_MODEL), lambda l, j: (0, 0)),
            pl.BlockSpec((1, D_MODEL, tn), lambda l, j: (l, 0, j)),
            pl.BlockSpec((1, 1, tn), lambda l, j: (l, 0, j)),
        ],
        out_specs=pl.BlockSpec((1, 8, tn), lambda l, j: (l, 0, j)),
        out_shape=jax.ShapeDtypeStruct((DEPTH, 8, 6 * D_MODEL), F32),
        compiler_params=_cparams(("parallel", "parallel")),
        name="adaln",
    )(cv8, w_ada, b_ada.reshape(DEPTH, 1, 6 * D_MODEL))


def _proj_kernel(x_ref, g_ref, sh_ref, sc_ref, w_ref, *o_refs, splits):
    h = _norm_mod(x_ref[...], g_ref[...], sh_ref[0], sc_ref[0]).astype(BF16)
    off = 0
    for o_ref, n in zip(o_refs, splits):
        o_ref[...] = jnp.dot(h, w_ref[:, off:off + n], preferred_element_type=F32)
        off += n


def _norm_proj(x, g, shift, scale, w_bf16, splits, name):
    tm = 512
    n = w_bf16.shape[1]
    return pl.pallas_call(
        functools.partial(_proj_kernel, splits=splits),
        grid=(M_ALL // tm,),
        in_specs=[
            pl.BlockSpec((tm, D_MODEL), lambda i: (i, 0)),
            _full_spec((1, D_MODEL)),
            _mod_spec(tm),
            _mod_spec(tm),
            _full_spec((D_MODEL, n)),
        ],
        out_specs=[pl.BlockSpec((tm, s), lambda i: (i, 0)) for s in splits],
        out_shape=[jax.ShapeDtypeStruct((M_ALL, s), F32) for s in splits],
        compiler_params=_cparams(("parallel",)),
        name=name,
    )(x, g.reshape(1, D_MODEL), shift, scale, w_bf16)


def _out_kernel(x_ref, gate_ref, *refs, n_in):
    a_refs, w_refs, o_ref = refs[:n_in], refs[n_in:2 * n_in], refs[2 * n_in]
    acc = jnp.dot(a_refs[0][...], w_refs[0][...], preferred_element_type=F32)
    for a_ref, w_ref in zip(a_refs[1:], w_refs[1:]):
        acc = acc + jnp.dot(a_ref[...], w_ref[...], preferred_element_type=F32)
    o_ref[...] = x_ref[...] + gate_ref[0] * acc


def _out_proj(x, gate, acts, ws, name):
    tm = 512
    n_in = len(acts)
    return pl.pallas_call(
        functools.partial(_out_kernel, n_in=n_in),
        grid=(M_ALL // tm,),
        in_specs=[pl.BlockSpec((tm, D_MODEL), lambda i: (i, 0)), _mod_spec(tm)]
        + [pl.BlockSpec((tm, a.shape[1]), lambda i: (i, 0)) for a in acts]
        + [_full_spec(w.shape) for w in ws],
        out_specs=pl.BlockSpec((tm, D_MODEL), lambda i: (i, 0)),
        out_shape=jax.ShapeDtypeStruct((M_ALL, D_MODEL), F32),
        compiler_params=_cparams(("parallel",)),
        name=name,
    )(x, gate, *acts, *ws)


def _ffn_kernel(x_ref, g_ref, sh_ref, sc_ref, gate_ref, w1_ref, w3_ref, w2_ref, o_ref, h_ref, acc_ref):
    f = pl.program_id(1)

    @pl.when(f == 0)
    def _():
        h_ref[...] = _norm_mod(x_ref[...], g_ref[...], sh_ref[0], sc_ref[0]).astype(BF16)

    h = h_ref[...]
    a = jnp.dot(h, w1_ref[...], preferred_element_type=F32)
    b = jnp.dot(h, w3_ref[...], preferred_element_type=F32)
    gated = ((a * _sigmoid(a)) * b).astype(BF16)
    part = jnp.dot(gated, w2_ref[...], preferred_element_type=F32)

    @pl.when(f == 0)
    def _():
        acc_ref[...] = part

    @pl.when(f > 0)
    def _():
        acc_ref[...] += part

    @pl.when(f == pl.num_programs(1) - 1)
    def _():
        o_ref[...] = x_ref[...] + gate_ref[0] * acc_ref[...]


def _ffn(x, g, shift, scale, gate, w1, w3, w2, name):
    tm, tf = 1024, 256
    return pl.pallas_call(
        _ffn_kernel,
        grid=(M_ALL // tm, D_FF // tf),
        in_specs=[
            pl.BlockSpec((tm, D_MODEL), lambda i, f: (i, 0)),
            _full_spec((1, D_MODEL)),
            _mod_spec(tm),
            _mod_spec(tm),
            _mod_spec(tm),
            pl.BlockSpec((D_MODEL, tf), lambda i, f: (0, f)),
            pl.BlockSpec((D_MODEL, tf), lambda i, f: (0, f)),
            pl.BlockSpec((tf, D_MODEL), lambda i, f: (f, 0)),
        ],
        out_specs=pl.BlockSpec((tm, D_MODEL), lambda i, f: (i, 0)),
        out_shape=jax.ShapeDtypeStruct((M_ALL, D_MODEL), F32),
        scratch_shapes=[pltpu.VMEM((tm, D_MODEL), BF16), pltpu.VMEM((tm, D_MODEL), F32)],
        compiler_params=_cparams(("parallel", "arbitrary")),
        name=name,
    )(x, g.reshape(1, D_MODEL), shift, scale, gate, w1, w3, w2)


def _final_norm_kernel(x_ref, g_ref, o_ref):
    x = x_ref[...]
    r = lax.rsqrt(jnp.mean(x * x, axis=-1, keepdims=True) + NORM_EPS)
    o_ref[...] = (x * r) * g_ref[...]


def _final_norm(x, g):
    tm = 1024
    return pl.pallas_call(
        _final_norm_kernel,
        grid=(M_ALL // tm,),
        in_specs=[pl.BlockSpec((tm, D_MODEL), lambda i: (i, 0)), _full_spec((1, D_MODEL))],
        out_specs=pl.BlockSpec((tm, D_MODEL), lambda i: (i, 0)),
        out_shape=jax.ShapeDtypeStruct((M_ALL, D_MODEL), F32),
        compiler_params=_cparams(("parallel",)),
        name="final_norm",
    )(x, g.reshape(1, D_MODEL))


def _half_masks():
    lane = lax.broadcasted_iota(jnp.int32, (1, LANES), 1)
    return (lane < HEAD_DIM, lane >= HEAD_DIM)


def _na_ctx_kernel(q_ref, k_ref, v_ref, o_ref):
    q = q_ref[...] * QK_SCALE
    k = k_ref[...].astype(BF16)
    v = v_ref[...]
    out = jnp.zeros(o_ref.shape, F32)
    for m in _half_masks():
        s = _bdot_nt(jnp.where(m, q, 0.0), k)
        e = jnp.exp(s - jnp.max(s, axis=-1, keepdims=True))
        inv_l = 1.0 / jnp.sum(e, axis=-1, keepdims=True)
        out = out + _bdot(e, jnp.where(m, v, 0.0)) * inv_l
    o_ref[...] = out.astype(BF16)


def _na_ctx(qkv):
    n_pairs = D_NA // LANES
    return pl.pallas_call(
        _na_ctx_kernel,
        grid=(BATCH, n_pairs),
        in_specs=[
            pl.BlockSpec((SEQ, LANES), lambda b, p: (b, p)),
            pl.BlockSpec((SEQ, LANES), lambda b, p: (b, n_pairs + p)),
            pl.BlockSpec((SEQ, LANES), lambda b, p: (b, 2 * n_pairs + p)),
        ],
        out_specs=pl.BlockSpec((SEQ, LANES), lambda b, p: (b, p)),
        out_shape=jax.ShapeDtypeStruct((N_CTX_ROWS, D_NA), BF16),
        compiler_params=_cparams(("parallel", "parallel")),
        name="na_ctx",
    )(qkv, qkv, qkv)


def _na_lat_kernel(q_ref, k_ref, v_ref, kc_ref, vc_ref, bias_ref, o_ref):
    i = pl.program_id(2)
    r0 = jnp.clip(i - NA_KH // 2, 0, GRID_ROWS - NA_KH)
    start = pl.multiple_of(r0 * GRID_W, GRID_W)
    n_loc = NA_KH * GRID_W
    q = q_ref[...] * QK_SCALE
    kl = k_ref[pl.ds(start, n_loc), :].astype(BF16)
    vl = v_ref[pl.ds(start, n_loc), :]
    kc = kc_ref[0].astype(BF16)
    vc = vc_ref[0]
    out = jnp.zeros(o_ref.shape, F32)
    for hh, m in enumerate(_half_masks()):
        qm = jnp.where(m, q, 0.0)
        sl = _bdot_nt(qm, kl) + bias_ref[0, hh]
        sc = _bdot_nt(qm, kc)
        mx = jnp.maximum(jnp.max(sl, axis=-1, keepdims=True), jnp.max(sc, axis=-1, keepdims=True))
        el = jnp.exp(sl - mx)
        ec = jnp.exp(sc - mx)
        inv_l = 1.0 / (jnp.sum(el, axis=-1, keepdims=True) + jnp.sum(ec, axis=-1, keepdims=True))
        pv = _bdot(el, jnp.where(m, vl, 0.0)) + _bdot(ec, jnp.where(m, vc, 0.0))
        out = out + pv * inv_l
    o_ref[...] = out.astype(BF16)


def _na_bias_table(rpb):
    jj = jnp.arange(GRID_W)[:, None]
    cc = jnp.arange(GRID_W)[None, :]
    c0 = jnp.clip(jj - NA_KW // 2, 0, GRID_W - NA_KW)
    inwin = (cc >= c0) & (cc < c0 + NA_KW)
    idx = jnp.clip(cc - jj + NA_KW - 1, 0, 2 * NA_KW - 2)
    full = rpb[:, :, idx]
    full = jnp.where(inwin[None, None], full, NEG_BIG)
    tabs = []
    for v in range(NA_KH):
        t = full[:, v:v + NA_KH]
        tabs.append(t.transpose(0, 2, 1, 3).reshape(N_HEADS, GRID_W, NA_KH * GRID_W))
    return jnp.stack(tabs, axis=0)


def _na_lat(qkv, kc, vc, bias_tab):
    n_pairs = D_NA // LANES
    lat_blk = N_CTX_ROWS // DEC_SEQ
    row_blk = N_CTX_ROWS // GRID_W

    def variant(i):
        return jnp.clip(i - NA_KH // 2, 0, GRID_ROWS - NA_KH) - i + NA_KH - 1

    return pl.pallas_call(
        _na_lat_kernel,
        grid=(DEC_BATCH, n_pairs, GRID_ROWS),
        in_specs=[
            pl.BlockSpec((GRID_W, LANES), lambda b, p, i: (row_blk + b * GRID_ROWS + i, p)),
            pl.BlockSpec((DEC_SEQ, LANES), lambda b, p, i: (lat_blk + b, n_pairs + p)),
            pl.BlockSpec((DEC_SEQ, LANES), lambda b, p, i: (lat_blk + b, 2 * n_pairs + p)),
            pl.BlockSpec((1, PAST_LEN, LANES), lambda b, p, i: (b, 0, p)),
            pl.BlockSpec((1, PAST_LEN, LANES), lambda b, p, i: (b, 0, p)),
            pl.BlockSpec((1, 2, GRID_W, NA_KH * GRID_W), lambda b, p, i: (variant(i), p, 0, 0)),
        ],
        out_specs=pl.BlockSpec((GRID_W, LANES), lambda b, p, i: (b * GRID_ROWS + i, p)),
        out_shape=jax.ShapeDtypeStruct((N_LAT_ROWS, D_NA), BF16),
        compiler_params=_cparams(("parallel", "parallel", "arbitrary")),
        name="na_lat",
    )(qkv, qkv, qkv, kc, vc, bias_tab)


def _lam_value(lq_ref, lk_ref, lam_init):
    s = jnp.sum(lq_ref[...] * lk_ref[...], axis=-1, keepdims=True)
    return jnp.exp(s[0:1]) - jnp.exp(s[1:2]) + lam_init


def _subln(o, sub, lam_init):
    r = lax.rsqrt(jnp.mean(o * o, axis=-1, keepdims=True) + NORM_EPS)
    return ((o * r) * sub) * (1.0 - lam_init)


def _diff_ctx_kernel(lq_ref, lk_ref, sub_ref, q_ref, k_ref, v_ref, o_ref, *, lam_init):
    lam = _lam_value(lq_ref, lk_ref, lam_init)
    q = q_ref[...] * QK_SCALE
    k = k_ref[...].astype(BF16)
    p = []
    for m in _half_masks():
        s = _bdot_nt(jnp.where(m, q, 0.0), k)
        e = jnp.exp(s - jnp.max(s, axis=-1, keepdims=True))
        p.append(e * (1.0 / jnp.sum(e, axis=-1, keepdims=True)))
    o = _bdot(p[0] - lam * p[1], v_ref[...])
    o_ref[...] = _subln(o, sub_ref[...], lam_init).astype(BF16)


def _diff_ctx(qkv, lam_q, lam_k, subln, lam_init):
    return pl.pallas_call(
        functools.partial(_diff_ctx_kernel, lam_init=lam_init),
        grid=(BATCH, N_HEADS),
        in_specs=[
            _full_spec((2, HEAD_DIM)),
            _full_spec((2, HEAD_DIM)),
            _full_spec((1, LANES)),
            pl.BlockSpec((SEQ, LANES), lambda b, h: (b, h)),
            pl.BlockSpec((SEQ, LANES), lambda b, h: (b, N_HEADS + h)),
            pl.BlockSpec((SEQ, LANES), lambda b, h: (b, 2 * N_HEADS + h)),
        ],
        out_specs=pl.BlockSpec((SEQ, LANES), lambda b, h: (b, h)),
        out_shape=jax.ShapeDtypeStruct((N_CTX_ROWS, D_DIFF), BF16),
        compiler_params=_cparams(("parallel", "parallel")),
        name="diff_ctx",
    )(lam_q, lam_k, subln.reshape(1, LANES), qkv, qkv, qkv)


def _rope_kernel(x_ref, cos_ref, sin_ref, o_ref):
    cos = cos_ref[...]
    sin = sin_ref[...]
    lane = lax.broadcasted_iota(jnp.int32, (1, LANES), 1)
    first = (lane % (2 * ROPE_F)) < ROPE_F
    n_q = D_DIFF // LANES
    for j in range(2 * n_q):
        x = x_ref[:, j * LANES:(j + 1) * LANES]
        partner = jnp.where(first, pltpu.roll(x, LANES - ROPE_F, 1), pltpu.roll(x, ROPE_F, 1))
        y = x * cos + partner * sin
        if j < n_q:
            y = y * QK_SCALE
        o_ref[:, j * LANES:(j + 1) * LANES] = y.astype(BF16)
    o_ref[:, 2 * D_DIFF:] = x_ref[:, 2 * D_DIFF:].astype(BF16)


def _rope_tables():
    t = jnp.arange(DEC_SEQ)
    pos = jnp.stack([t // GRID_W, t % GRID_W], -1).astype(F32)
    inv = ROPE_BASE ** (-jnp.arange(ROPE_F, dtype=F32) / ROPE_F)
    ang = pos[:, :, None] * inv
    cos, sin = jnp.cos(ang), jnp.sin(ang)
    cos64 = jnp.concatenate([cos, cos], axis=-1).reshape(DEC_SEQ, HEAD_DIM)
    sin64 = jnp.concatenate([-sin, sin], axis=-1).reshape(DEC_SEQ, HEAD_DIM)
    return jnp.tile(cos64, (1, 2)), jnp.tile(sin64, (1, 2))


def _rope_cast(qkv, cos_t, sin_t):
    tm = 512
    base = N_CTX_ROWS // tm
    per_seq = DEC_SEQ // tm
    return pl.pallas_call(
        _rope_kernel,
        grid=(N_LAT_ROWS // tm,),
        in_specs=[
            pl.BlockSpec((tm, 3 * D_DIFF), lambda i: (base + i, 0)),
            pl.BlockSpec((tm, LANES), lambda i: (i % per_seq, 0)),
            pl.BlockSpec((tm, LANES), lambda i: (i % per_seq, 0)),
        ],
        out_specs=pl.BlockSpec((tm, 3 * D_DIFF), lambda i: (i, 0)),
        out_shape=jax.ShapeDtypeStruct((N_LAT_ROWS, 3 * D_DIFF), BF16),
        compiler_params=_cparams(("parallel",)),
        name="rope_cast",
    )(qkv, cos_t, sin_t)


def _diff_lat_kernel(lq_ref, lk_ref, sub_ref, q_ref, kl_ref, vl_ref, kc_ref, vc_ref, o_ref, *, lam_init):
    lam = _lam_value(lq_ref, lk_ref, lam_init)
    q = q_ref[...]
    kl = kl_ref[...]
    kc = kc_ref[0].astype(BF16)
    pl_, pc_ = [], []
    for m in _half_masks():
        qm = jnp.where(m, q, jnp.zeros_like(q))
        sl = _bdot_nt(qm, kl)
        sc = _bdot_nt(qm, kc)
        mx = jnp.maximum(jnp.max(sl, axis=-1, keepdims=True), jnp.max(sc, axis=-1, keepdims=True))
        el = jnp.exp(sl - mx)
        ec = jnp.exp(sc - mx)
        inv_l = 1.0 / (jnp.sum(el, axis=-1, keepdims=True) + jnp.sum(ec, axis=-1, keepdims=True))
        pl_.append(el * inv_l)
        pc_.append(ec * inv_l)
    o = _bdot(pl_[0] - lam * pl_[1], vl_ref[...]) + _bdot(pc_[0] - lam * pc_[1], vc_ref[0])
    o_ref[...] = _subln(o, sub_ref[...], lam_init).astype(BF16)


def _diff_lat(qkv_r, kc, vc, lam_q, lam_k, subln, lam_init):
    tq = 256
    per_seq = DEC_SEQ // tq
    return pl.pallas_call(
        functools.partial(_diff_lat_kernel, lam_init=lam_init),
        grid=(DEC_BATCH, N_HEADS, per_seq),
        in_specs=[
            _full_spec((2, HEAD_DIM)),
            _full_spec((2, HEAD_DIM)),
            _full_spec((1, LANES)),
            pl.BlockSpec((tq, LANES), lambda b, h, i: (b * per_seq + i, h)),
            pl.BlockSpec((DEC_SEQ, LANES), lambda b, h, i: (b, N_HEADS + h)),
            pl.BlockSpec((DEC_SEQ, LANES), lambda b, h, i: (b, 2 * N_HEADS + h)),
            pl.BlockSpec((1, PAST_LEN, LANES), lambda b, h, i: (b, 0, h)),
            pl.BlockSpec((1, PAST_LEN, LANES), lambda b, h, i: (b, 0, h)),
        ],
        out_specs=pl.BlockSpec((tq, LANES), lambda b, h, i: (b * per_seq + i, h)),
        out_shape=jax.ShapeDtypeStruct((N_LAT_ROWS, D_DIFF), BF16),
        compiler_params=_cparams(("parallel", "parallel", "arbitrary")),
        name="diff_lat",
    )(lam_q, lam_k, subln.reshape(1, LANES), qkv_r, qkv_r, qkv_r, kc, vc)


def _token_shift(u, prev_row, next_row, mu):
    n = u.shape[0]
    rows = lax.broadcasted_iota(jnp.int32, (n, 1), 0)
    up = jnp.where(rows == 0, prev_row, pltpu.roll(u, 1, 0))
    un = jnp.where(rows == n - 1, next_row, pltpu.roll(u, n - 1, 0))
    return u + mu[0:1] * (up - u) + mu[1:2] * (un - u)


def _icl_rate_and_key(us, a0, a2p, k_a):
    lo = us[:, 3 * D_RWKV:3 * D_RWKV + LANES]
    k = us[:, D_RWKV:2 * D_RWKV]
    a = _sigmoid(a0 + _bdot(lo, a2p))
    return a, k * (1.0 + (a - 1.0) * k_a)


def _rwkv_scan_kernel(*refs, n_chunks, has_state):
    if has_state:
        (u_ref, up_ref, un_ref, s0_ref, mu_ref, w0_ref, w2_ref, a0_ref, a2_ref, kk_ref, ka_ref, bd_ref,
         y_ref, sfin_ref, s_ref) = refs
    else:
        (u_ref, up_ref, un_ref, mu_ref, w0_ref, w2_ref, a0_ref, a2_ref, kk_ref, ka_ref, bd_ref,
         y_ref, sfin_ref, s_ref) = refs
        s0_ref = None
    d = pl.program_id(1)
    c = pl.program_id(2)
    cc = c + d * (n_chunks - 1 - 2 * c)
    n_pairs = D_RWKV // LANES

    @pl.when(c == 0)
    def _():
        if has_state:
            s_ref[...] = s0_ref[0, 0]
        else:
            s_ref[...] = jnp.zeros(s_ref.shape, F32)

    not_first = (cc != 0).astype(F32)
    not_last = (cc != n_chunks - 1).astype(F32)
    us = _token_shift(u_ref[...], up_ref[7:8, :] * not_first, un_ref[0:1, :] * not_last, mu_ref[...])
    r = us[:, 0:D_RWKV]
    k = us[:, D_RWKV:2 * D_RWKV]
    v = us[:, 2 * D_RWKV:3 * D_RWKV]
    lo = us[:, 3 * D_RWKV:3 * D_RWKV + LANES]

    w_raw = w0_ref[0] + _bdot(jnp.tanh(lo), w2_ref[0])
    logw = -jnp.exp(-_softplus(-w_raw) - 0.5)
    a, kt = _icl_rate_and_key(us, a0_ref[0], a2_ref[0], ka_ref[...])
    kk_raw = k * kk_ref[...]
    ss = _dot3_rhs_exact(kk_raw * kk_raw, bd_ref[...])
    kk = kk_raw * lax.rsqrt(jnp.maximum(ss, 1e-12))
    b = kk * a

    row = lax.broadcasted_iota(jnp.int32, (CHUNK, CHUNK), 0)
    col = lax.broadcasted_iota(jnp.int32, (CHUNK, CHUNK), 1)
    rel = (col - row) * (1 - 2 * d)
    strict = rel < 0
    incl = rel <= 0
    same_blk = (row // INV_BLOCK) == (col // INV_BLOCK)
    eye = (row == col).astype(F32)

    cum = _dot3_lhs_exact(incl.astype(BF16), logw)
    p_in = jnp.exp(cum)
    p_ex = jnp.exp(cum - logw)
    p_inv = jnp.exp(-cum)
    tot = jnp.where(d == 0, cum[CHUNK - 1:CHUNK, :], cum[0:1, :])
    p_tot = jnp.exp(tot)
    a_hat = p_ex * kk
    b_hat = b * p_inv
    k_hat = kt * p_inv
    r_hat = r * p_in

    prow = lax.broadcasted_iota(jnp.int32, (LANES, LANES), 0)
    pcol = lax.broadcasted_iota(jnp.int32, (LANES, LANES), 1)
    blockdiag = (prow // HEAD_DIM) == (pcol // HEAD_DIM)
    eye_pair = (prow == pcol).astype(F32)
    masks = _half_masks()

    for p in range(n_pairs):
        sl = slice(p * LANES, (p + 1) * LANES)
        ah, bh, kh, rh, vv = a_hat[:, sl], b_hat[:, sl], k_hat[:, sl], r_hat[:, sl], v[:, sl]
        heads = []
        u_pair = jnp.zeros((CHUNK, LANES), F32)
        for m in masks:
            am = jnp.where(m, ah, 0.0)
            ar = jnp.concatenate([am, jnp.where(m, rh, 0.0)], axis=0)
            xb = _bdot_nt(ar, bh)
            xk = _bdot_nt(ar, kh)
            l_ab = jnp.where(strict, xb[:CHUNK], 0.0)
            l_ak = jnp.where(strict, xk[:CHUNK], 0.0)
            m_rb = jnp.where(incl, xb[CHUNK:], 0.0)
            m_rk = jnp.where(incl, xk[CHUNK:], 0.0)
            x1 = -jnp.where(same_blk, l_ab, 0.0)
            l_off = jnp.where(same_blk, 0.0, l_ab)
            x2 = _bdot(x1, x1)
            x4 = _bdot(x2, x2)
            x8 = _bdot(x4, x4)
            td = eye + x1
            td = td + _bdot(td, x2)
            td = td + _bdot(td, x4)
            td = td + _bdot(td, x8)
            mm = _bdot(td, l_off)
            n1 = eye - mm
            n2 = n1 + _bdot(n1, _bdot(mm, mm))
            t_inv = _bdot(n2, td)
            vm = jnp.where(m, vv, 0.0)
            u_pair = u_pair + _bdot(l_ak, vm)
            heads.append((am, m_rb, m_rk, t_inv, vm))
        ta_pair = jnp.zeros((CHUNK, LANES), F32)
        w1_pair = jnp.zeros((CHUNK, LANES), F32)
        q_eff = rh
        y_loc = jnp.zeros((CHUNK, LANES), F32)
        for m, (am, m_rb, m_rk, t_inv, vm) in zip(masks, heads):
            tx = _bdot(t_inv, jnp.concatenate([am, jnp.where(m, u_pair, 0.0)], axis=1))
            mx = _bdot(m_rb, tx)
            ta_pair = ta_pair + tx[:, :LANES]
            w1_pair = w1_pair + tx[:, LANES:]
            q_eff = q_eff - mx[:, :LANES]
            y_loc = y_loc + _bdot(m_rk, vm) - mx[:, LANES:]
        pt = p_tot[:, sl]
        g_mat = jnp.where(blockdiag, eye_pair - _bdot_tn(ta_pair, bh), 0.0) * pt
        h_mat = jnp.where(
            blockdiag,
            _bdot_tn(jnp.concatenate([vv, -w1_pair], axis=0), jnp.concatenate([kh, bh], axis=0)),
            0.0) * pt
        s_old = s_ref[p]
        y_ref[0, :, sl] = _bdot_nt(q_eff, s_old) + y_loc
        s_ref[p] = _bdot(s_old, g_mat) + h_mat

    @pl.when(c == n_chunks - 1)
    def _():
        sfin_ref[0, 0] = s_ref[...]


def _rwkv_scan(u, s0p, prm, row_base, n_seq, seq_len):
    n_chunks = seq_len // CHUNK
    n_pairs = D_RWKV // LANES
    base_chunk = row_base // CHUNK
    n_blk8 = M_ALL // 8
    has_state = s0p is not None

    def chunk_id(b, d, c):
        return base_chunk + b * n_chunks + c + d * (n_chunks - 1 - 2 * c)

    in_specs = [
        pl.BlockSpec((CHUNK, P_RWKV), lambda b, d, c: (chunk_id(b, d, c), 0)),
        pl.BlockSpec((8, P_RWKV), lambda b, d, c: (jnp.maximum(chunk_id(b, d, c) * (CHUNK // 8) - 1, 0), 0)),
        pl.BlockSpec((8, P_RWKV),
                     lambda b, d, c: (jnp.minimum((chunk_id(b, d, c) + 1) * (CHUNK // 8), n_blk8 - 1), 0)),
    ]
    args = [u, u, u]
    if has_state:
        in_specs.append(pl.BlockSpec((1, 1, n_pairs, LANES, LANES), lambda b, d, c: (b, d, 0, 0, 0)))
        args.append(s0p)
    in_specs += [
        _full_spec((2, P_RWKV)),
        pl.BlockSpec((1, 1, D_RWKV), lambda b, d, c: (d, 0, 0)),
        pl.BlockSpec((1, LANES, D_RWKV), lambda b, d, c: (d, 0, 0)),
        pl.BlockSpec((1, 1, D_RWKV), lambda b, d, c: (d, 0, 0)),
        pl.BlockSpec((1, LANES, D_RWKV), lambda b, d, c: (d, 0, 0)),
        _full_spec((1, D_RWKV)),
        _full_spec((1, D_RWKV)),
        _full_spec((D_RWKV, D_RWKV)),
    ]
    args += [prm["mu"], prm["w0"], prm["w2p"], prm["a0"], prm["a2p"], prm["kk"], prm["ka"], prm["bd_ones"]]
    n_rows = n_seq * seq_len
    return pl.pallas_call(
        functools.partial(_rwkv_scan_kernel, n_chunks=n_chunks, has_state=has_state),
        grid=(n_seq, 2, n_chunks),
        in_specs=in_specs,
        out_specs=[
            pl.BlockSpec((1, CHUNK, D_RWKV),
                         lambda b, d, c: (d, b * n_chunks + c + d * (n_chunks - 1 - 2 * c), 0)),
            pl.BlockSpec((1, 1, n_pairs, LANES, LANES), lambda b, d, c: (b, d, 0, 0, 0)),
        ],
        out_shape=[
            jax.ShapeDtypeStruct((2, n_rows, D_RWKV), F32),
            jax.ShapeDtypeStruct((n_seq, 2, n_pairs, LANES, LANES), F32),
        ],
        scratch_shapes=[pltpu.VMEM((n_pairs, LANES, LANES), F32)],
        compiler_params=_cparams(("parallel", "parallel", "arbitrary")),
        name="rwkv_scan_" + ("lat" if has_state else "ctx"),
    )(*args)


def _rwkv_fin_kernel(u_ref, up_ref, un_ref, y_ref, mu_ref, a0_ref, a2_ref, ka_ref, bonus_ref, g2_ref,
                     lnw_ref, lnb_ref, bd1_ref, bdm_ref, o_ref, *, tm):
    i = pl.program_id(0)
    n_ctx_tiles = N_CTX_ROWS // tm
    per_seq = DEC_SEQ // tm
    j = (i - n_ctx_tiles) % per_seq
    is_ctx = i < n_ctx_tiles
    not_first = jnp.where(is_ctx | (j == 0), 0.0, 1.0)
    not_last = jnp.where(is_ctx | (j == per_seq - 1), 0.0, 1.0)
    us = _token_shift(u_ref[...], up_ref[7:8, :] * not_first, un_ref[0:1, :] * not_last, mu_ref[...])
    r = us[:, 0:D_RWKV]
    v = us[:, 2 * D_RWKV:3 * D_RWKV]
    g_lo = us[:, 3 * D_RWKV + LANES:]
    bon = jnp.zeros((tm, D_RWKV), F32)
    for d in range(2):
        _, kt = _icl_rate_and_key(us, a0_ref[d], a2_ref[d], ka_ref[...])
        bon = bon + _dot3_rhs_exact(r * kt * bonus_ref[d], bd1_ref[...])
    bon = bon * v
    y = y_ref[0] + y_ref[1]
    mean = _dot3_rhs_exact(y, bdm_ref[...])
    yc = y - mean
    var = _dot3_rhs_exact(yc * yc, bdm_ref[...])
    yn = (yc * lax.rsqrt(var + GN_EPS)) * lnw_ref[...] + lnb_ref[...]
    gate = _bdot(_sigmoid(g_lo), g2_ref[...])
    o_ref[...] = ((yn + bon) * gate).astype(BF16)


def _rwkv_finish(u, y, prm):
    tm = 256
    n_blk8 = M_ALL // 8
    return pl.pallas_call(
        functools.partial(_rwkv_fin_kernel, tm=tm),
        grid=(M_ALL // tm,),
        in_specs=[
            pl.BlockSpec((tm, P_RWKV), lambda i: (i, 0)),
            pl.BlockSpec((8, P_RWKV), lambda i: (jnp.maximum(i * (tm // 8) - 1, 0), 0)),
            pl.BlockSpec((8, P_RWKV), lambda i: (jnp.minimum((i + 1) * (tm // 8), n_blk8 - 1), 0)),
            pl.BlockSpec((2, tm, D_RWKV), lambda i: (0, i, 0)),
            _full_spec((2, P_RWKV)),
            _full_spec((2, 1, D_RWKV)),
            _full_spec((2, LANES, D_RWKV)),
            _full_spec((1, D_RWKV)),
            _full_spec((2, 1, D_RWKV)),
            _full_spec((GATE_LORA, D_RWKV)),
            _full_spec((1, D_RWKV)),
            _full_spec((1, D_RWKV)),
            _full_spec((D_RWKV, D_RWKV)),
            _full_spec((D_RWKV, D_RWKV)),
        ],
        out_specs=pl.BlockSpec((tm, D_RWKV), lambda i: (i, 0)),
        out_shape=jax.ShapeDtypeStruct((M_ALL, D_RWKV), BF16),
        compiler_params=_cparams(("parallel",)),
        name="rwkv_finish",
    )(u, u, u, y, prm["mu"], prm["a0"], prm["a2p"], prm["ka"], prm["bonus"], prm["g2"],
      prm["lnw"], prm["lnb"], prm["bd_ones"], prm["bd_mean"])


def _pack_state_pairs(s):
    lead = s.shape[:-3]
    s = s.reshape(lead + (N_HEADS // 2, 2, HEAD_DIM, HEAD_DIM))
    z = jnp.zeros_like(s[..., 0, :, :])
    top = jnp.concatenate([s[..., 0, :, :], z], axis=-1)
    bot = jnp.concatenate([z, s[..., 1, :, :]], axis=-1)
    return jnp.concatenate([top, bot], axis=-2)


def _unpack_state_pairs(sp):
    lead = sp.shape[:-3]
    a = sp[..., :HEAD_DIM, :HEAD_DIM]
    b = sp[..., HEAD_DIM:, HEAD_DIM:]
    return jnp.stack([a, b], axis=-3).reshape(lead + (N_HEADS, HEAD_DIM, HEAD_DIM))


def kernel(x_prompt, x_sample, c, cache_na_k, cache_na_v, state_rwkv, cache_diff_k, cache_diff_v, c_ctx, w_ada, b_ada, norm_mix, norm_ffn, norm_final, w_in_even, w_out_even, na_rpb, rw_mu, rw_w0, rw_w2, rw_a0, rw_a2, rw_kk, rw_ka, rw_bonus, rw_g2, rw_lnw, rw_lnb, w_qkv_diff, w_out_diff, diff_lam_q, diff_lam_k, diff_subln, ffn_w1, ffn_w3, ffn_w2):
    x = jnp.concatenate([x_prompt.reshape(N_CTX_ROWS, D_MODEL), x_sample.reshape(N_LAT_ROWS, D_MODEL)], axis=0)

    cv8 = jnp.concatenate([c_ctx[None, :], c, jnp.zeros((8 - 1 - DEC_BATCH, D_MODEL), F32)], axis=0)
    mods = _ada_all(cv8, w_ada, b_ada)
    mods = mods.reshape(DEPTH, 8, 6, D_MODEL).transpose(0, 2, 1, 3)[:, :, :1 + DEC_BATCH, None, :]

    hd_idx = jnp.arange(D_RWKV) // HEAD_DIM
    bd_ones = (hd_idx[:, None] == hd_idx[None, :]).astype(BF16)
    bd_mean = (bd_ones.astype(F32) / HEAD_DIM).astype(BF16)
    cos_t, sin_t = _rope_tables()

    na_k_out, na_v_out, rw_out, dk_out, dv_out = [], [], [], [], []
    for l in range(DEPTH):
        md = mods[l]
        if l % 2 == 0:
            e = l // 2
            qkv, u = _norm_proj(x, norm_mix[l], md[0], md[1], w_in_even[e].astype(BF16),
                                (3 * D_NA, P_RWKV), "proj_even")
            o_ctx = _na_ctx(qkv)
            o_lat = _na_lat(qkv, cache_na_k[:, e].reshape(DEC_BATCH, PAST_LEN, D_NA),
                            cache_na_v[:, e].reshape(DEC_BATCH, PAST_LEN, D_NA), _na_bias_table(na_rpb[e]))
            o_na = jnp.concatenate([o_ctx, o_lat], axis=0)

            zpad = jnp.zeros((2, LANES - DECAY_LORA, D_RWKV), F32)
            prm = {
                "mu": rw_mu[e],
                "w0": rw_w0[e].reshape(2, 1, D_RWKV),
                "w2p": jnp.concatenate([rw_w2[e], zpad], axis=1),
                "a0": rw_a0[e].reshape(2, 1, D_RWKV),
                "a2p": jnp.concatenate([zpad, rw_a2[e]], axis=1),
                "kk": rw_kk[e].reshape(1, D_RWKV),
                "ka": rw_ka[e].reshape(1, D_RWKV),
                "bonus": rw_bonus[e].reshape(2, 1, D_RWKV),
                "g2": rw_g2[e],
                "lnw": rw_lnw[e].reshape(1, D_RWKV),
                "lnb": rw_lnb[e].reshape(1, D_RWKV),
                "bd_ones": bd_ones,
                "bd_mean": bd_mean,
            }
            y_ctx, s_ctx = _rwkv_scan(u, None, prm, 0, BATCH, SEQ)
            y_lat, _ = _rwkv_scan(u, _pack_state_pairs(state_rwkv[:, e]), prm, N_CTX_ROWS, DEC_BATCH, DEC_SEQ)
            o_rw = _rwkv_finish(u, jnp.concatenate([y_ctx, y_lat], axis=1), prm)

            w_out = w_out_even[e].astype(BF16)
            x = _out_proj(x, md[2], [o_na, o_rw], [w_out[:D_NA], w_out[D_NA:]], "out_even")

            na_k_out.append(qkv[:N_CTX_ROWS, D_NA:2 * D_NA].reshape(BATCH, SEQ, N_HEADS, HEAD_DIM))
            na_v_out.append(qkv[:N_CTX_ROWS, 2 * D_NA:].reshape(BATCH, SEQ, N_HEADS, HEAD_DIM))
            rw_out.append(_unpack_state_pairs(s_ctx))
        else:
            o = l // 2
            lam_init = 0.8 - 0.6 * math.exp(-0.3 * l)
            (qkv,) = _norm_proj(x, norm_mix[l], md[0], md[1], w_qkv_diff[o].astype(BF16),
                                (3 * D_DIFF,), "proj_odd")
            o_ctx = _diff_ctx(qkv, diff_lam_q[o], diff_lam_k[o], diff_subln[o], lam_init)
            qkv_r = _rope_cast(qkv, cos_t, sin_t)
            o_lat = _diff_lat(qkv_r, cache_diff_k[:, o].reshape(DEC_BATCH, PAST_LEN, D_DIFF),
                              cache_diff_v[:, o].reshape(DEC_BATCH, PAST_LEN, D_DIFF),
                              diff_lam_q[o], diff_lam_k[o], diff_subln[o], lam_init)
            x = _out_proj(x, md[2], [jnp.concatenate([o_ctx, o_lat], axis=0)],
                          [w_out_diff[o].astype(BF16)], "out_odd")
            dk_out.append(qkv[:N_CTX_ROWS, D_DIFF:2 * D_DIFF].reshape(BATCH, SEQ, N_HEADS, 2 * HEAD_DIM))
            dv_out.append(qkv[:N_CTX_ROWS, 2 * D_DIFF:].reshape(BATCH, SEQ, N_HEADS, 2 * HEAD_DIM))
        x = _ffn(x, norm_ffn[l], md[3], md[4], md[5], ffn_w1[l].astype(BF16), ffn_w3[l].astype(BF16),
                 ffn_w2[l].astype(BF16), "ffn")

    y = _final_norm(x, norm_final)
    y_prompt = y[:N_CTX_ROWS].reshape(BATCH, SEQ, D_MODEL)
    y_sample = y[N_CTX_ROWS:].reshape(DEC_BATCH, DEC_SEQ, D_MODEL)
    return (y_prompt, y_sample, jnp.stack(na_k_out, axis=1), jnp.stack(na_v_out, axis=1),
            jnp.stack(rw_out, axis=1), jnp.stack(dk_out, axis=1), jnp.stack(dv_out, axis=1))
```

```python
import functools
import math

import jax
import jax.numpy as jnp
from jax import lax
from jax.experimental import pallas as pl
from jax.experimental.pallas import tpu as pltpu

F32 = jnp.float32
BF16 = jnp.bfloat16

D_MODEL = 1024
BATCH = 32
SEQ = 256
DEPTH = 4
DEC_BATCH = 2
DEC_SEQ = 4096
PAST_LEN = 512
GRID_W = 64
GRID_ROWS = DEC_SEQ // GRID_W
HEAD_DIM = 64
N_HEADS = 8
D_NA = 512
D_RWKV = 512
D_DIFF = 1024
NA_KH = 8
NA_KW = 16
DECAY_LORA = 64
ICL_LORA = 64
GATE_LORA = 128
P_RWKV = 3 * D_RWKV + DECAY_LORA + ICL_LORA + GATE_LORA
P_EVEN = 3 * D_NA + P_RWKV
D_FF = 2816
ROPE_F = HEAD_DIM // 4
ROPE_BASE = 10000.0
NORM_EPS = 1e-6
GN_EPS = 64e-5
QK_SCALE = HEAD_DIM ** -0.5

N_CTX_ROWS = BATCH * SEQ
N_LAT_ROWS = DEC_BATCH * DEC_SEQ
M_ALL = N_CTX_ROWS + N_LAT_ROWS

LANES = 128
CHUNK = 64
INV_BLOCK = 16
NEG_BIG = -1e30
VMEM_LIMIT = 56 * 1024 * 1024


def _cparams(sem):
    return pltpu.CompilerParams(dimension_semantics=sem, vmem_limit_bytes=VMEM_LIMIT)


def _bdot(a, b):
    return jnp.dot(a.astype(BF16), b.astype(BF16), preferred_element_type=F32)


def _bdot_nt(a, b):
    return lax.dot_general(a.astype(BF16), b.astype(BF16), (((1,), (1,)), ((), ())),
                           preferred_element_type=F32)


def _bdot_tn(a, b):
    return lax.dot_general(a.astype(BF16), b.astype(BF16), (((0,), (0,)), ((), ())),
                           preferred_element_type=F32)


def _split3(x):
    hi = x.astype(BF16)
    r1 = x - hi.astype(F32)
    mid = r1.astype(BF16)
    lo = (r1 - mid.astype(F32)).astype(BF16)
    return hi, mid, lo


def _dot3_rhs_exact(x, e):
    hi, mid, lo = _split3(x)
    d = functools.partial(jnp.dot, preferred_element_type=F32)
    return d(hi, e) + d(mid, e) + d(lo, e)


def _dot3_lhs_exact(e, x):
    hi, mid, lo = _split3(x)
    d = functools.partial(jnp.dot, preferred_element_type=F32)
    return d(e, hi) + d(e, mid) + d(e, lo)


def _sigmoid(x):
    return 1.0 / (1.0 + jnp.exp(-x))


def _softplus(x):
    return jnp.maximum(x, 0.0) + jnp.log(1.0 + jnp.exp(-jnp.abs(x)))


def _norm_mod(x, g, shift, scale):
    r = lax.rsqrt(jnp.mean(x * x, axis=-1, keepdims=True) + NORM_EPS)
    return ((x * r) * g) * (1.0 + scale) + shift


def _group_of_tile(i, tm):
    n_ctx_tiles = N_CTX_ROWS // tm
    tiles_per_lat = DEC_SEQ // tm
    return jnp.where(i < n_ctx_tiles, 0, 1 + (i - n_ctx_tiles) // tiles_per_lat)


def _mod_spec(tm):
    return pl.BlockSpec((1, 1, D_MODEL), lambda i, *_: (_group_of_tile(i, tm), 0, 0))


def _full_spec(shape):
    n = len(shape)
    return pl.BlockSpec(shape, lambda *_: (0,) * n)


def _ada_kernel(cv_ref, w_ref, b_ref, o_ref):
    cv = cv_ref[...]
    s = cv * _sigmoid(cv)
    o_ref[0] = _bdot(s, w_ref[0]) + b_ref[0]


def _ada_all(cv8, w_ada, b_ada):
    tn = 512
    return pl.pallas_call(
        _ada_kernel,
        grid=(DEPTH, 6 * D_MODEL // tn),
        in_specs=[
            pl.BlockSpec((8, D_MODEL), lambda l, j: (0, 0)),
            pl.BlockSpec((1, D_MODEL, tn), lambda l, j: (l, 0, j)),
            pl.BlockSpec((1, 1, tn), lambda l, j: (l, 0, j)),
        ],
        out_specs=pl.BlockSpec((1, 8, tn), lambda l, j: (l, 0, j)),
        out_shape=jax.ShapeDtypeStruct((DEPTH, 8, 6 * D_MODEL), F32),
        compiler_params=_cparams(("parallel", "parallel")),
        name="adaln",
    )(cv8, w_ada, b_ada.reshape(DEPTH, 1, 6 * D_MODEL))


def _proj_kernel(x_ref, g_ref, sh_ref, sc_ref, w_ref, *o_refs, splits):
    h = _norm_mod(x_ref[...], g_ref[...], sh_ref[0], sc_ref[0]).astype(BF16)
    off = 0
    for o_ref, n in zip(o_refs, splits):
        o_ref[...] = jnp.dot(h, w_ref[:, off:off + n], preferred_element_type=F32)
        off += n


def _norm_proj(x, g, shift, scale, w_bf16, splits, name):
    tm = 512
    n = w_bf16.shape[1]
    return pl.pallas_call(
        functools.partial(_proj_kernel, splits=splits),
        grid=(M_ALL // tm,),
        in_specs=[
            pl.BlockSpec((tm, D_MODEL), lambda i: (i, 0)),
            _full_spec((1, D_MODEL)),
            _mod_spec(tm),
            _mod_spec(tm),
            _full_spec((D_MODEL, n)),
        ],
        out_specs=[pl.BlockSpec((tm, s), lambda i: (i, 0)) for s in splits],
        out_shape=[jax.ShapeDtypeStruct((M_ALL, s), F32) for s in splits],
        compiler_params=_cparams(("parallel",)),
        name=name,
    )(x, g.reshape(1, D_MODEL), shift, scale, w_bf16)


def _part_specs(tm, ncols):
    n_ctx_tiles = N_CTX_ROWS // tm
    return [pl.BlockSpec((tm, ncols), lambda i, *_: (jnp.minimum(i, n_ctx_tiles - 1), 0)),
            pl.BlockSpec((tm, ncols), lambda i, *_: (jnp.maximum(i - n_ctx_tiles, 0), 0))]


def _pick_part(i, tm, ctx_ref, lat_ref):
    return jnp.where(i < N_CTX_ROWS // tm, ctx_ref[...], lat_ref[...])


def _out_kernel(x_ref, gate_ref, *refs, split, tm):
    i = pl.program_id(0)
    n_act = sum(2 if s else 1 for s in split)
    a_refs, w_refs, o_ref = refs[:n_act], refs[n_act:n_act + len(split)], refs[n_act + len(split)]
    acc = None
    pos = 0
    for is_split, w_ref in zip(split, w_refs):
        if is_split:
            a = _pick_part(i, tm, a_refs[pos], a_refs[pos + 1])
            pos += 2
        else:
            a = a_refs[pos][...]
            pos += 1
        t = jnp.dot(a, w_ref[...], preferred_element_type=F32)
        acc = t if acc is None else acc + t
    o_ref[...] = x_ref[...] + gate_ref[0] * acc


def _out_proj(x, gate, acts, ws, name):
    tm = 512
    in_specs = [pl.BlockSpec((tm, D_MODEL), lambda i: (i, 0)), _mod_spec(tm)]
    flat, split = [], []
    for a in acts:
        if isinstance(a, tuple):
            in_specs += _part_specs(tm, a[0].shape[1])
            flat += list(a)
            split.append(True)
        else:
            in_specs.append(pl.BlockSpec((tm, a.shape[1]), lambda i: (i, 0)))
            flat.append(a)
            split.append(False)
    return pl.pallas_call(
        functools.partial(_out_kernel, split=tuple(split), tm=tm),
        grid=(M_ALL // tm,),
        in_specs=in_specs + [_full_spec(w.shape) for w in ws],
        out_specs=pl.BlockSpec((tm, D_MODEL), lambda i: (i, 0)),
        out_shape=jax.ShapeDtypeStruct((M_ALL, D_MODEL), F32),
        compiler_params=_cparams(("parallel",)),
        name=name,
    )(x, gate, *flat, *ws)


def _ffn_kernel(x_ref, g_ref, sh_ref, sc_ref, gate_ref, w1_ref, w3_ref, w2_ref, o_ref, h_ref, acc_ref):
    f = pl.program_id(1)

    @pl.when(f == 0)
    def _():
        h_ref[...] = _norm_mod(x_ref[...], g_ref[...], sh_ref[0], sc_ref[0]).astype(BF16)

    h = h_ref[...]
    a = jnp.dot(h, w1_ref[...], preferred_element_type=F32)
    b = jnp.dot(h, w3_ref[...], preferred_element_type=F32)
    gated = ((a * _sigmoid(a)) * b).astype(BF16)
    part = jnp.dot(gated, w2_ref[...], preferred_element_type=F32)

    @pl.when(f == 0)
    def _():
        acc_ref[...] = part

    @pl.when(f > 0)
    def _():
        acc_ref[...] += part

    @pl.when(f == pl.num_programs(1) - 1)
    def _():
        o_ref[...] = x_ref[...] + gate_ref[0] * acc_ref[...]


def _ffn(x, g, shift, scale, gate, w1, w3, w2, name):
    tm, tf = 1024, 256
    return pl.pallas_call(
        _ffn_kernel,
        grid=(M_ALL // tm, D_FF // tf),
        in_specs=[
            pl.BlockSpec((tm, D_MODEL), lambda i, f: (i, 0)),
            _full_spec((1, D_MODEL)),
            _mod_spec(tm),
            _mod_spec(tm),
            _mod_spec(tm),
            pl.BlockSpec((D_MODEL, tf), lambda i, f: (0, f)),
            pl.BlockSpec((D_MODEL, tf), lambda i, f: (0, f)),
            pl.BlockSpec((tf, D_MODEL), lambda i, f: (f, 0)),
        ],
        out_specs=pl.BlockSpec((tm, D_MODEL), lambda i, f: (i, 0)),
        out_shape=jax.ShapeDtypeStruct((M_ALL, D_MODEL), F32),
        scratch_shapes=[pltpu.VMEM((tm, D_MODEL), BF16), pltpu.VMEM((tm, D_MODEL), F32)],
        compiler_params=_cparams(("parallel", "arbitrary")),
        name=name,
    )(x, g.reshape(1, D_MODEL), shift, scale, gate, w1, w3, w2)


def _final_norm_kernel(x_ref, g_ref, o_ref):
    x = x_ref[...]
    r = lax.rsqrt(jnp.mean(x * x, axis=-1, keepdims=True) + NORM_EPS)
    o_ref[...] = (x * r) * g_ref[...]


def _final_norm(x, g):
    tm = 1024
    return pl.pallas_call(
        _final_norm_kernel,
        grid=(M_ALL // tm,),
        in_specs=[pl.BlockSpec((tm, D_MODEL), lambda i: (i, 0)), _full_spec((1, D_MODEL))],
        out_specs=pl.BlockSpec((tm, D_MODEL), lambda i: (i, 0)),
        out_shape=jax.ShapeDtypeStruct((M_ALL, D_MODEL), F32),
        compiler_params=_cparams(("parallel",)),
        name="final_norm",
    )(x, g.reshape(1, D_MODEL))


def _half_masks():
    lane = lax.broadcasted_iota(jnp.int32, (1, LANES), 1)
    return (lane < HEAD_DIM, lane >= HEAD_DIM)


def _na_ctx_kernel(q_ref, k_ref, v_ref, o_ref):
    q = q_ref[...] * QK_SCALE
    k = k_ref[...].astype(BF16)
    v = v_ref[...]
    out = jnp.zeros(o_ref.shape, F32)
    for m in _half_masks():
        s = _bdot_nt(jnp.where(m, q, 0.0), k)
        e = jnp.exp(s - jnp.max(s, axis=-1, keepdims=True))
        inv_l = 1.0 / jnp.sum(e, axis=-1, keepdims=True)
        out = out + _bdot(e, jnp.where(m, v, 0.0)) * inv_l
    o_ref[...] = out.astype(BF16)


def _na_ctx(qkv):
    n_pairs = D_NA // LANES
    return pl.pallas_call(
        _na_ctx_kernel,
        grid=(BATCH, n_pairs),
        in_specs=[
            pl.BlockSpec((SEQ, LANES), lambda b, p: (b, p)),
            pl.BlockSpec((SEQ, LANES), lambda b, p: (b, n_pairs + p)),
            pl.BlockSpec((SEQ, LANES), lambda b, p: (b, 2 * n_pairs + p)),
        ],
        out_specs=pl.BlockSpec((SEQ, LANES), lambda b, p: (b, p)),
        out_shape=jax.ShapeDtypeStruct((N_CTX_ROWS, D_NA), BF16),
        compiler_params=_cparams(("parallel", "parallel")),
        name="na_ctx",
    )(qkv, qkv, qkv)


def _na_lat_kernel(q_ref, k_ref, v_ref, kc_ref, vc_ref, bias_ref, o_ref):
    i = pl.program_id(2)
    r0 = jnp.clip(i - NA_KH // 2, 0, GRID_ROWS - NA_KH)
    start = pl.multiple_of(r0 * GRID_W, GRID_W)
    n_loc = NA_KH * GRID_W
    q = q_ref[...] * QK_SCALE
    kl = k_ref[pl.ds(start, n_loc), :].astype(BF16)
    vl = v_ref[pl.ds(start, n_loc), :]
    kc = kc_ref[0].astype(BF16)
    vc = vc_ref[0]
    out = jnp.zeros(o_ref.shape, F32)
    for hh, m in enumerate(_half_masks()):
        qm = jnp.where(m, q, 0.0)
        sl = _bdot_nt(qm, kl) + bias_ref[0, hh]
        sc = _bdot_nt(qm, kc)
        mx = jnp.maximum(jnp.max(sl, axis=-1, keepdims=True), jnp.max(sc, axis=-1, keepdims=True))
        el = jnp.exp(sl - mx)
        ec = jnp.exp(sc - mx)
        inv_l = 1.0 / (jnp.sum(el, axis=-1, keepdims=True) + jnp.sum(ec, axis=-1, keepdims=True))
        pv = _bdot(el, jnp.where(m, vl, 0.0)) + _bdot(ec, jnp.where(m, vc, 0.0))
        out = out + pv * inv_l
    o_ref[...] = out.astype(BF16)


def _na_bias_table(rpb):
    jj = jnp.arange(GRID_W)[:, None]
    cc = jnp.arange(GRID_W)[None, :]
    c0 = jnp.clip(jj - NA_KW // 2, 0, GRID_W - NA_KW)
    inwin = (cc >= c0) & (cc < c0 + NA_KW)
    idx = jnp.clip(cc - jj + NA_KW - 1, 0, 2 * NA_KW - 2)
    full = rpb[:, :, idx]
    full = jnp.where(inwin[None, None], full, NEG_BIG)
    tabs = []
    for v in range(NA_KH):
        t = full[:, v:v + NA_KH]
        tabs.append(t.transpose(0, 2, 1, 3).reshape(N_HEADS, GRID_W, NA_KH * GRID_W))
    return jnp.stack(tabs, axis=0)


def _na_lat(qkv, kc, vc, bias_tab):
    n_pairs = D_NA // LANES
    lat_blk = N_CTX_ROWS // DEC_SEQ
    row_blk = N_CTX_ROWS // GRID_W

    def variant(i):
        return jnp.clip(i - NA_KH // 2, 0, GRID_ROWS - NA_KH) - i + NA_KH - 1

    return pl.pallas_call(
        _na_lat_kernel,
        grid=(DEC_BATCH, n_pairs, GRID_ROWS),
        in_specs=[
            pl.BlockSpec((GRID_W, LANES), lambda b, p, i: (row_blk + b * GRID_ROWS + i, p)),
            pl.BlockSpec((DEC_SEQ, LANES), lambda b, p, i: (lat_blk + b, n_pairs + p)),
            pl.BlockSpec((DEC_SEQ, LANES), lambda b, p, i: (lat_blk + b, 2 * n_pairs + p)),
            pl.BlockSpec((1, PAST_LEN, LANES), lambda b, p, i: (b, 0, p)),
            pl.BlockSpec((1, PAST_LEN, LANES), lambda b, p, i: (b, 0, p)),
            pl.BlockSpec((1, 2, GRID_W, NA_KH * GRID_W), lambda b, p, i: (variant(i), p, 0, 0)),
        ],
        out_specs=pl.BlockSpec((GRID_W, LANES), lambda b, p, i: (b * GRID_ROWS + i, p)),
        out_shape=jax.ShapeDtypeStruct((N_LAT_ROWS, D_NA), BF16),
        compiler_params=_cparams(("parallel", "parallel", "arbitrary")),
        name="na_lat",
    )(qkv, qkv, qkv, kc, vc, bias_tab)


def _lam_value(lq_ref, lk_ref, lam_init):
    s = jnp.sum(lq_ref[...] * lk_ref[...], axis=-1, keepdims=True)
    return jnp.exp(s[0:1]) - jnp.exp(s[1:2]) + lam_init


def _subln(o, sub, lam_init):
    r = lax.rsqrt(jnp.mean(o * o, axis=-1, keepdims=True) + NORM_EPS)
    return ((o * r) * sub) * (1.0 - lam_init)


def _diff_ctx_kernel(lq_ref, lk_ref, sub_ref, q_ref, k_ref, v_ref, o_ref, *, lam_init):
    lam = _lam_value(lq_ref, lk_ref, lam_init)
    q = q_ref[...] * QK_SCALE
    k = k_ref[...].astype(BF16)
    p = []
    for m in _half_masks():
        s = _bdot_nt(jnp.where(m, q, 0.0), k)
        e = jnp.exp(s - jnp.max(s, axis=-1, keepdims=True))
        p.append(e * (1.0 / jnp.sum(e, axis=-1, keepdims=True)))
    o = _bdot(p[0] - lam * p[1], v_ref[...])
    o_ref[...] = _subln(o, sub_ref[...], lam_init).astype(BF16)


def _diff_ctx(qkv, lam_q, lam_k, subln, lam_init):
    return pl.pallas_call(
        functools.partial(_diff_ctx_kernel, lam_init=lam_init),
        grid=(BATCH, N_HEADS),
        in_specs=[
            _full_spec((2, HEAD_DIM)),
            _full_spec((2, HEAD_DIM)),
            _full_spec((1, LANES)),
            pl.BlockSpec((SEQ, LANES), lambda b, h: (b, h)),
            pl.BlockSpec((SEQ, LANES), lambda b, h: (b, N_HEADS + h)),
            pl.BlockSpec((SEQ, LANES), lambda b, h: (b, 2 * N_HEADS + h)),
        ],
        out_specs=pl.BlockSpec((SEQ, LANES), lambda b, h: (b, h)),
        out_shape=jax.ShapeDtypeStruct((N_CTX_ROWS, D_DIFF), BF16),
        compiler_params=_cparams(("parallel", "parallel")),
        name="diff_ctx",
    )(lam_q, lam_k, subln.reshape(1, LANES), qkv, qkv, qkv)


def _rope_kernel(x_ref, cos_ref, sin_ref, o_ref):
    cos = cos_ref[...]
    sin = sin_ref[...]
    lane = lax.broadcasted_iota(jnp.int32, (1, LANES), 1)
    first = (lane % (2 * ROPE_F)) < ROPE_F
    n_q = D_DIFF // LANES
    for j in range(2 * n_q):
        x = x_ref[:, j * LANES:(j + 1) * LANES]
        partner = jnp.where(first, pltpu.roll(x, LANES - ROPE_F, 1), pltpu.roll(x, ROPE_F, 1))
        y = x * cos + partner * sin
        if j < n_q:
            y = y * QK_SCALE
        o_ref[:, j * LANES:(j + 1) * LANES] = y.astype(BF16)
    o_ref[:, 2 * D_DIFF:] = x_ref[:, 2 * D_DIFF:].astype(BF16)


def _rope_tables():
    t = jnp.arange(DEC_SEQ)
    pos = jnp.stack([t // GRID_W, t % GRID_W], -1).astype(F32)
    inv = ROPE_BASE ** (-jnp.arange(ROPE_F, dtype=F32) / ROPE_F)
    ang = pos[:, :, None] * inv
    cos, sin = jnp.cos(ang), jnp.sin(ang)
    cos64 = jnp.concatenate([cos, cos], axis=-1).reshape(DEC_SEQ, HEAD_DIM)
    sin64 = jnp.concatenate([-sin, sin], axis=-1).reshape(DEC_SEQ, HEAD_DIM)
    return jnp.tile(cos64, (1, 2)), jnp.tile(sin64, (1, 2))


def _rope_cast(qkv, cos_t, sin_t):
    tm = 512
    base = N_CTX_ROWS // tm
    per_seq = DEC_SEQ // tm
    return pl.pallas_call(
        _rope_kernel,
        grid=(N_LAT_ROWS // tm,),
        in_specs=[
            pl.BlockSpec((tm, 3 * D_DIFF), lambda i: (base + i, 0)),
            pl.BlockSpec((tm, LANES), lambda i: (i % per_seq, 0)),
            pl.BlockSpec((tm, LANES), lambda i: (i % per_seq, 0)),
        ],
        out_specs=pl.BlockSpec((tm, 3 * D_DIFF), lambda i: (i, 0)),
        out_shape=jax.ShapeDtypeStruct((N_LAT_ROWS, 3 * D_DIFF), BF16),
        compiler_params=_cparams(("parallel",)),
        name="rope_cast",
    )(qkv, cos_t, sin_t)


def _diff_lat_kernel(lq_ref, lk_ref, sub_ref, q_ref, kl_ref, vl_ref, kc_ref, vc_ref, o_ref, *, lam_init):
    lam = _lam_value(lq_ref, lk_ref, lam_init)
    q = q_ref[...]
    kl = kl_ref[...]
    kc = kc_ref[0].astype(BF16)
    pl_, pc_ = [], []
    for m in _half_masks():
        qm = jnp.where(m, q, jnp.zeros_like(q))
        sl = _bdot_nt(qm, kl)
        sc = _bdot_nt(qm, kc)
        mx = jnp.maximum(jnp.max(sl, axis=-1, keepdims=True), jnp.max(sc, axis=-1, keepdims=True))
        el = jnp.exp(sl - mx)
        ec = jnp.exp(sc - mx)
        inv_l = 1.0 / (jnp.sum(el, axis=-1, keepdims=True) + jnp.sum(ec, axis=-1, keepdims=True))
        pl_.append(el * inv_l)
        pc_.append(ec * inv_l)
    o = _bdot(pl_[0] - lam * pl_[1], vl_ref[...]) + _bdot(pc_[0] - lam * pc_[1], vc_ref[0])
    o_ref[...] = _subln(o, sub_ref[...], lam_init).astype(BF16)


def _diff_lat(qkv_r, kc, vc, lam_q, lam_k, subln, lam_init):
    tq = 256
    per_seq = DEC_SEQ // tq
    return pl.pallas_call(
        functools.partial(_diff_lat_kernel, lam_init=lam_init),
        grid=(DEC_BATCH, N_HEADS, per_seq),
        in_specs=[
            _full_spec((2, HEAD_DIM)),
            _full_spec((2, HEAD_DIM)),
            _full_spec((1, LANES)),
            pl.BlockSpec((tq, LANES), lambda b, h, i: (b * per_seq + i, h)),
            pl.BlockSpec((DEC_SEQ, LANES), lambda b, h, i: (b, N_HEADS + h)),
            pl.BlockSpec((DEC_SEQ, LANES), lambda b, h, i: (b, 2 * N_HEADS + h)),
            pl.BlockSpec((1, PAST_LEN, LANES), lambda b, h, i: (b, 0, h)),
            pl.BlockSpec((1, PAST_LEN, LANES), lambda b, h, i: (b, 0, h)),
        ],
        out_specs=pl.BlockSpec((tq, LANES), lambda b, h, i: (b * per_seq + i, h)),
        out_shape=jax.ShapeDtypeStruct((N_LAT_ROWS, D_DIFF), BF16),
        compiler_params=_cparams(("parallel", "parallel", "arbitrary")),
        name="diff_lat",
    )(lam_q, lam_k, subln.reshape(1, LANES), qkv_r, qkv_r, qkv_r, kc, vc)


def _token_shift(u, prev_row, next_row, mu):
    n = u.shape[0]
    rows = lax.broadcasted_iota(jnp.int32, (n, 1), 0)
    up = jnp.where(rows == 0, prev_row, pltpu.roll(u, 1, 0))
    un = jnp.where(rows == n - 1, next_row, pltpu.roll(u, n - 1, 0))
    return u + mu[0:1] * (up - u) + mu[1:2] * (un - u)


def _icl_rate_and_key(us, a0, a2p, k_a):
    lo = us[:, 3 * D_RWKV:3 * D_RWKV + LANES]
    k = us[:, D_RWKV:2 * D_RWKV]
    a = _sigmoid(a0 + _bdot(lo, a2p))
    return a, k * (1.0 + (a - 1.0) * k_a)


def _stack_blockdiag(x):
    shape = (2 * x.shape[0], x.shape[1])
    row = lax.broadcasted_iota(jnp.int32, shape, 0)
    col = lax.broadcasted_iota(jnp.int32, shape, 1)
    keep = (row // HEAD_DIM) == ((col % LANES) // HEAD_DIM)
    return jnp.where(keep, jnp.concatenate([x, x], axis=0), 0.0).astype(BF16)


def _rwkv_chunk_inputs(u, prev_row, next_row, reverse, mu, w0, w2p, a0, a2p, k_k, k_a, bd_ones):
    us = _token_shift(u, prev_row, next_row, mu)
    r = us[:, 0:D_RWKV]
    k = us[:, D_RWKV:2 * D_RWKV]
    v = us[:, 2 * D_RWKV:3 * D_RWKV]
    lo = us[:, 3 * D_RWKV:3 * D_RWKV + LANES]
    w_raw = w0 + _bdot(jnp.tanh(lo), w2p)
    logw = -jnp.exp(-_softplus(-w_raw) - 0.5)
    a, kt = _icl_rate_and_key(us, a0, a2p, k_a)
    kk_raw = k * k_k
    ss = _dot3_rhs_exact(kk_raw * kk_raw, bd_ones)
    kk = kk_raw * lax.rsqrt(jnp.maximum(ss, 1e-12))
    row = lax.broadcasted_iota(jnp.int32, (CHUNK, CHUNK), 0)
    col = lax.broadcasted_iota(jnp.int32, (CHUNK, CHUNK), 1)
    seen = (col >= row) if reverse else (col <= row)
    cum = _dot3_lhs_exact(seen.astype(F32).astype(BF16), logw)
    p_inv = jnp.exp(-cum)
    tot = cum[0:1, :] if reverse else cum[CHUNK - 1:CHUNK, :]
    return dict(a_hat=jnp.exp(cum - logw) * kk, b_hat=(kk * a) * p_inv, k_hat=kt * p_inv,
                r_hat=r * jnp.exp(cum), v=v, p_tot=jnp.exp(tot))


def _rwkv_scan_kernel(*refs, n_chunks, n_par, has_state):
    u_refs = (refs[0:3], refs[3:6])
    pos = 6
    s0_ref = None
    if has_state:
        s0_ref = refs[pos]
        pos += 1
    mu_ref, w0_ref, w2_ref, a0_ref, a2_ref, kk_ref, ka_ref, bd_ref = refs[pos:pos + 8]
    y_refs = refs[pos + 8:pos + 10]
    sfin_ref, s_ref = refs[pos + 10], refs[pos + 11]
    c = pl.program_id(1)
    n_pairs = D_RWKV // LANES

    @pl.when(c == 0)
    def _():
        if has_state:
            s_ref[...] = s0_ref[...]
        else:
            s_ref[...] = jnp.zeros(s_ref.shape, F32)

    at_start = jnp.where(c == 0, 0.0, 1.0)
    at_end = jnp.where(c == n_chunks - 1, 0.0, 1.0)
    has_prev = (at_start, at_end)
    has_next = (at_end, at_start)

    row = lax.broadcasted_iota(jnp.int32, (CHUNK, LANES), 0)
    colm = lax.broadcasted_iota(jnp.int32, (CHUNK, LANES), 1) % CHUNK
    strict = (colm < row, colm > row)
    incl = (colm <= row, colm >= row)
    same_blk = (row // INV_BLOCK) == (colm // INV_BLOCK)
    eye = (row == colm).astype(F32)
    prow = lax.broadcasted_iota(jnp.int32, (LANES, LANES), 0)
    pcol = lax.broadcasted_iota(jnp.int32, (LANES, LANES), 1)
    blockdiag = (prow // HEAD_DIM) == (pcol // HEAD_DIM)
    eye_pair = (prow == pcol).astype(F32)

    chains = []
    for j in range(n_par):
        for d in range(2):
            u_ref, up_ref, un_ref = u_refs[d]
            q = _rwkv_chunk_inputs(
                u_ref[j], up_ref[j, 7:8, :] * has_prev[d], un_ref[j, 0:1, :] * has_next[d], d == 1,
                mu_ref[...], w0_ref[d], w2_ref[d], a0_ref[d], a2_ref[d], kk_ref[...], ka_ref[...], bd_ref[...])
            for p in range(n_pairs):
                sl = slice(p * LANES, (p + 1) * LANES)
                chains.append(dict(j=j, d=d, p=p, sl=sl, ah=q["a_hat"][:, sl], bh=q["b_hat"][:, sl],
                                   kh=q["k_hat"][:, sl], rh=q["r_hat"][:, sl], vv=q["v"][:, sl],
                                   pt=q["p_tot"][:, sl]))

    def stage(fn):
        for ch in chains:
            fn(ch)

    def s_scores(ch):
        ar = jnp.concatenate([ch["ah"], ch["rh"]], axis=0)
        xb = _bdot_nt(ar, _stack_blockdiag(ch["bh"]))
        xk = _bdot_nt(ar, _stack_blockdiag(ch["kh"]))
        d = ch["d"]
        l_ab = jnp.where(strict[d], xb[:CHUNK], 0.0)
        ch["l_ak"] = jnp.where(strict[d], xk[:CHUNK], 0.0)
        ch["m_rb"] = jnp.where(incl[d], xb[CHUNK:], 0.0)
        ch["m_rk"] = jnp.where(incl[d], xk[CHUNK:], 0.0)
        ch["x1"] = -jnp.where(same_blk, l_ab, 0.0)
        ch["l_off"] = jnp.where(same_blk, 0.0, l_ab)
        ch["vs"] = _stack_blockdiag(ch["vv"])
    stage(s_scores)

    def s_x2(ch):
        ch["x2"] = _bdot(ch["x1"], _stack_blockdiag(ch["x1"]))
        ch["x2s"] = _stack_blockdiag(ch["x2"])
        ch["u"] = _bdot(ch["l_ak"], ch["vs"])
    stage(s_x2)

    def s_x4(ch):
        ch["x4"] = _bdot(ch["x2"], ch["x2s"])
        td = eye + ch["x1"]
        ch["td"] = td + _bdot(td, ch["x2s"])
    stage(s_x4)

    def s_x8(ch):
        x4s = _stack_blockdiag(ch["x4"])
        ch["x8s"] = _stack_blockdiag(_bdot(ch["x4"], x4s))
        ch["td"] = ch["td"] + _bdot(ch["td"], x4s)
    stage(s_x8)

    def s_td(ch):
        ch["td"] = ch["td"] + _bdot(ch["td"], ch["x8s"])
    stage(s_td)

    def s_mm(ch):
        ch["mm"] = _bdot(ch["td"], _stack_blockdiag(ch["l_off"]))
    stage(s_mm)

    def s_m2(ch):
        ch["m2s"] = _stack_blockdiag(_bdot(ch["mm"], _stack_blockdiag(ch["mm"])))
    stage(s_m2)

    def s_n2(ch):
        n1 = eye - ch["mm"]
        ch["n2"] = n1 + _bdot(n1, ch["m2s"])
    stage(s_n2)

    def s_tinv(ch):
        ch["t_inv"] = _bdot(ch["n2"], _stack_blockdiag(ch["td"]))
    stage(s_tinv)

    def s_tx(ch):
        ch["tx"] = _bdot(ch["t_inv"], _stack_blockdiag(jnp.concatenate([ch["ah"], ch["u"]], axis=1)))
    stage(s_tx)

    def s_out(ch):
        tx = ch["tx"]
        mx = _bdot(ch["m_rb"], _stack_blockdiag(tx))
        q_eff = ch["rh"] - mx[:, :LANES]
        y_loc = _bdot(ch["m_rk"], ch["vs"]) - mx[:, LANES:]
        g_mat = jnp.where(blockdiag, eye_pair - _bdot_tn(tx[:, :LANES], ch["bh"]), 0.0) * ch["pt"]
        h_mat = jnp.where(
            blockdiag,
            _bdot_tn(jnp.concatenate([ch["vv"], -tx[:, LANES:]], axis=0),
                     jnp.concatenate([ch["kh"], ch["bh"]], axis=0)),
            0.0) * ch["pt"]
        j, d, p = ch["j"], ch["d"], ch["p"]
        s_old = s_ref[j, d, p]
        y_refs[d][j, :, ch["sl"]] = _bdot_nt(q_eff, s_old) + y_loc
        s_ref[j, d, p] = _bdot(s_old, g_mat) + h_mat
    stage(s_out)

    @pl.when(c == n_chunks - 1)
    def _():
        sfin_ref[...] = s_ref[...]


def _rwkv_scan(u, s0p, prm, row_base, n_seq, seq_len):
    n_par = 2
    n_chunks = seq_len // CHUNK
    n_pairs = D_RWKV // LANES
    base_blk = row_base // (seq_len * n_par)
    n_blk8 = seq_len // 8
    has_state = s0p is not None
    u3 = u.reshape(M_ALL // seq_len, seq_len, P_RWKV)

    def chunk_pos(d, c):
        return c if d == 0 else n_chunks - 1 - c

    in_specs, args = [], []
    for d in range(2):
        in_specs += [
            pl.BlockSpec((n_par, CHUNK, P_RWKV), lambda g, c, d=d: (base_blk + g, chunk_pos(d, c), 0)),
            pl.BlockSpec((n_par, 8, P_RWKV),
                         lambda g, c, d=d: (base_blk + g, jnp.maximum(chunk_pos(d, c) * (CHUNK // 8) - 1, 0), 0)),
            pl.BlockSpec((n_par, 8, P_RWKV),
                         lambda g, c, d=d: (base_blk + g,
                                            jnp.minimum((chunk_pos(d, c) + 1) * (CHUNK // 8), n_blk8 - 1), 0)),
        ]
        args += [u3, u3, u3]
    state_spec = pl.BlockSpec((n_par, 2, n_pairs, LANES, LANES), lambda g, c: (g, 0, 0, 0, 0))
    if has_state:
        in_specs.append(state_spec)
        args.append(s0p)
    in_specs += [
        _full_spec((2, P_RWKV)),
        _full_spec((2, 1, D_RWKV)),
        _full_spec((2, LANES, D_RWKV)),
        _full_spec((2, 1, D_RWKV)),
        _full_spec((2, LANES, D_RWKV)),
        _full_spec((1, D_RWKV)),
        _full_spec((1, D_RWKV)),
        _full_spec((D_RWKV, D_RWKV)),
    ]
    args += [prm["mu"], prm["w0"], prm["w2p"], prm["a0"], prm["a2p"], prm["kk"], prm["ka"], prm["bd_ones"]]
    y_shape = jax.ShapeDtypeStruct((n_seq, seq_len, D_RWKV), F32)
    y_f, y_b, s_fin = pl.pallas_call(
        functools.partial(_rwkv_scan_kernel, n_chunks=n_chunks, n_par=n_par, has_state=has_state),
        grid=(n_seq // n_par, n_chunks),
        in_specs=in_specs,
        out_specs=[
            pl.BlockSpec((n_par, CHUNK, D_RWKV), lambda g, c: (g, chunk_pos(0, c), 0)),
            pl.BlockSpec((n_par, CHUNK, D_RWKV), lambda g, c: (g, chunk_pos(1, c), 0)),
            state_spec,
        ],
        out_shape=[y_shape, y_shape, jax.ShapeDtypeStruct((n_seq, 2, n_pairs, LANES, LANES), F32)],
        scratch_shapes=[pltpu.VMEM((n_par, 2, n_pairs, LANES, LANES), F32)],
        compiler_params=_cparams(("parallel", "arbitrary")),
        name="rwkv_scan_" + ("lat" if has_state else "ctx"),
    )(*args)
    n_rows = n_seq * seq_len
    return (y_f.reshape(n_rows, D_RWKV), y_b.reshape(n_rows, D_RWKV)), s_fin


def _rwkv_fin_kernel(u_ref, up_ref, un_ref, ycf_ref, ylf_ref, ycb_ref, ylb_ref, mu_ref, a0_ref, a2_ref,
                     ka_ref, bonus_ref, g2_ref, lnw_ref, lnb_ref, bd1_ref, bdm_ref, o_ref, *, tm):
    i = pl.program_id(0)
    n_ctx_tiles = N_CTX_ROWS // tm
    per_seq = DEC_SEQ // tm
    j = (i - n_ctx_tiles) % per_seq
    is_ctx = i < n_ctx_tiles
    not_first = jnp.where(is_ctx | (j == 0), 0.0, 1.0)
    not_last = jnp.where(is_ctx | (j == per_seq - 1), 0.0, 1.0)
    us = _token_shift(u_ref[...], up_ref[7:8, :] * not_first, un_ref[0:1, :] * not_last, mu_ref[...])
    r = us[:, 0:D_RWKV]
    v = us[:, 2 * D_RWKV:3 * D_RWKV]
    g_lo = us[:, 3 * D_RWKV + LANES:]
    bon = jnp.zeros((tm, D_RWKV), F32)
    for d in range(2):
        _, kt = _icl_rate_and_key(us, a0_ref[d], a2_ref[d], ka_ref[...])
        bon = bon + _dot3_rhs_exact(r * kt * bonus_ref[d], bd1_ref[...])
    bon = bon * v
    y = _pick_part(i, tm, ycf_ref, ylf_ref) + _pick_part(i, tm, ycb_ref, ylb_ref)
    mean = _dot3_rhs_exact(y, bdm_ref[...])
    yc = y - mean
    var = _dot3_rhs_exact(yc * yc, bdm_ref[...])
    yn = (yc * lax.rsqrt(var + GN_EPS)) * lnw_ref[...] + lnb_ref[...]
    gate = _bdot(_sigmoid(g_lo), g2_ref[...])
    o_ref[...] = ((yn + bon) * gate).astype(BF16)


def _rwkv_finish(u, y_ctx, y_lat, prm):
    tm = 256
    n_blk8 = M_ALL // 8
    return pl.pallas_call(
        functools.partial(_rwkv_fin_kernel, tm=tm),
        grid=(M_ALL // tm,),
        in_specs=[
            pl.BlockSpec((tm, P_RWKV), lambda i: (i, 0)),
            pl.BlockSpec((8, P_RWKV), lambda i: (jnp.maximum(i * (tm // 8) - 1, 0), 0)),
            pl.BlockSpec((8, P_RWKV), lambda i: (jnp.minimum((i + 1) * (tm // 8), n_blk8 - 1), 0)),
            *_part_specs(tm, D_RWKV),
            *_part_specs(tm, D_RWKV),
            _full_spec((2, P_RWKV)),
            _full_spec((2, 1, D_RWKV)),
            _full_spec((2, LANES, D_RWKV)),
            _full_spec((1, D_RWKV)),
            _full_spec((2, 1, D_RWKV)),
            _full_spec((GATE_LORA, D_RWKV)),
            _full_spec((1, D_RWKV)),
            _full_spec((1, D_RWKV)),
            _full_spec((D_RWKV, D_RWKV)),
            _full_spec((D_RWKV, D_RWKV)),
        ],
        out_specs=pl.BlockSpec((tm, D_RWKV), lambda i: (i, 0)),
        out_shape=jax.ShapeDtypeStruct((M_ALL, D_RWKV), BF16),
        compiler_params=_cparams(("parallel",)),
        name="rwkv_finish",
    )(u, u, u, y_ctx[0], y_lat[0], y_ctx[1], y_lat[1], prm["mu"], prm["a0"], prm["a2p"], prm["ka"],
      prm["bonus"], prm["g2"], prm["lnw"], prm["lnb"], prm["bd_ones"], prm["bd_mean"])


def _pack_state_pairs(s):
    lead = s.shape[:-3]
    s = s.reshape(lead + (N_HEADS // 2, 2, HEAD_DIM, HEAD_DIM))
    z = jnp.zeros_like(s[..., 0, :, :])
    top = jnp.concatenate([s[..., 0, :, :], z], axis=-1)
    bot = jnp.concatenate([z, s[..., 1, :, :]], axis=-1)
    return jnp.concatenate([top, bot], axis=-2)


def _unpack_state_pairs(sp):
    lead = sp.shape[:-3]
    a = sp[..., :HEAD_DIM, :HEAD_DIM]
    b = sp[..., HEAD_DIM:, HEAD_DIM:]
    return jnp.stack([a, b], axis=-3).reshape(lead + (N_HEADS, HEAD_DIM, HEAD_DIM))


def kernel(x_prompt, x_sample, c, cache_na_k, cache_na_v, state_rwkv, cache_diff_k, cache_diff_v, c_ctx, w_ada, b_ada, norm_mix, norm_ffn, norm_final, w_in_even, w_out_even, na_rpb, rw_mu, rw_w0, rw_w2, rw_a0, rw_a2, rw_kk, rw_ka, rw_bonus, rw_g2, rw_lnw, rw_lnb, w_qkv_diff, w_out_diff, diff_lam_q, diff_lam_k, diff_subln, ffn_w1, ffn_w3, ffn_w2):
    x = jnp.concatenate([x_prompt.reshape(N_CTX_ROWS, D_MODEL), x_sample.reshape(N_LAT_ROWS, D_MODEL)], axis=0)

    cv8 = jnp.concatenate([c_ctx[None, :], c, jnp.zeros((8 - 1 - DEC_BATCH, D_MODEL), F32)], axis=0)
    mods = _ada_all(cv8, w_ada, b_ada)
    mods = mods.reshape(DEPTH, 8, 6, D_MODEL).transpose(0, 2, 1, 3)[:, :, :1 + DEC_BATCH, None, :]

    hd_idx = jnp.arange(D_RWKV) // HEAD_DIM
    bd_ones = (hd_idx[:, None] == hd_idx[None, :]).astype(BF16)
    bd_mean = (bd_ones.astype(F32) / HEAD_DIM).astype(BF16)
    cos_t, sin_t = _rope_tables()

    na_k_out, na_v_out, rw_out, dk_out, dv_out = [], [], [], [], []
    for l in range(DEPTH):
        md = mods[l]
        if l % 2 == 0:
            e = l // 2
            qkv, u = _norm_proj(x, norm_mix[l], md[0], md[1], w_in_even[e].astype(BF16),
                                (3 * D_NA, P_RWKV), "proj_even")
            o_ctx = _na_ctx(qkv)
            o_lat = _na_lat(qkv, cache_na_k[:, e].reshape(DEC_BATCH, PAST_LEN, D_NA),
                            cache_na_v[:, e].reshape(DEC_BATCH, PAST_LEN, D_NA), _na_bias_table(na_rpb[e]))

            zpad = jnp.zeros((2, LANES - DECAY_LORA, D_RWKV), F32)
            prm = {
                "mu": rw_mu[e],
                "w0": rw_w0[e].reshape(2, 1, D_RWKV),
                "w2p": jnp.concatenate([rw_w2[e], zpad], axis=1),
                "a0": rw_a0[e].reshape(2, 1, D_RWKV),
                "a2p": jnp.concatenate([zpad, rw_a2[e]], axis=1),
                "kk": rw_kk[e].reshape(1, D_RWKV),
                "ka": rw_ka[e].reshape(1, D_RWKV),
                "bonus": rw_bonus[e].reshape(2, 1, D_RWKV),
                "g2": rw_g2[e],
                "lnw": rw_lnw[e].reshape(1, D_RWKV),
                "lnb": rw_lnb[e].reshape(1, D_RWKV),
                "bd_ones": bd_ones,
                "bd_mean": bd_mean,
            }
            y_ctx, s_ctx = _rwkv_scan(u, None, prm, 0, BATCH, SEQ)
            y_lat, _ = _rwkv_scan(u, _pack_state_pairs(state_rwkv[:, e]), prm, N_CTX_ROWS, DEC_BATCH, DEC_SEQ)
            o_rw = _rwkv_finish(u, y_ctx, y_lat, prm)

            w_out = w_out_even[e].astype(BF16)
            x = _out_proj(x, md[2], [(o_ctx, o_lat), o_rw], [w_out[:D_NA], w_out[D_NA:]], "out_even")

            na_k_out.append(qkv[:N_CTX_ROWS, D_NA:2 * D_NA].reshape(BATCH, SEQ, N_HEADS, HEAD_DIM))
            na_v_out.append(qkv[:N_CTX_ROWS, 2 * D_NA:].reshape(BATCH, SEQ, N_HEADS, HEAD_DIM))
            rw_out.append(_unpack_state_pairs(s_ctx))
        else:
            o = l // 2
            lam_init = 0.8 - 0.6 * math.exp(-0.3 * l)
            (qkv,) = _norm_proj(x, norm_mix[l], md[0], md[1], w_qkv_diff[o].astype(BF16),
                                (3 * D_DIFF,), "proj_odd")
            o_ctx = _diff_ctx(qkv, diff_lam_q[o], diff_lam_k[o], diff_subln[o], lam_init)
            qkv_r = _rope_cast(qkv, cos_t, sin_t)
            o_lat = _diff_lat(qkv_r, cache_diff_k[:, o].reshape(DEC_BATCH, PAST_LEN, D_DIFF),
                              cache_diff_v[:, o].reshape(DEC_BATCH, PAST_LEN, D_DIFF),
                              diff_lam_q[o], diff_lam_k[o], diff_subln[o], lam_init)
            x = _out_proj(x, md[2], [(o_ctx, o_lat)], [w_out_diff[o].astype(BF16)], "out_odd")
            dk_out.append(qkv[:N_CTX_ROWS, D_DIFF:2 * D_DIFF].reshape(BATCH, SEQ, N_HEADS, 2 * HEAD_DIM))
            dv_out.append(qkv[:N_CTX_ROWS, 2 * D_DIFF:].reshape(BATCH, SEQ, N_HEADS, 2 * HEAD_DIM))
        x = _ffn(x, norm_ffn[l], md[3], md[4], md[5], ffn_w1[l].astype(BF16), ffn_w3[l].astype(BF16),
                 ffn_w2[l].astype(BF16), "ffn")

    y = _final_norm(x, norm_final)
    y_prompt = y[:N_CTX_ROWS].reshape(BATCH, SEQ, D_MODEL)
    y_sample = y[N_CTX_ROWS:].reshape(DEC_BATCH, DEC_SEQ, D_MODEL)
    return (y_prompt, y_sample, jnp.stack(na_k_out, axis=1), jnp.stack(na_v_out, axis=1),
            jnp.stack(rw_out, axis=1), jnp.stack(dk_out, axis=1), jnp.stack(dv_out, axis=1))
```

```python
import functools
import math

import jax
import jax.numpy as jnp
from jax import lax
from jax.experimental import pallas as pl
from jax.experimental.pallas import tpu as pltpu

F32 = jnp.float32
BF16 = jnp.bfloat16

D_MODEL = 1024
BATCH = 32
SEQ = 256
DEPTH = 4
DEC_BATCH = 2
DEC_SEQ = 4096
PAST_LEN = 512
GRID_W = 64
GRID_ROWS = DEC_SEQ // GRID_W
HEAD_DIM = 64
N_HEADS = 8
D_NA = 512
D_RWKV = 512
D_DIFF = 1024
NA_KH = 8
NA_KW = 16
DECAY_LORA = 64
ICL_LORA = 64
GATE_LORA = 128
P_RWKV = 3 * D_RWKV + DECAY_LORA + ICL_LORA + GATE_LORA
P_EVEN = 3 * D_NA + P_RWKV
D_FF = 2816
ROPE_F = HEAD_DIM // 4
ROPE_BASE = 10000.0
NORM_EPS = 1e-6
GN_EPS = 64e-5
QK_SCALE = HEAD_DIM ** -0.5
LOG2_E = math.log2(math.e)

N_CTX_ROWS = BATCH * SEQ
N_LAT_ROWS = DEC_BATCH * DEC_SEQ
M_ALL = N_CTX_ROWS + N_LAT_ROWS

LANES = 128
CHUNK = 64
INV_BLOCK = 16
NEG_BIG = -1e30
VMEM_LIMIT = 56 * 1024 * 1024


def _cparams(sem):
    return pltpu.CompilerParams(dimension_semantics=sem, vmem_limit_bytes=VMEM_LIMIT)


def _bdot(a, b):
    return jnp.dot(a.astype(BF16), b.astype(BF16), preferred_element_type=F32)


def _bdot_nt(a, b):
    return lax.dot_general(a.astype(BF16), b.astype(BF16), (((1,), (1,)), ((), ())),
                           preferred_element_type=F32)


def _bdot_tn(a, b):
    return lax.dot_general(a.astype(BF16), b.astype(BF16), (((0,), (0,)), ((), ())),
                           preferred_element_type=F32)


def _split3(x):
    hi = x.astype(BF16)
    r1 = x - hi.astype(F32)
    mid = r1.astype(BF16)
    lo = (r1 - mid.astype(F32)).astype(BF16)
    return hi, mid, lo


def _dot3_rhs_exact(x, e):
    hi, mid, lo = _split3(x)
    d = functools.partial(jnp.dot, preferred_element_type=F32)
    return d(hi, e) + d(mid, e) + d(lo, e)


def _dot3_lhs_exact(e, x):
    hi, mid, lo = _split3(x)
    d = functools.partial(jnp.dot, preferred_element_type=F32)
    return d(e, hi) + d(e, mid) + d(e, lo)


def _sigmoid(x):
    return 1.0 / (1.0 + jnp.exp(-x))


def _softplus(x):
    return jnp.maximum(x, 0.0) + jnp.log(1.0 + jnp.exp(-jnp.abs(x)))


def _norm_mod(x, g, shift, scale):
    r = lax.rsqrt(jnp.mean(x * x, axis=-1, keepdims=True) + NORM_EPS)
    return ((x * r) * g) * (1.0 + scale) + shift


def _group_of_tile(i, tm):
    n_ctx_tiles = N_CTX_ROWS // tm
    tiles_per_lat = DEC_SEQ // tm
    return jnp.where(i < n_ctx_tiles, 0, 1 + (i - n_ctx_tiles) // tiles_per_lat)


def _mod_spec(tm):
    return pl.BlockSpec((1, 1, D_MODEL), lambda i, *_: (_group_of_tile(i, tm), 0, 0))


def _full_spec(shape):
    n = len(shape)
    return pl.BlockSpec(shape, lambda *_: (0,) * n)


def _ada_kernel(cv_ref, w_ref, b_ref, o_ref):
    cv = cv_ref[...]
    s = cv * _sigmoid(cv)
    o_ref[0] = _bdot(s, w_ref[0]) + b_ref[0]


def _ada_all(cv8, w_ada, b_ada):
    tn = 512
    return pl.pallas_call(
        _ada_kernel,
        grid=(DEPTH, 6 * D_MODEL // tn),
        in_specs=[
            pl.BlockSpec((8, D_MODEL), lambda l, j: (0, 0)),
            pl.BlockSpec((1, D_MODEL, tn), lambda l, j: (l, 0, j)),
            pl.BlockSpec((1, 1, tn), lambda l, j: (l, 0, j)),
        ],
        out_specs=pl.BlockSpec((1, 8, tn), lambda l, j: (l, 0, j)),
        out_shape=jax.ShapeDtypeStruct((DEPTH, 8, 6 * D_MODEL), F32),
        compiler_params=_cparams(("parallel", "parallel")),
        name="adaln",
    )(cv8, w_ada, b_ada.reshape(DEPTH, 1, 6 * D_MODEL))


def _proj_kernel(x_ref, g_ref, sh_ref, sc_ref, w_ref, *o_refs, splits):
    h = _norm_mod(x_ref[...], g_ref[...], sh_ref[0], sc_ref[0]).astype(BF16)
    off = 0
    for o_ref, n in zip(o_refs, splits):
        o_ref[...] = jnp.dot(h, w_ref[:, off:off + n], preferred_element_type=F32)
        off += n


def _norm_proj(x, g, shift, scale, w_bf16, splits, name):
    tm = 512
    n = w_bf16.shape[1]
    return pl.pallas_call(
        functools.partial(_proj_kernel, splits=splits),
        grid=(M_ALL // tm,),
        in_specs=[
            pl.BlockSpec((tm, D_MODEL), lambda i: (i, 0)),
            _full_spec((1, D_MODEL)),
            _mod_spec(tm),
            _mod_spec(tm),
            _full_spec((D_MODEL, n)),
        ],
        out_specs=[pl.BlockSpec((tm, s), lambda i: (i, 0)) for s in splits],
        out_shape=[jax.ShapeDtypeStruct((M_ALL, s), F32) for s in splits],
        compiler_params=_cparams(("parallel",)),
        name=name,
    )(x, g.reshape(1, D_MODEL), shift, scale, w_bf16)


def _part_specs(tm, ncols):
    n_ctx_tiles = N_CTX_ROWS // tm
    return [pl.BlockSpec((tm, ncols), lambda i, *_: (jnp.minimum(i, n_ctx_tiles - 1), 0)),
            pl.BlockSpec((tm, ncols), lambda i, *_: (jnp.maximum(i - n_ctx_tiles, 0), 0))]


def _pick_part(i, tm, ctx_ref, lat_ref):
    return jnp.where(i < N_CTX_ROWS // tm, ctx_ref[...], lat_ref[...])


def _out_kernel(x_ref, gate_ref, *refs, split, tm):
    i = pl.program_id(0)
    n_act = sum(2 if s else 1 for s in split)
    a_refs, w_refs, o_ref = refs[:n_act], refs[n_act:n_act + len(split)], refs[n_act + len(split)]
    acc = None
    pos = 0
    for is_split, w_ref in zip(split, w_refs):
        if is_split:
            a = _pick_part(i, tm, a_refs[pos], a_refs[pos + 1])
            pos += 2
        else:
            a = a_refs[pos][...]
            pos += 1
        t = jnp.dot(a, w_ref[...], preferred_element_type=F32)
        acc = t if acc is None else acc + t
    o_ref[...] = x_ref[...] + gate_ref[0] * acc


def _out_proj(x, gate, acts, ws, name):
    tm = 512
    in_specs = [pl.BlockSpec((tm, D_MODEL), lambda i: (i, 0)), _mod_spec(tm)]
    flat, split = [], []
    for a in acts:
        if isinstance(a, tuple):
            in_specs += _part_specs(tm, a[0].shape[1])
            flat += list(a)
            split.append(True)
        else:
            in_specs.append(pl.BlockSpec((tm, a.shape[1]), lambda i: (i, 0)))
            flat.append(a)
            split.append(False)
    return pl.pallas_call(
        functools.partial(_out_kernel, split=tuple(split), tm=tm),
        grid=(M_ALL // tm,),
        in_specs=in_specs + [_full_spec(w.shape) for w in ws],
        out_specs=pl.BlockSpec((tm, D_MODEL), lambda i: (i, 0)),
        out_shape=jax.ShapeDtypeStruct((M_ALL, D_MODEL), F32),
        compiler_params=_cparams(("parallel",)),
        name=name,
    )(x, gate, *flat, *ws)


def _ffn_kernel(x_ref, g_ref, sh_ref, sc_ref, gate_ref, w1_ref, w3_ref, w2_ref, o_ref):
    x = x_ref[...]
    h = _norm_mod(x, g_ref[...], sh_ref[0], sc_ref[0]).astype(BF16)
    a = jnp.dot(h, w1_ref[...], preferred_element_type=F32)
    b = jnp.dot(h, w3_ref[...], preferred_element_type=F32)
    gated = ((a * _sigmoid(a)) * b).astype(BF16)
    o_ref[...] = x + gate_ref[0] * jnp.dot(gated, w2_ref[...], preferred_element_type=F32)


def _resident_spec(shape):
    n = len(shape)
    return pl.BlockSpec(shape, lambda *_: (0,) * n, pipeline_mode=pl.Buffered(1))


def _ffn(x, g, shift, scale, gate, w1, w3, w2, name):
    tm = 512
    return pl.pallas_call(
        _ffn_kernel,
        grid=(M_ALL // tm,),
        in_specs=[
            pl.BlockSpec((tm, D_MODEL), lambda i: (i, 0)),
            _full_spec((1, D_MODEL)),
            _mod_spec(tm),
            _mod_spec(tm),
            _mod_spec(tm),
            _resident_spec((D_MODEL, D_FF)),
            _resident_spec((D_MODEL, D_FF)),
            _resident_spec((D_FF, D_MODEL)),
        ],
        out_specs=pl.BlockSpec((tm, D_MODEL), lambda i: (i, 0)),
        out_shape=jax.ShapeDtypeStruct((M_ALL, D_MODEL), F32),
        compiler_params=_cparams(("parallel",)),
        name=name,
    )(x, g.reshape(1, D_MODEL), shift, scale, gate, w1, w3, w2)


def _final_norm_kernel(x_ref, g_ref, oc_ref, ol_ref, *, tm):
    i = pl.program_id(0)
    x = x_ref[...]
    r = lax.rsqrt(jnp.mean(x * x, axis=-1, keepdims=True) + NORM_EPS)
    y = (x * r) * g_ref[...]

    @pl.when(i < N_CTX_ROWS // tm)
    def _():
        oc_ref[...] = y

    @pl.when(i >= N_CTX_ROWS // tm)
    def _():
        ol_ref[...] = y


def _final_norm(x, g):
    tm = 1024
    return pl.pallas_call(
        functools.partial(_final_norm_kernel, tm=tm),
        grid=(M_ALL // tm,),
        in_specs=[pl.BlockSpec((tm, D_MODEL), lambda i: (i, 0)), _full_spec((1, D_MODEL))],
        out_specs=_part_specs(tm, D_MODEL),
        out_shape=[jax.ShapeDtypeStruct((N_CTX_ROWS, D_MODEL), F32),
                   jax.ShapeDtypeStruct((N_LAT_ROWS, D_MODEL), F32)],
        compiler_params=_cparams(("arbitrary",)),
        name="final_norm",
    )(x, g.reshape(1, D_MODEL))


def _half_masks():
    lane = lax.broadcasted_iota(jnp.int32, (1, LANES), 1)
    return (lane < HEAD_DIM, lane >= HEAD_DIM)


def _na_ctx_kernel(q_ref, k_ref, v_ref, o_ref):
    q = q_ref[...] * QK_SCALE
    k = k_ref[...].astype(BF16)
    v = v_ref[...]
    out = jnp.zeros(o_ref.shape, F32)
    for m in _half_masks():
        s = _bdot_nt(jnp.where(m, q, 0.0), k)
        e = jnp.exp(s - jnp.max(s, axis=-1, keepdims=True))
        inv_l = 1.0 / jnp.sum(e, axis=-1, keepdims=True)
        out = out + _bdot(e, jnp.where(m, v, 0.0)) * inv_l
    o_ref[...] = out.astype(BF16)


def _na_ctx(q, k, v):
    spec = pl.BlockSpec((SEQ, LANES), lambda b, p: (b, p))
    return pl.pallas_call(
        _na_ctx_kernel,
        grid=(BATCH, D_NA // LANES),
        in_specs=[spec, spec, spec],
        out_specs=spec,
        out_shape=jax.ShapeDtypeStruct((N_CTX_ROWS, D_NA), BF16),
        compiler_params=_cparams(("parallel", "parallel")),
        name="na_ctx",
    )(q, k, v)


def _na_lat_kernel(q_ref, k_ref, v_ref, kc_ref, vc_ref, bias_ref, o_ref, *, rows_per_step):
    n_loc = NA_KH * GRID_W
    masks = _half_masks()
    kc = kc_ref[0].astype(BF16)
    vc = vc_ref[0]
    vcm = [jnp.where(m, vc, 0.0).astype(BF16) for m in masks]
    work = []
    for rr in range(rows_per_step):
        i = pl.program_id(2) * rows_per_step + rr
        r0 = jnp.clip(i - NA_KH // 2, 0, GRID_ROWS - NA_KH)
        start = pl.multiple_of(r0 * GRID_W, GRID_W)
        q = q_ref[rr * GRID_W:(rr + 1) * GRID_W, :] * QK_SCALE
        kl = k_ref[pl.ds(start, n_loc), :].astype(BF16)
        vl = v_ref[pl.ds(start, n_loc), :]
        for hh, m in enumerate(masks):
            qm = jnp.where(m, q, 0.0).astype(BF16)
            work.append(dict(rr=rr, hh=hh, vlm=jnp.where(m, vl, 0.0).astype(BF16),
                             sl=_bdot_nt(qm, kl) + bias_ref[r0 - i + NA_KH - 1, hh], sc=_bdot_nt(qm, kc)))
    for w in work:
        mx = jnp.maximum(jnp.max(w["sl"], axis=-1, keepdims=True), jnp.max(w["sc"], axis=-1, keepdims=True))
        w["el"] = jnp.exp(w["sl"] - mx)
        w["ec"] = jnp.exp(w["sc"] - mx)
    for w in work:
        inv_l = 1.0 / (jnp.sum(w["el"], axis=-1, keepdims=True) + jnp.sum(w["ec"], axis=-1, keepdims=True))
        w["o"] = (_bdot(w["el"], w["vlm"]) + _bdot(w["ec"], vcm[w["hh"]])) * inv_l
    for rr in range(rows_per_step):
        a, b = [w["o"] for w in work if w["rr"] == rr]
        o_ref[rr * GRID_W:(rr + 1) * GRID_W, :] = (a + b).astype(BF16)


def _na_bias_table(rpb):
    jj = jnp.arange(GRID_W)[:, None]
    cc = jnp.arange(GRID_W)[None, :]
    c0 = jnp.clip(jj - NA_KW // 2, 0, GRID_W - NA_KW)
    inwin = (cc >= c0) & (cc < c0 + NA_KW)
    idx = jnp.clip(cc - jj + NA_KW - 1, 0, 2 * NA_KW - 2)
    full = rpb[:, :, idx]
    full = jnp.where(inwin[None, None], full, NEG_BIG)
    tabs = []
    for v in range(NA_KH):
        t = full[:, v:v + NA_KH]
        tabs.append(t.transpose(0, 2, 1, 3).reshape(N_HEADS, GRID_W, NA_KH * GRID_W))
    return jnp.stack(tabs, axis=0)


def _na_lat(q, k, v, kc, vc, bias_tab):
    rows_per_step = 4
    tq = rows_per_step * GRID_W
    steps = GRID_ROWS // rows_per_step
    lat_blk = N_CTX_ROWS // DEC_SEQ
    row_blk = N_CTX_ROWS // tq
    return pl.pallas_call(
        functools.partial(_na_lat_kernel, rows_per_step=rows_per_step),
        grid=(DEC_BATCH, D_NA // LANES, steps),
        in_specs=[
            pl.BlockSpec((tq, LANES), lambda b, p, i: (row_blk + b * steps + i, p)),
            pl.BlockSpec((DEC_SEQ, LANES), lambda b, p, i: (lat_blk + b, p)),
            pl.BlockSpec((DEC_SEQ, LANES), lambda b, p, i: (lat_blk + b, p)),
            pl.BlockSpec((1, PAST_LEN, LANES), lambda b, p, i: (b, 0, p)),
            pl.BlockSpec((1, PAST_LEN, LANES), lambda b, p, i: (b, 0, p)),
            pl.BlockSpec((NA_KH, 2, GRID_W, NA_KH * GRID_W), lambda b, p, i: (0, p, 0, 0)),
        ],
        out_specs=pl.BlockSpec((tq, LANES), lambda b, p, i: (b * steps + i, p)),
        out_shape=jax.ShapeDtypeStruct((N_LAT_ROWS, D_NA), BF16),
        compiler_params=_cparams(("parallel", "parallel", "arbitrary")),
        name="na_lat",
    )(q, k, v, kc, vc, bias_tab)


def _lam_value(lq_ref, lk_ref, lam_init):
    s = jnp.sum(lq_ref[...] * lk_ref[...], axis=-1, keepdims=True)
    return jnp.exp(s[0:1]) - jnp.exp(s[1:2]) + lam_init


def _subln(o, sub, lam_init):
    r = lax.rsqrt(jnp.mean(o * o, axis=-1, keepdims=True) + NORM_EPS)
    return ((o * r) * sub) * (1.0 - lam_init)


def _diff_ctx_kernel(lq_ref, lk_ref, sub_ref, q_ref, k_ref, v_ref, o_ref, *, lam_init):
    lam = _lam_value(lq_ref, lk_ref, lam_init)
    q = q_ref[...] * QK_SCALE
    k = k_ref[...].astype(BF16)
    p = []
    for m in _half_masks():
        s = _bdot_nt(jnp.where(m, q, 0.0), k)
        e = jnp.exp(s - jnp.max(s, axis=-1, keepdims=True))
        p.append(e * (1.0 / jnp.sum(e, axis=-1, keepdims=True)))
    o = _bdot(p[0] - lam * p[1], v_ref[...])
    o_ref[...] = _subln(o, sub_ref[...], lam_init).astype(BF16)


def _diff_ctx(q, k, v, lam_q, lam_k, subln, lam_init):
    spec = pl.BlockSpec((SEQ, LANES), lambda b, h: (b, h))
    return pl.pallas_call(
        functools.partial(_diff_ctx_kernel, lam_init=lam_init),
        grid=(BATCH, N_HEADS),
        in_specs=[_full_spec((2, HEAD_DIM)), _full_spec((2, HEAD_DIM)), _full_spec((1, LANES)),
                  spec, spec, spec],
        out_specs=spec,
        out_shape=jax.ShapeDtypeStruct((N_CTX_ROWS, D_DIFF), BF16),
        compiler_params=_cparams(("parallel", "parallel")),
        name="diff_ctx",
    )(lam_q, lam_k, subln.reshape(1, LANES), q, k, v)


def _rope_kernel(q_ref, k_ref, v_ref, cos_ref, sin_ref, qo_ref, ko_ref, vt_ref):
    cos = cos_ref[...]
    sin = sin_ref[...]
    lane = lax.broadcasted_iota(jnp.int32, (1, LANES), 1)
    first = (lane % (2 * ROPE_F)) < ROPE_F
    for x_ref, o_ref, scale in ((q_ref, qo_ref, QK_SCALE * LOG2_E), (k_ref, ko_ref, None)):
        for j in range(D_DIFF // LANES):
            x = x_ref[:, j * LANES:(j + 1) * LANES]
            partner = jnp.where(first, pltpu.roll(x, LANES - ROPE_F, 1), pltpu.roll(x, ROPE_F, 1))
            y = x * cos + partner * sin
            if scale is not None:
                y = y * scale
            o_ref[:, j * LANES:(j + 1) * LANES] = y.astype(BF16)
    vt_ref[0] = v_ref[...].T.astype(BF16)


def _rope_tables():
    t = jnp.arange(DEC_SEQ)
    pos = jnp.stack([t // GRID_W, t % GRID_W], -1).astype(F32)
    inv = ROPE_BASE ** (-jnp.arange(ROPE_F, dtype=F32) / ROPE_F)
    ang = pos[:, :, None] * inv
    cos, sin = jnp.cos(ang), jnp.sin(ang)
    cos64 = jnp.concatenate([cos, cos], axis=-1).reshape(DEC_SEQ, HEAD_DIM)
    sin64 = jnp.concatenate([-sin, sin], axis=-1).reshape(DEC_SEQ, HEAD_DIM)
    return jnp.tile(cos64, (1, 2)), jnp.tile(sin64, (1, 2))


def _rope_cast(q, k, v, cos_t, sin_t):
    tm = 512
    base = N_CTX_ROWS // tm
    per_seq = DEC_SEQ // tm
    row_spec = pl.BlockSpec((tm, D_DIFF), lambda i: (base + i, 0))
    tab_spec = pl.BlockSpec((tm, LANES), lambda i: (i % per_seq, 0))
    out_spec = pl.BlockSpec((tm, D_DIFF), lambda i: (i, 0))
    return pl.pallas_call(
        _rope_kernel,
        grid=(N_LAT_ROWS // tm,),
        in_specs=[row_spec, row_spec, row_spec, tab_spec, tab_spec],
        out_specs=[out_spec, out_spec,
                   pl.BlockSpec((1, D_DIFF, tm), lambda i: (i // per_seq, 0, i % per_seq))],
        out_shape=[jax.ShapeDtypeStruct((N_LAT_ROWS, D_DIFF), BF16),
                   jax.ShapeDtypeStruct((N_LAT_ROWS, D_DIFF), BF16),
                   jax.ShapeDtypeStruct((DEC_BATCH, D_DIFF, DEC_SEQ), BF16)],
        compiler_params=_cparams(("parallel",)),
        name="rope_cast",
    )(q, k, v, cos_t, sin_t)


def _diff_lat_kernel(lq_ref, lk_ref, sub_ref, q_ref, kl_ref, vlt_ref, kc_ref, vct_ref, o_ref, *,
                     lam_init, sub_q, n_sub):
    lam = _lam_value(lq_ref, lk_ref, lam_init)
    kl = kl_ref[...]
    kc = kc_ref[0].astype(BF16)
    vlt = vlt_ref[0]
    vct = vct_ref[0].astype(BF16)
    work = []
    for t in range(n_sub):
        q = q_ref[t * sub_q:(t + 1) * sub_q, :]
        for m in _half_masks():
            qm = jnp.where(m, q, jnp.zeros_like(q))
            work.append(dict(sl=_bdot_nt(kl, qm), sc=_bdot_nt(kc, qm)))
    for w in work:
        mx = jnp.maximum(jnp.max(w["sl"], axis=0, keepdims=True), jnp.max(w["sc"], axis=0, keepdims=True))
        el = jnp.exp2(w["sl"] - mx)
        ec = jnp.exp2(w["sc"] - mx)
        w["inv_l"] = 1.0 / (jnp.sum(el, axis=0, keepdims=True) + jnp.sum(ec, axis=0, keepdims=True))
        w["el"] = el.astype(BF16)
        w["ec"] = ec.astype(BF16)
    for w in work:
        w["o"] = (jnp.dot(vlt, w["el"], preferred_element_type=F32)
                  + jnp.dot(vct, w["ec"], preferred_element_type=F32)) * w["inv_l"]
    for t in range(n_sub):
        ot = work[2 * t]["o"] - lam * work[2 * t + 1]["o"]
        r = lax.rsqrt(jnp.mean(ot * ot, axis=0, keepdims=True) + NORM_EPS)
        on = ((ot * r) * sub_ref[...]) * (1.0 - lam_init)
        o_ref[t * sub_q:(t + 1) * sub_q, :] = on.T.astype(BF16)


def _diff_lat(q_r, k_r, v_t, kc, vc_t, lam_q, lam_k, subln, lam_init):
    sub_q, n_sub = 256, 2
    tq = sub_q * n_sub
    per_seq = DEC_SEQ // tq
    return pl.pallas_call(
        functools.partial(_diff_lat_kernel, lam_init=lam_init, sub_q=sub_q, n_sub=n_sub),
        grid=(DEC_BATCH, N_HEADS, per_seq),
        in_specs=[
            _full_spec((2, HEAD_DIM)),
            _full_spec((2, HEAD_DIM)),
            _full_spec((LANES, 1)),
            pl.BlockSpec((tq, LANES), lambda b, h, i: (b * per_seq + i, h)),
            pl.BlockSpec((DEC_SEQ, LANES), lambda b, h, i: (b, h)),
            pl.BlockSpec((1, LANES, DEC_SEQ), lambda b, h, i: (b, h, 0)),
            pl.BlockSpec((1, PAST_LEN, LANES), lambda b, h, i: (b, 0, h)),
            pl.BlockSpec((1, LANES, PAST_LEN), lambda b, h, i: (b, h, 0)),
        ],
        out_specs=pl.BlockSpec((tq, LANES), lambda b, h, i: (b * per_seq + i, h)),
        out_shape=jax.ShapeDtypeStruct((N_LAT_ROWS, D_DIFF), BF16),
        compiler_params=_cparams(("parallel", "parallel", "arbitrary")),
        name="diff_lat",
    )(lam_q, lam_k, subln.reshape(LANES, 1), q_r, k_r, v_t, kc, vc_t)


def _token_shift(u, prev_row, next_row, mu):
    n = u.shape[0]
    rows = lax.broadcasted_iota(jnp.int32, (n, 1), 0)
    up = jnp.where(rows == 0, prev_row, pltpu.roll(u, 1, 0))
    un = jnp.where(rows == n - 1, next_row, pltpu.roll(u, n - 1, 0))
    return u + mu[0:1] * (up - u) + mu[1:2] * (un - u)


def _icl_rate_and_key(us, a0, a2p, k_a):
    lo = us[:, 3 * D_RWKV:3 * D_RWKV + LANES]
    k = us[:, D_RWKV:2 * D_RWKV]
    a = _sigmoid(a0 + _bdot(lo, a2p))
    return a, k * (1.0 + (a - 1.0) * k_a)


def _stack_blockdiag(x):
    shape = (2 * x.shape[0], x.shape[1])
    row = lax.broadcasted_iota(jnp.int32, shape, 0)
    col = lax.broadcasted_iota(jnp.int32, shape, 1)
    keep = (row // HEAD_DIM) == ((col % LANES) // HEAD_DIM)
    return jnp.where(keep, jnp.concatenate([x, x], axis=0), 0.0).astype(BF16)


def _rwkv_chunk_inputs(u, prev_row, next_row, reverse, mu, w0, w2p, a0, a2p, k_k, k_a, bd_ones):
    us = _token_shift(u, prev_row, next_row, mu)
    r = us[:, 0:D_RWKV]
    k = us[:, D_RWKV:2 * D_RWKV]
    v = us[:, 2 * D_RWKV:3 * D_RWKV]
    lo = us[:, 3 * D_RWKV:3 * D_RWKV + LANES]
    w_raw = w0 + _bdot(jnp.tanh(lo), w2p)
    logw = -jnp.exp(-_softplus(-w_raw) - 0.5)
    a, kt = _icl_rate_and_key(us, a0, a2p, k_a)
    kk_raw = k * k_k
    ss = _dot3_rhs_exact(kk_raw * kk_raw, bd_ones)
    kk = kk_raw * lax.rsqrt(jnp.maximum(ss, 1e-12))
    row = lax.broadcasted_iota(jnp.int32, (CHUNK, CHUNK), 0)
    col = lax.broadcasted_iota(jnp.int32, (CHUNK, CHUNK), 1)
    seen = (col >= row) if reverse else (col <= row)
    cum = _dot3_lhs_exact(seen.astype(F32).astype(BF16), logw)
    p_inv = jnp.exp(-cum)
    tot = cum[0:1, :] if reverse else cum[CHUNK - 1:CHUNK, :]
    return dict(a_hat=jnp.exp(cum - logw) * kk, b_hat=(kk * a) * p_inv, k_hat=kt * p_inv,
                r_hat=r * jnp.exp(cum), v=v, p_tot=jnp.exp(tot))


def _rwkv_scan_kernel(*refs, n_chunks, n_par, has_state):
    u_refs = (refs[0:3], refs[3:6])
    pos = 6
    s0_ref = None
    if has_state:
        s0_ref = refs[pos]
        pos += 1
    mu_ref, w0_ref, w2_ref, a0_ref, a2_ref, kk_ref, ka_ref, bd_ref = refs[pos:pos + 8]
    y_refs = refs[pos + 8:pos + 10]
    sfin_ref, s_ref = refs[pos + 10], refs[pos + 11]
    c = pl.program_id(1)
    n_pairs = D_RWKV // LANES

    @pl.when(c == 0)
    def _():
        if has_state:
            s_ref[...] = s0_ref[...]
        else:
            s_ref[...] = jnp.zeros(s_ref.shape, F32)

    at_start = jnp.where(c == 0, 0.0, 1.0)
    at_end = jnp.where(c == n_chunks - 1, 0.0, 1.0)
    has_prev = (at_start, at_end)
    has_next = (at_end, at_start)

    row = lax.broadcasted_iota(jnp.int32, (CHUNK, LANES), 0)
    colm = lax.broadcasted_iota(jnp.int32, (CHUNK, LANES), 1) % CHUNK
    strict = (colm < row, colm > row)
    incl = (colm <= row, colm >= row)
    same_blk = (row // INV_BLOCK) == (colm // INV_BLOCK)
    eye = (row == colm).astype(F32)
    prow = lax.broadcasted_iota(jnp.int32, (LANES, LANES), 0)
    pcol = lax.broadcasted_iota(jnp.int32, (LANES, LANES), 1)
    blockdiag = (prow // HEAD_DIM) == (pcol // HEAD_DIM)
    eye_pair = (prow == pcol).astype(F32)

    chains = []
    for j in range(n_par):
        for d in range(2):
            u_ref, up_ref, un_ref = u_refs[d]
            q = _rwkv_chunk_inputs(
                u_ref[j], up_ref[j, 7:8, :] * has_prev[d], un_ref[j, 0:1, :] * has_next[d], d == 1,
                mu_ref[...], w0_ref[d], w2_ref[d], a0_ref[d], a2_ref[d], kk_ref[...], ka_ref[...], bd_ref[...])
            for p in range(n_pairs):
                sl = slice(p * LANES, (p + 1) * LANES)
                chains.append(dict(j=j, d=d, p=p, sl=sl, ah=q["a_hat"][:, sl], bh=q["b_hat"][:, sl],
                                   kh=q["k_hat"][:, sl], rh=q["r_hat"][:, sl], vv=q["v"][:, sl],
                                   pt=q["p_tot"][:, sl]))

    def stage(fn):
        for ch in chains:
            fn(ch)

    def s_scores(ch):
        ar = jnp.concatenate([ch["ah"], ch["rh"]], axis=0)
        xb = _bdot_nt(ar, _stack_blockdiag(ch["bh"]))
        xk = _bdot_nt(ar, _stack_blockdiag(ch["kh"]))
        d = ch["d"]
        l_ab = jnp.where(strict[d], xb[:CHUNK], 0.0)
        ch["l_ak"] = jnp.where(strict[d], xk[:CHUNK], 0.0)
        ch["m_rb"] = jnp.where(incl[d], xb[CHUNK:], 0.0)
        ch["m_rk"] = jnp.where(incl[d], xk[CHUNK:], 0.0)
        ch["x1"] = -jnp.where(same_blk, l_ab, 0.0)
        ch["l_off"] = jnp.where(same_blk, 0.0, l_ab)
        ch["vs"] = _stack_blockdiag(ch["vv"])
    stage(s_scores)

    def s_x2(ch):
        ch["x2"] = _bdot(ch["x1"], _stack_blockdiag(ch["x1"]))
        ch["x2s"] = _stack_blockdiag(ch["x2"])
        ch["u"] = _bdot(ch["l_ak"], ch["vs"])
    stage(s_x2)

    def s_x4(ch):
        ch["x4"] = _bdot(ch["x2"], ch["x2s"])
        td = eye + ch["x1"]
        ch["td"] = td + _bdot(td, ch["x2s"])
    stage(s_x4)

    def s_x8(ch):
        x4s = _stack_blockdiag(ch["x4"])
        ch["x8s"] = _stack_blockdiag(_bdot(ch["x4"], x4s))
        ch["td"] = ch["td"] + _bdot(ch["td"], x4s)
    stage(s_x8)

    def s_td(ch):
        ch["td"] = ch["td"] + _bdot(ch["td"], ch["x8s"])
    stage(s_td)

    def s_mm(ch):
        ch["mm"] = _bdot(ch["td"], _stack_blockdiag(ch["l_off"]))
    stage(s_mm)

    def s_m2(ch):
        ch["m2s"] = _stack_blockdiag(_bdot(ch["mm"], _stack_blockdiag(ch["mm"])))
    stage(s_m2)

    def s_n2(ch):
        n1 = eye - ch["mm"]
        ch["n2"] = n1 + _bdot(n1, ch["m2s"])
    stage(s_n2)

    def s_tinv(ch):
        ch["t_inv"] = _bdot(ch["n2"], _stack_blockdiag(ch["td"]))
    stage(s_tinv)

    def s_tx(ch):
        ch["tx"] = _bdot(ch["t_inv"], _stack_blockdiag(jnp.concatenate([ch["ah"], ch["u"]], axis=1)))
    stage(s_tx)

    def s_out(ch):
        tx = ch["tx"]
        mx = _bdot(ch["m_rb"], _stack_blockdiag(tx))
        q_eff = ch["rh"] - mx[:, :LANES]
        y_loc = _bdot(ch["m_rk"], ch["vs"]) - mx[:, LANES:]
        g_mat = jnp.where(blockdiag, eye_pair - _bdot_tn(tx[:, :LANES], ch["bh"]), 0.0) * ch["pt"]
        h_mat = jnp.where(
            blockdiag,
            _bdot_tn(jnp.concatenate([ch["vv"], -tx[:, LANES:]], axis=0),
                     jnp.concatenate([ch["kh"], ch["bh"]], axis=0)),
            0.0) * ch["pt"]
        j, d, p = ch["j"], ch["d"], ch["p"]
        s_old = s_ref[j, d, p]
        y_refs[d][j, :, ch["sl"]] = _bdot_nt(q_eff, s_old) + y_loc
        s_ref[j, d, p] = _bdot(s_old, g_mat) + h_mat
    stage(s_out)

    @pl.when(c == n_chunks - 1)
    def _():
        sfin_ref[...] = s_ref[...]


def _rwkv_scan(u, s0p, prm, row_base, n_seq, seq_len):
    n_par = 2
    n_chunks = seq_len // CHUNK
    n_pairs = D_RWKV // LANES
    base_blk = row_base // (seq_len * n_par)
    n_blk8 = seq_len // 8
    has_state = s0p is not None
    u3 = u.reshape(M_ALL // seq_len, seq_len, P_RWKV)

    def chunk_pos(d, c):
        return c if d == 0 else n_chunks - 1 - c

    in_specs, args = [], []
    for d in range(2):
        in_specs += [
            pl.BlockSpec((n_par, CHUNK, P_RWKV), lambda g, c, d=d: (base_blk + g, chunk_pos(d, c), 0)),
            pl.BlockSpec((n_par, 8, P_RWKV),
                         lambda g, c, d=d: (base_blk + g, jnp.maximum(chunk_pos(d, c) * (CHUNK // 8) - 1, 0), 0)),
            pl.BlockSpec((n_par, 8, P_RWKV),
                         lambda g, c, d=d: (base_blk + g,
                                            jnp.minimum((chunk_pos(d, c) + 1) * (CHUNK // 8), n_blk8 - 1), 0)),
        ]
        args += [u3, u3, u3]
    state_spec = pl.BlockSpec((n_par, 2, n_pairs, LANES, LANES), lambda g, c: (g, 0, 0, 0, 0))
    if has_state:
        in_specs.append(state_spec)
        args.append(s0p)
    in_specs += [
        _full_spec((2, P_RWKV)),
        _full_spec((2, 1, D_RWKV)),
        _full_spec((2, LANES, D_RWKV)),
        _full_spec((2, 1, D_RWKV)),
        _full_spec((2, LANES, D_RWKV)),
        _full_spec((1, D_RWKV)),
        _full_spec((1, D_RWKV)),
        _full_spec((D_RWKV, D_RWKV)),
    ]
    args += [prm["mu"], prm["w0"], prm["w2p"], prm["a0"], prm["a2p"], prm["kk"], prm["ka"], prm["bd_ones"]]
    y_shape = jax.ShapeDtypeStruct((n_seq, seq_len, D_RWKV), F32)
    y_f, y_b, s_fin = pl.pallas_call(
        functools.partial(_rwkv_scan_kernel, n_chunks=n_chunks, n_par=n_par, has_state=has_state),
        grid=(n_seq // n_par, n_chunks),
        in_specs=in_specs,
        out_specs=[
            pl.BlockSpec((n_par, CHUNK, D_RWKV), lambda g, c: (g, chunk_pos(0, c), 0)),
            pl.BlockSpec((n_par, CHUNK, D_RWKV), lambda g, c: (g, chunk_pos(1, c), 0)),
            state_spec,
        ],
        out_shape=[y_shape, y_shape, jax.ShapeDtypeStruct((n_seq, 2, n_pairs, LANES, LANES), F32)],
        scratch_shapes=[pltpu.VMEM((n_par, 2, n_pairs, LANES, LANES), F32)],
        compiler_params=_cparams(("parallel", "arbitrary")),
        name="rwkv_scan_" + ("lat" if has_state else "ctx"),
    )(*args)
    n_rows = n_seq * seq_len
    return (y_f.reshape(n_rows, D_RWKV), y_b.reshape(n_rows, D_RWKV)), s_fin


def _rwkv_fin_kernel(u_ref, up_ref, un_ref, ycf_ref, ylf_ref, ycb_ref, ylb_ref, mu_ref, a0_ref, a2_ref,
                     ka_ref, bonus_ref, g2_ref, lnw_ref, lnb_ref, bd1_ref, bdm_ref, o_ref, *, tm):
    i = pl.program_id(0)
    n_ctx_tiles = N_CTX_ROWS // tm
    per_seq = DEC_SEQ // tm
    j = (i - n_ctx_tiles) % per_seq
    is_ctx = i < n_ctx_tiles
    not_first = jnp.where(is_ctx | (j == 0), 0.0, 1.0)
    not_last = jnp.where(is_ctx | (j == per_seq - 1), 0.0, 1.0)
    us = _token_shift(u_ref[...], up_ref[7:8, :] * not_first, un_ref[0:1, :] * not_last, mu_ref[...])
    r = us[:, 0:D_RWKV]
    v = us[:, 2 * D_RWKV:3 * D_RWKV]
    g_lo = us[:, 3 * D_RWKV + LANES:]
    bon = jnp.zeros((tm, D_RWKV), F32)
    for d in range(2):
        _, kt = _icl_rate_and_key(us, a0_ref[d], a2_ref[d], ka_ref[...])
        bon = bon + _dot3_rhs_exact(r * kt * bonus_ref[d], bd1_ref[...])
    bon = bon * v
    y = _pick_part(i, tm, ycf_ref, ylf_ref) + _pick_part(i, tm, ycb_ref, ylb_ref)
    mean = _dot3_rhs_exact(y, bdm_ref[...])
    yc = y - mean
    var = _dot3_rhs_exact(yc * yc, bdm_ref[...])
    yn = (yc * lax.rsqrt(var + GN_EPS)) * lnw_ref[...] + lnb_ref[...]
    gate = _bdot(_sigmoid(g_lo), g2_ref[...])
    o_ref[...] = ((yn + bon) * gate).astype(BF16)


def _rwkv_finish(u, y_ctx, y_lat, prm):
    tm = 256
    n_blk8 = M_ALL // 8
    return pl.pallas_call(
        functools.partial(_rwkv_fin_kernel, tm=tm),
        grid=(M_ALL // tm,),
        in_specs=[
            pl.BlockSpec((tm, P_RWKV), lambda i: (i, 0)),
            pl.BlockSpec((8, P_RWKV), lambda i: (jnp.maximum(i * (tm // 8) - 1, 0), 0)),
            pl.BlockSpec((8, P_RWKV), lambda i: (jnp.minimum((i + 1) * (tm // 8), n_blk8 - 1), 0)),
            *_part_specs(tm, D_RWKV),
            *_part_specs(tm, D_RWKV),
            _full_spec((2, P_RWKV)),
            _full_spec((2, 1, D_RWKV)),
            _full_spec((2, LANES, D_RWKV)),
            _full_spec((1, D_RWKV)),
            _full_spec((2, 1, D_RWKV)),
            _full_spec((GATE_LORA, D_RWKV)),
            _full_spec((1, D_RWKV)),
            _full_spec((1, D_RWKV)),
            _full_spec((D_RWKV, D_RWKV)),
            _full_spec((D_RWKV, D_RWKV)),
        ],
        out_specs=pl.BlockSpec((tm, D_RWKV), lambda i: (i, 0)),
        out_shape=jax.ShapeDtypeStruct((M_ALL, D_RWKV), BF16),
        compiler_params=_cparams(("parallel",)),
        name="rwkv_finish",
    )(u, u, u, y_ctx[0], y_lat[0], y_ctx[1], y_lat[1], prm["mu"], prm["a0"], prm["a2p"], prm["ka"],
      prm["bonus"], prm["g2"], prm["lnw"], prm["lnb"], prm["bd_ones"], prm["bd_mean"])


def _pack_state_pairs(s):
    lead = s.shape[:-3]
    s = s.reshape(lead + (N_HEADS // 2, 2, HEAD_DIM, HEAD_DIM))
    z = jnp.zeros_like(s[..., 0, :, :])
    top = jnp.concatenate([s[..., 0, :, :], z], axis=-1)
    bot = jnp.concatenate([z, s[..., 1, :, :]], axis=-1)
    return jnp.concatenate([top, bot], axis=-2)


def _unpack_state_pairs(sp):
    lead = sp.shape[:-3]
    a = sp[..., :HEAD_DIM, :HEAD_DIM]
    b = sp[..., HEAD_DIM:, HEAD_DIM:]
    return jnp.stack([a, b], axis=-3).reshape(lead + (N_HEADS, HEAD_DIM, HEAD_DIM))


def kernel(x_prompt, x_sample, c, cache_na_k, cache_na_v, state_rwkv, cache_diff_k, cache_diff_v, c_ctx, w_ada, b_ada, norm_mix, norm_ffn, norm_final, w_in_even, w_out_even, na_rpb, rw_mu, rw_w0, rw_w2, rw_a0, rw_a2, rw_kk, rw_ka, rw_bonus, rw_g2, rw_lnw, rw_lnb, w_qkv_diff, w_out_diff, diff_lam_q, diff_lam_k, diff_subln, ffn_w1, ffn_w3, ffn_w2):
    x = jnp.concatenate([x_prompt.reshape(N_CTX_ROWS, D_MODEL), x_sample.reshape(N_LAT_ROWS, D_MODEL)], axis=0)

    cv8 = jnp.concatenate([c_ctx[None, :], c, jnp.zeros((8 - 1 - DEC_BATCH, D_MODEL), F32)], axis=0)
    mods = _ada_all(cv8, w_ada, b_ada)
    mods = mods.reshape(DEPTH, 8, 6, D_MODEL).transpose(0, 2, 1, 3)[:, :, :1 + DEC_BATCH, None, :]

    hd_idx = jnp.arange(D_RWKV) // HEAD_DIM
    bd_ones = (hd_idx[:, None] == hd_idx[None, :]).astype(BF16)
    bd_mean = (bd_ones.astype(F32) / HEAD_DIM).astype(BF16)
    cos_t, sin_t = _rope_tables()

    na_k_out, na_v_out, rw_out, dk_out, dv_out = [], [], [], [], []
    for l in range(DEPTH):
        md = mods[l]
        if l % 2 == 0:
            e = l // 2
            q, k, v, u = _norm_proj(x, norm_mix[l], md[0], md[1], w_in_even[e].astype(BF16),
                                    (D_NA, D_NA, D_NA, P_RWKV), "proj_even")
            o_ctx = _na_ctx(q, k, v)
            o_lat = _na_lat(q, k, v, cache_na_k[:, e].reshape(DEC_BATCH, PAST_LEN, D_NA),
                            cache_na_v[:, e].reshape(DEC_BATCH, PAST_LEN, D_NA), _na_bias_table(na_rpb[e]))

            zpad = jnp.zeros((2, LANES - DECAY_LORA, D_RWKV), F32)
            prm = {
                "mu": rw_mu[e],
                "w0": rw_w0[e].reshape(2, 1, D_RWKV),
                "w2p": jnp.concatenate([rw_w2[e], zpad], axis=1),
                "a0": rw_a0[e].reshape(2, 1, D_RWKV),
                "a2p": jnp.concatenate([zpad, rw_a2[e]], axis=1),
                "kk": rw_kk[e].reshape(1, D_RWKV),
                "ka": rw_ka[e].reshape(1, D_RWKV),
                "bonus": rw_bonus[e].reshape(2, 1, D_RWKV),
                "g2": rw_g2[e],
                "lnw": rw_lnw[e].reshape(1, D_RWKV),
                "lnb": rw_lnb[e].reshape(1, D_RWKV),
                "bd_ones": bd_ones,
                "bd_mean": bd_mean,
            }
            y_ctx, s_ctx = _rwkv_scan(u, None, prm, 0, BATCH, SEQ)
            y_lat, _ = _rwkv_scan(u, _pack_state_pairs(state_rwkv[:, e]), prm, N_CTX_ROWS, DEC_BATCH, DEC_SEQ)
            o_rw = _rwkv_finish(u, y_ctx, y_lat, prm)

            w_out = w_out_even[e].astype(BF16)
            x = _out_proj(x, md[2], [(o_ctx, o_lat), o_rw], [w_out[:D_NA], w_out[D_NA:]], "out_even")

            na_k_out.append(k[:N_CTX_ROWS].reshape(BATCH, SEQ, N_HEADS, HEAD_DIM))
            na_v_out.append(v[:N_CTX_ROWS].reshape(BATCH, SEQ, N_HEADS, HEAD_DIM))
            rw_out.append(_unpack_state_pairs(s_ctx))
        else:
            o = l // 2
            lam_init = 0.8 - 0.6 * math.exp(-0.3 * l)
            q, k, v = _norm_proj(x, norm_mix[l], md[0], md[1], w_qkv_diff[o].astype(BF16),
                                 (D_DIFF, D_DIFF, D_DIFF), "proj_odd")
            o_ctx = _diff_ctx(q, k, v, diff_lam_q[o], diff_lam_k[o], diff_subln[o], lam_init)
            q_r, k_r, v_t = _rope_cast(q, k, v, cos_t, sin_t)
            o_lat = _diff_lat(q_r, k_r, v_t, cache_diff_k[:, o].reshape(DEC_BATCH, PAST_LEN, D_DIFF),
                              cache_diff_v[:, o].reshape(DEC_BATCH, PAST_LEN, D_DIFF).swapaxes(1, 2),
                              diff_lam_q[o], diff_lam_k[o], diff_subln[o], lam_init)
            x = _out_proj(x, md[2], [(o_ctx, o_lat)], [w_out_diff[o].astype(BF16)], "out_odd")
            dk_out.append(k[:N_CTX_ROWS].reshape(BATCH, SEQ, N_HEADS, 2 * HEAD_DIM))
            dv_out.append(v[:N_CTX_ROWS].reshape(BATCH, SEQ, N_HEADS, 2 * HEAD_DIM))
        x = _ffn(x, norm_ffn[l], md[3], md[4], md[5], ffn_w1[l].astype(BF16), ffn_w3[l].astype(BF16),
                 ffn_w2[l].astype(BF16), "ffn")

    y_ctx, y_lat = _final_norm(x, norm_final)
    y_prompt = y_ctx.reshape(BATCH, SEQ, D_MODEL)
    y_sample = y_lat.reshape(DEC_BATCH, DEC_SEQ, D_MODEL)
    return (y_prompt, y_sample, jnp.stack(na_k_out, axis=1), jnp.stack(na_v_out, axis=1),
            jnp.stack(rw_out, axis=1), jnp.stack(dk_out, axis=1), jnp.stack(dv_out, axis=1))
```

```python
import functools
import math

import jax
import jax.numpy as jnp
from jax import lax
from jax.experimental import pallas as pl
from jax.experimental.pallas import tpu as pltpu

F32 = jnp.float32
BF16 = jnp.bfloat16

D_MODEL = 1024
BATCH = 32
SEQ = 256
DEPTH = 4
DEC_BATCH = 2
DEC_SEQ = 4096
PAST_LEN = 512
GRID_W = 64
GRID_ROWS = DEC_SEQ // GRID_W
HEAD_DIM = 64
N_HEADS = 8
D_NA = 512
D_RWKV = 512
D_DIFF = 1024
NA_KH = 8
NA_KW = 16
DECAY_LORA = 64
ICL_LORA = 64
GATE_LORA = 128
P_RWKV = 3 * D_RWKV + DECAY_LORA + ICL_LORA + GATE_LORA
P_EVEN = 3 * D_NA + P_RWKV
D_FF = 2816
ROPE_F = HEAD_DIM // 4
ROPE_BASE = 10000.0
NORM_EPS = 1e-6
GN_EPS = 64e-5
QK_SCALE = HEAD_DIM ** -0.5
LOG2_E = math.log2(math.e)

N_CTX_ROWS = BATCH * SEQ
N_LAT_ROWS = DEC_BATCH * DEC_SEQ
M_ALL = N_CTX_ROWS + N_LAT_ROWS

LANES = 128
MXU_WIDTH = 256
CHUNK = 64
INV_BLOCK = 16
NEG_BIG = -1e30
VMEM_LIMIT = 56 * 1024 * 1024


def _cparams(sem):
    return pltpu.CompilerParams(dimension_semantics=sem, vmem_limit_bytes=VMEM_LIMIT)


def _bdot(a, b):
    return jnp.dot(a.astype(BF16), b.astype(BF16), preferred_element_type=F32)


def _bdot_nt(a, b):
    return lax.dot_general(a.astype(BF16), b.astype(BF16), (((1,), (1,)), ((), ())),
                           preferred_element_type=F32)


def _bdot_tn(a, b):
    return lax.dot_general(a.astype(BF16), b.astype(BF16), (((0,), (0,)), ((), ())),
                           preferred_element_type=F32)


def _split3(x):
    hi = x.astype(BF16)
    r1 = x - hi.astype(F32)
    mid = r1.astype(BF16)
    lo = (r1 - mid.astype(F32)).astype(BF16)
    return hi, mid, lo


def _head_sums(x, bd):
    half = bd.shape[0]
    d = functools.partial(jnp.dot, preferred_element_type=F32)
    outs = []
    for j in range(x.shape[1] // half):
        xs = x[:, j * half:(j + 1) * half]
        hi = xs.astype(BF16)
        lo = (xs - hi.astype(F32)).astype(BF16)
        outs.append(d(hi, bd) + d(lo, bd))
    return jnp.concatenate(outs, axis=1)


def _dot3_lhs_exact(e, x):
    hi, mid, lo = _split3(x)
    d = functools.partial(jnp.dot, preferred_element_type=F32)
    return d(e, hi) + d(e, mid) + d(e, lo)


def _sigmoid(x):
    return 1.0 / (1.0 + jnp.exp(-x))


def _softplus(x):
    return jnp.maximum(x, 0.0) + jnp.log(1.0 + jnp.exp(-jnp.abs(x)))


def _norm_mod(x, g, shift, scale):
    r = lax.rsqrt(jnp.mean(x * x, axis=-1, keepdims=True) + NORM_EPS)
    return ((x * r) * g) * (1.0 + scale) + shift


def _group_of_tile(i, tm):
    n_ctx_tiles = N_CTX_ROWS // tm
    tiles_per_lat = DEC_SEQ // tm
    return jnp.where(i < n_ctx_tiles, 0, 1 + (i - n_ctx_tiles) // tiles_per_lat)


def _mod_spec(tm):
    return pl.BlockSpec((1, 1, D_MODEL), lambda i, *_: (_group_of_tile(i, tm), 0, 0))


def _full_spec(shape):
    n = len(shape)
    return pl.BlockSpec(shape, lambda *_: (0,) * n)


def _ada_kernel(cv_ref, w_ref, b_ref, o_ref):
    cv = cv_ref[...]
    s = cv * _sigmoid(cv)
    o_ref[0] = _bdot(s, w_ref[0]) + b_ref[0]


def _ada_all(cv8, w_ada, b_ada):
    tn = 512
    return pl.pallas_call(
        _ada_kernel,
        grid=(DEPTH, 6 * D_MODEL // tn),
        in_specs=[
            pl.BlockSpec((8, D_MODEL), lambda l, j: (0, 0)),
            pl.BlockSpec((1, D_MODEL, tn), lambda l, j: (l, 0, j)),
            pl.BlockSpec((1, 1, tn), lambda l, j: (l, 0, j)),
        ],
        out_specs=pl.BlockSpec((1, 8, tn), lambda l, j: (l, 0, j)),
        out_shape=jax.ShapeDtypeStruct((DEPTH, 8, 6 * D_MODEL), F32),
        compiler_params=_cparams(("parallel", "parallel")),
        name="adaln",
    )(cv8, w_ada, b_ada.reshape(DEPTH, 1, 6 * D_MODEL))


def _proj_kernel(*refs, splits, cache_cols, n_prev, slot, tm):
    x_ref, g_ref, sh_ref, sc_ref, w_ref = refs[:5]
    o_refs = refs[5 + n_prev:5 + n_prev + len(splits)]
    c_refs = refs[5 + n_prev + len(splits):]
    i = pl.program_id(0)
    h = _norm_mod(x_ref[...], g_ref[...], sh_ref[0], sc_ref[0]).astype(BF16)
    off = 0
    ys = []
    for o_ref, n in zip(o_refs, splits):
        y = jnp.dot(h, w_ref[:, off:off + n], preferred_element_type=F32)
        o_ref[...] = y
        ys.append(y)
        off += n

    @pl.when(i < N_CTX_ROWS // tm)
    def _():
        for c_ref, col in zip(c_refs, cache_cols):
            for s in range(tm // SEQ):
                c_ref[s, 0] = ys[col][s * SEQ:(s + 1) * SEQ]
                if slot == 0:
                    c_ref[s, 1] = jnp.zeros((SEQ, splits[col]), F32)


def _norm_proj(x, g, shift, scale, w_bf16, splits, name, cache_cols, slot, prev_caches):
    tm = 512
    n = w_bf16.shape[1]
    n_ctx_tiles = N_CTX_ROWS // tm
    seq_per_tile = tm // SEQ
    n_prev = len(prev_caches)
    n_slots = 2
    if slot == 0:
        cache_specs = [pl.BlockSpec((seq_per_tile, n_slots, SEQ, splits[c]),
                                    lambda i: (jnp.minimum(i, n_ctx_tiles - 1), 0, 0, 0)) for c in cache_cols]
    else:
        cache_specs = [pl.BlockSpec((seq_per_tile, 1, SEQ, splits[c]),
                                    lambda i: (jnp.minimum(i, n_ctx_tiles - 1), slot, 0, 0)) for c in cache_cols]
    outs = pl.pallas_call(
        functools.partial(_proj_kernel, splits=splits, cache_cols=tuple(cache_cols), n_prev=n_prev,
                          slot=slot, tm=tm),
        grid=(M_ALL // tm,),
        in_specs=[
            pl.BlockSpec((tm, D_MODEL), lambda i: (i, 0)),
            _full_spec((1, D_MODEL)),
            _mod_spec(tm),
            _mod_spec(tm),
            _full_spec((D_MODEL, n)),
        ] + [pl.BlockSpec(memory_space=pl.ANY)] * n_prev,
        out_specs=[pl.BlockSpec((tm, s), lambda i: (i, 0)) for s in splits] + cache_specs,
        out_shape=[jax.ShapeDtypeStruct((M_ALL, s), F32) for s in splits]
        + [jax.ShapeDtypeStruct((BATCH, n_slots, SEQ, splits[c]), F32) for c in cache_cols],
        input_output_aliases={5 + j: len(splits) + j for j in range(n_prev)},
        compiler_params=_cparams(("arbitrary",)),
        name=name,
    )(x, g.reshape(1, D_MODEL), shift, scale, w_bf16, *prev_caches)
    return outs[:len(splits)], outs[len(splits):]


def _part_specs(tm, ncols):
    n_ctx_tiles = N_CTX_ROWS // tm
    return [pl.BlockSpec((tm, ncols), lambda i, *_: (jnp.minimum(i, n_ctx_tiles - 1), 0)),
            pl.BlockSpec((tm, ncols), lambda i, *_: (jnp.maximum(i - n_ctx_tiles, 0), 0))]


def _pick_part(i, tm, ctx_ref, lat_ref):
    return jnp.where(i < N_CTX_ROWS // tm, ctx_ref[...], lat_ref[...])


def _out_kernel(x_ref, gate_ref, *refs, split, tm):
    i = pl.program_id(0)
    n_act = sum(2 if s else 1 for s in split)
    a_refs, w_refs, o_ref = refs[:n_act], refs[n_act:n_act + len(split)], refs[n_act + len(split)]
    acc = None
    pos = 0
    for is_split, w_ref in zip(split, w_refs):
        if is_split:
            a = _pick_part(i, tm, a_refs[pos], a_refs[pos + 1])
            pos += 2
        else:
            a = a_refs[pos][...]
            pos += 1
        t = jnp.dot(a, w_ref[...], preferred_element_type=F32)
        acc = t if acc is None else acc + t
    o_ref[...] = x_ref[...] + gate_ref[0] * acc


def _out_proj(x, gate, acts, ws, name):
    tm = 512
    in_specs = [pl.BlockSpec((tm, D_MODEL), lambda i: (i, 0)), _mod_spec(tm)]
    flat, split = [], []
    for a in acts:
        if isinstance(a, tuple):
            in_specs += _part_specs(tm, a[0].shape[1])
            flat += list(a)
            split.append(True)
        else:
            in_specs.append(pl.BlockSpec((tm, a.shape[1]), lambda i: (i, 0)))
            flat.append(a)
            split.append(False)
    return pl.pallas_call(
        functools.partial(_out_kernel, split=tuple(split), tm=tm),
        grid=(M_ALL // tm,),
        in_specs=in_specs + [_full_spec(w.shape) for w in ws],
        out_specs=pl.BlockSpec((tm, D_MODEL), lambda i: (i, 0)),
        out_shape=jax.ShapeDtypeStruct((M_ALL, D_MODEL), F32),
        compiler_params=_cparams(("parallel",)),
        name=name,
    )(x, gate, *flat, *ws)


def _ffn_kernel(x_ref, g_ref, sh_ref, sc_ref, gate_ref, w1_ref, w3_ref, w2_ref, o_ref):
    x = x_ref[...]
    h = _norm_mod(x, g_ref[...], sh_ref[0], sc_ref[0]).astype(BF16)
    a = jnp.dot(h, w1_ref[...], preferred_element_type=F32)
    b = jnp.dot(h, w3_ref[...], preferred_element_type=F32)
    gated = ((a * _sigmoid(a)) * b).astype(BF16)
    o_ref[...] = x + gate_ref[0] * jnp.dot(gated, w2_ref[...], preferred_element_type=F32)


def _resident_spec(shape):
    n = len(shape)
    return pl.BlockSpec(shape, lambda *_: (0,) * n, pipeline_mode=pl.Buffered(1))


def _ffn(x, g, shift, scale, gate, w1, w3, w2, name):
    tm = 512
    return pl.pallas_call(
        _ffn_kernel,
        grid=(M_ALL // tm,),
        in_specs=[
            pl.BlockSpec((tm, D_MODEL), lambda i: (i, 0)),
            _full_spec((1, D_MODEL)),
            _mod_spec(tm),
            _mod_spec(tm),
            _mod_spec(tm),
            _resident_spec((D_MODEL, D_FF)),
            _resident_spec((D_MODEL, D_FF)),
            _resident_spec((D_FF, D_MODEL)),
        ],
        out_specs=pl.BlockSpec((tm, D_MODEL), lambda i: (i, 0)),
        out_shape=jax.ShapeDtypeStruct((M_ALL, D_MODEL), F32),
        compiler_params=_cparams(("parallel",)),
        name=name,
    )(x, g.reshape(1, D_MODEL), shift, scale, gate, w1, w3, w2)


def _final_norm_kernel(x_ref, g_ref, oc_ref, ol_ref, *, tm):
    i = pl.program_id(0)
    x = x_ref[...]
    r = lax.rsqrt(jnp.mean(x * x, axis=-1, keepdims=True) + NORM_EPS)
    y = (x * r) * g_ref[...]

    @pl.when(i < N_CTX_ROWS // tm)
    def _():
        oc_ref[...] = y

    @pl.when(i >= N_CTX_ROWS // tm)
    def _():
        ol_ref[...] = y


def _final_norm(x, g):
    tm = 1024
    return pl.pallas_call(
        functools.partial(_final_norm_kernel, tm=tm),
        grid=(M_ALL // tm,),
        in_specs=[pl.BlockSpec((tm, D_MODEL), lambda i: (i, 0)), _full_spec((1, D_MODEL))],
        out_specs=_part_specs(tm, D_MODEL),
        out_shape=[jax.ShapeDtypeStruct((N_CTX_ROWS, D_MODEL), F32),
                   jax.ShapeDtypeStruct((N_LAT_ROWS, D_MODEL), F32)],
        compiler_params=_cparams(("arbitrary",)),
        name="final_norm",
    )(x, g.reshape(1, D_MODEL))


def _half_masks():
    lane = lax.broadcasted_iota(jnp.int32, (1, LANES), 1)
    return (lane < HEAD_DIM, lane >= HEAD_DIM)


def _na_ctx_kernel(q_ref, k_ref, v_ref, o_ref):
    masks = _half_masks()
    work = []
    for p in range(D_NA // LANES):
        sl = slice(p * LANES, (p + 1) * LANES)
        q = q_ref[:, sl] * (QK_SCALE * LOG2_E)
        k = k_ref[:, sl].astype(BF16)
        v = v_ref[:, sl]
        for m in masks:
            work.append(dict(s=_bdot_nt(jnp.where(m, q, 0.0), k), vm=jnp.where(m, v, 0.0).astype(BF16)))
    for w in work:
        e = jnp.exp2(w["s"] - jnp.max(w["s"], axis=-1, keepdims=True))
        w["inv_l"] = 1.0 / jnp.sum(e, axis=-1, keepdims=True)
        w["e"] = e.astype(BF16)
    for w in work:
        w["o"] = jnp.dot(w["e"], w["vm"], preferred_element_type=F32) * w["inv_l"]
    for p in range(D_NA // LANES):
        o_ref[:, p * LANES:(p + 1) * LANES] = (work[2 * p]["o"] + work[2 * p + 1]["o"]).astype(BF16)


def _na_ctx(q, k, v):
    spec = pl.BlockSpec((SEQ, D_NA), lambda b: (b, 0))
    return pl.pallas_call(
        _na_ctx_kernel,
        grid=(BATCH,),
        in_specs=[spec, spec, spec],
        out_specs=spec,
        out_shape=jax.ShapeDtypeStruct((N_CTX_ROWS, D_NA), BF16),
        compiler_params=_cparams(("parallel",)),
        name="na_ctx",
    )(q, k, v)


def _na_lat_kernel(q_ref, k_ref, v_ref, kc_ref, vc_ref, bias_ref, o_ref, *, rows_per_step):
    n_loc = NA_KH * GRID_W
    masks = _half_masks()
    kc = kc_ref[0].astype(BF16)
    vc = vc_ref[0]
    vcm = [jnp.where(m, vc, 0.0).astype(BF16) for m in masks]
    work = []
    for rr in range(rows_per_step):
        i = pl.program_id(2) * rows_per_step + rr
        r0 = jnp.clip(i - NA_KH // 2, 0, GRID_ROWS - NA_KH)
        start = pl.multiple_of(r0 * GRID_W, GRID_W)
        q = q_ref[rr * GRID_W:(rr + 1) * GRID_W, :] * QK_SCALE
        kl = k_ref[pl.ds(start, n_loc), :].astype(BF16)
        vl = v_ref[pl.ds(start, n_loc), :]
        for hh, m in enumerate(masks):
            qm = jnp.where(m, q, 0.0).astype(BF16)
            work.append(dict(rr=rr, hh=hh, vlm=jnp.where(m, vl, 0.0).astype(BF16),
                             sl=_bdot_nt(qm, kl) + bias_ref[r0 - i + NA_KH - 1, hh], sc=_bdot_nt(qm, kc)))
    for w in work:
        mx = jnp.maximum(jnp.max(w["sl"], axis=-1, keepdims=True), jnp.max(w["sc"], axis=-1, keepdims=True))
        w["el"] = jnp.exp(w["sl"] - mx)
        w["ec"] = jnp.exp(w["sc"] - mx)
    for w in work:
        inv_l = 1.0 / (jnp.sum(w["el"], axis=-1, keepdims=True) + jnp.sum(w["ec"], axis=-1, keepdims=True))
        w["o"] = (_bdot(w["el"], w["vlm"]) + _bdot(w["ec"], vcm[w["hh"]])) * inv_l
    for rr in range(rows_per_step):
        a, b = [w["o"] for w in work if w["rr"] == rr]
        o_ref[rr * GRID_W:(rr + 1) * GRID_W, :] = (a + b).astype(BF16)


def _na_bias_table(rpb):
    jj = jnp.arange(GRID_W)[:, None]
    cc = jnp.arange(GRID_W)[None, :]
    c0 = jnp.clip(jj - NA_KW // 2, 0, GRID_W - NA_KW)
    inwin = (cc >= c0) & (cc < c0 + NA_KW)
    idx = jnp.clip(cc - jj + NA_KW - 1, 0, 2 * NA_KW - 2)
    full = rpb[:, :, idx]
    full = jnp.where(inwin[None, None], full, NEG_BIG)
    tabs = []
    for v in range(NA_KH):
        t = full[:, v:v + NA_KH]
        tabs.append(t.transpose(0, 2, 1, 3).reshape(N_HEADS, GRID_W, NA_KH * GRID_W))
    return jnp.stack(tabs, axis=0)


def _na_lat(q, k, v, kc, vc, bias_tab):
    rows_per_step = 4
    tq = rows_per_step * GRID_W
    steps = GRID_ROWS // rows_per_step
    lat_blk = N_CTX_ROWS // DEC_SEQ
    row_blk = N_CTX_ROWS // tq
    return pl.pallas_call(
        functools.partial(_na_lat_kernel, rows_per_step=rows_per_step),
        grid=(DEC_BATCH, D_NA // LANES, steps),
        in_specs=[
            pl.BlockSpec((tq, LANES), lambda b, p, i: (row_blk + b * steps + i, p)),
            pl.BlockSpec((DEC_SEQ, LANES), lambda b, p, i: (lat_blk + b, p)),
            pl.BlockSpec((DEC_SEQ, LANES), lambda b, p, i: (lat_blk + b, p)),
            pl.BlockSpec((1, PAST_LEN, LANES), lambda b, p, i: (b, 0, p)),
            pl.BlockSpec((1, PAST_LEN, LANES), lambda b, p, i: (b, 0, p)),
            pl.BlockSpec((NA_KH, 2, GRID_W, NA_KH * GRID_W), lambda b, p, i: (0, p, 0, 0)),
        ],
        out_specs=pl.BlockSpec((tq, LANES), lambda b, p, i: (b * steps + i, p)),
        out_shape=jax.ShapeDtypeStruct((N_LAT_ROWS, D_NA), BF16),
        compiler_params=_cparams(("parallel", "parallel", "arbitrary")),
        name="na_lat",
    )(q, k, v, kc, vc, bias_tab)


def _lam_value(lq_ref, lk_ref, lam_init):
    s = jnp.sum(lq_ref[...] * lk_ref[...], axis=-1, keepdims=True)
    return jnp.exp(s[0:1]) - jnp.exp(s[1:2]) + lam_init


def _subln(o, sub, lam_init):
    r = lax.rsqrt(jnp.mean(o * o, axis=-1, keepdims=True) + NORM_EPS)
    return ((o * r) * sub) * (1.0 - lam_init)


def _diff_ctx_kernel(lq_ref, lk_ref, sub_ref, q_ref, k_ref, v_ref, o_ref, *, lam_init):
    lam = _lam_value(lq_ref, lk_ref, lam_init)
    masks = _half_masks()
    work = []
    for h in range(N_HEADS):
        sl = slice(h * LANES, (h + 1) * LANES)
        q = q_ref[:, sl] * (QK_SCALE * LOG2_E)
        k = k_ref[:, sl].astype(BF16)
        for m in masks:
            work.append(dict(s=_bdot_nt(jnp.where(m, q, 0.0), k)))
    for w in work:
        e = jnp.exp2(w["s"] - jnp.max(w["s"], axis=-1, keepdims=True))
        w["p"] = e * (1.0 / jnp.sum(e, axis=-1, keepdims=True))
    outs = []
    for h in range(N_HEADS):
        att = work[2 * h]["p"] - lam * work[2 * h + 1]["p"]
        outs.append(_bdot(att, v_ref[:, h * LANES:(h + 1) * LANES]))
    for h in range(N_HEADS):
        o_ref[:, h * LANES:(h + 1) * LANES] = _subln(outs[h], sub_ref[...], lam_init).astype(BF16)


def _diff_ctx(q, k, v, lam_q, lam_k, subln, lam_init):
    spec = pl.BlockSpec((SEQ, D_DIFF), lambda b: (b, 0))
    return pl.pallas_call(
        functools.partial(_diff_ctx_kernel, lam_init=lam_init),
        grid=(BATCH,),
        in_specs=[_full_spec((2, HEAD_DIM)), _full_spec((2, HEAD_DIM)), _full_spec((1, LANES)),
                  spec, spec, spec],
        out_specs=spec,
        out_shape=jax.ShapeDtypeStruct((N_CTX_ROWS, D_DIFF), BF16),
        compiler_params=_cparams(("parallel",)),
        name="diff_ctx",
    )(lam_q, lam_k, subln.reshape(1, LANES), q, k, v)


def _rope_kernel(q_ref, k_ref, v_ref, cos_ref, sin_ref, qo_ref, ko_ref, vt_ref):
    cos = cos_ref[...]
    sin = sin_ref[...]
    lane = lax.broadcasted_iota(jnp.int32, (1, LANES), 1)
    first = (lane % (2 * ROPE_F)) < ROPE_F
    for x_ref, o_ref, scale in ((q_ref, qo_ref, QK_SCALE * LOG2_E), (k_ref, ko_ref, None)):
        for j in range(D_DIFF // LANES):
            x = x_ref[:, j * LANES:(j + 1) * LANES]
            partner = jnp.where(first, pltpu.roll(x, LANES - ROPE_F, 1), pltpu.roll(x, ROPE_F, 1))
            y = x * cos + partner * sin
            if scale is not None:
                y = y * scale
            o_ref[:, j * LANES:(j + 1) * LANES] = y.astype(BF16)
    vt_ref[0] = v_ref[...].T.astype(BF16)


def _rope_tables():
    t = jnp.arange(DEC_SEQ)
    pos = jnp.stack([t // GRID_W, t % GRID_W], -1).astype(F32)
    inv = ROPE_BASE ** (-jnp.arange(ROPE_F, dtype=F32) / ROPE_F)
    ang = pos[:, :, None] * inv
    cos, sin = jnp.cos(ang), jnp.sin(ang)
    cos64 = jnp.concatenate([cos, cos], axis=-1).reshape(DEC_SEQ, HEAD_DIM)
    sin64 = jnp.concatenate([-sin, sin], axis=-1).reshape(DEC_SEQ, HEAD_DIM)
    return jnp.tile(cos64, (1, 2)), jnp.tile(sin64, (1, 2))


def _rope_cast(q, k, v, cos_t, sin_t):
    tm = 512
    base = N_CTX_ROWS // tm
    per_seq = DEC_SEQ // tm
    row_spec = pl.BlockSpec((tm, D_DIFF), lambda i: (base + i, 0))
    tab_spec = pl.BlockSpec((tm, LANES), lambda i: (i % per_seq, 0))
    out_spec = pl.BlockSpec((tm, D_DIFF), lambda i: (i, 0))
    return pl.pallas_call(
        _rope_kernel,
        grid=(N_LAT_ROWS // tm,),
        in_specs=[row_spec, row_spec, row_spec, tab_spec, tab_spec],
        out_specs=[out_spec, out_spec,
                   pl.BlockSpec((1, D_DIFF, tm), lambda i: (i // per_seq, 0, i % per_seq))],
        out_shape=[jax.ShapeDtypeStruct((N_LAT_ROWS, D_DIFF), BF16),
                   jax.ShapeDtypeStruct((N_LAT_ROWS, D_DIFF), BF16),
                   jax.ShapeDtypeStruct((DEC_BATCH, D_DIFF, DEC_SEQ), BF16)],
        compiler_params=_cparams(("parallel",)),
        name="rope_cast",
    )(q, k, v, cos_t, sin_t)


def _diff_lat_kernel(lq_ref, lk_ref, sub_ref, q_ref, kl_ref, vlt_ref, kc_ref, vct_ref, o_ref, *,
                     lam_init, sub_q, n_sub):
    lam = _lam_value(lq_ref, lk_ref, lam_init)
    kl = kl_ref[...]
    kc = kc_ref[0].astype(BF16)
    vlt = vlt_ref[0]
    vct = vct_ref[0].astype(BF16)
    work = []
    for t in range(n_sub):
        q = q_ref[t * sub_q:(t + 1) * sub_q, :]
        for m in _half_masks():
            qm = jnp.where(m, q, jnp.zeros_like(q))
            work.append(dict(sl=_bdot_nt(kl, qm), sc=_bdot_nt(kc, qm)))
    for w in work:
        mx = jnp.maximum(jnp.max(w["sl"], axis=0, keepdims=True), jnp.max(w["sc"], axis=0, keepdims=True))
        el = jnp.exp2(w["sl"] - mx)
        ec = jnp.exp2(w["sc"] - mx)
        w["inv_l"] = 1.0 / (jnp.sum(el, axis=0, keepdims=True) + jnp.sum(ec, axis=0, keepdims=True))
        w["el"] = el.astype(BF16)
        w["ec"] = ec.astype(BF16)
    for w in work:
        w["o"] = (jnp.dot(vlt, w["el"], preferred_element_type=F32)
                  + jnp.dot(vct, w["ec"], preferred_element_type=F32)) * w["inv_l"]
    for t in range(n_sub):
        ot = work[2 * t]["o"] - lam * work[2 * t + 1]["o"]
        r = lax.rsqrt(jnp.mean(ot * ot, axis=0, keepdims=True) + NORM_EPS)
        on = ((ot * r) * sub_ref[...]) * (1.0 - lam_init)
        o_ref[t * sub_q:(t + 1) * sub_q, :] = on.T.astype(BF16)


def _diff_lat(q_r, k_r, v_t, kc, vc_t, lam_q, lam_k, subln, lam_init):
    sub_q, n_sub = 256, 2
    tq = sub_q * n_sub
    per_seq = DEC_SEQ // tq
    return pl.pallas_call(
        functools.partial(_diff_lat_kernel, lam_init=lam_init, sub_q=sub_q, n_sub=n_sub),
        grid=(DEC_BATCH, N_HEADS, per_seq),
        in_specs=[
            _full_spec((2, HEAD_DIM)),
            _full_spec((2, HEAD_DIM)),
            _full_spec((LANES, 1)),
            pl.BlockSpec((tq, LANES), lambda b, h, i: (b * per_seq + i, h)),
            pl.BlockSpec((DEC_SEQ, LANES), lambda b, h, i: (b, h)),
            pl.BlockSpec((1, LANES, DEC_SEQ), lambda b, h, i: (b, h, 0)),
            pl.BlockSpec((1, PAST_LEN, LANES), lambda b, h, i: (b, 0, h)),
            pl.BlockSpec((1, LANES, PAST_LEN), lambda b, h, i: (b, h, 0)),
        ],
        out_specs=pl.BlockSpec((tq, LANES), lambda b, h, i: (b * per_seq + i, h)),
        out_shape=jax.ShapeDtypeStruct((N_LAT_ROWS, D_DIFF), BF16),
        compiler_params=_cparams(("parallel", "parallel", "arbitrary")),
        name="diff_lat",
    )(lam_q, lam_k, subln.reshape(LANES, 1), q_r, k_r, v_t, kc, vc_t)


def _token_shift(u, prev_row, next_row, mu):
    n = u.shape[0]
    rows = lax.broadcasted_iota(jnp.int32, (n, 1), 0)
    up = jnp.where(rows == 0, prev_row, pltpu.roll(u, 1, 0))
    un = jnp.where(rows == n - 1, next_row, pltpu.roll(u, n - 1, 0))
    return u + mu[0:1] * (up - u) + mu[1:2] * (un - u)


def _icl_rate_and_key(us, a0, a2p, k_a):
    lo = us[:, 3 * D_RWKV:3 * D_RWKV + LANES]
    k = us[:, D_RWKV:2 * D_RWKV]
    a = _sigmoid(a0 + _bdot(lo, a2p))
    return a, k * (1.0 + (a - 1.0) * k_a)


def _stack_blockdiag(x):
    shape = (2 * x.shape[0], x.shape[1])
    row = lax.broadcasted_iota(jnp.int32, shape, 0)
    col = lax.broadcasted_iota(jnp.int32, shape, 1)
    keep = (row // HEAD_DIM) == ((col % LANES) // HEAD_DIM)
    return jnp.where(keep, jnp.concatenate([x, x], axis=0), 0.0).astype(BF16)


def _rwkv_chunk_inputs(u, prev_row, next_row, reverse, mu, w0, w2p, a0, a2p, k_k, k_a, bd_ones):
    us = _token_shift(u, prev_row, next_row, mu)
    r = us[:, 0:D_RWKV]
    k = us[:, D_RWKV:2 * D_RWKV]
    v = us[:, 2 * D_RWKV:3 * D_RWKV]
    lo = us[:, 3 * D_RWKV:3 * D_RWKV + LANES]
    w_raw = w0 + _bdot(jnp.tanh(lo), w2p)
    logw = -jnp.exp(-_softplus(-w_raw) - 0.5)
    a, kt = _icl_rate_and_key(us, a0, a2p, k_a)
    kk_raw = k * k_k
    ss = _head_sums(kk_raw * kk_raw, bd_ones)
    kk = kk_raw * lax.rsqrt(jnp.maximum(ss, 1e-12))
    row = lax.broadcasted_iota(jnp.int32, (CHUNK, CHUNK), 0)
    col = lax.broadcasted_iota(jnp.int32, (CHUNK, CHUNK), 1)
    seen = (col >= row) if reverse else (col <= row)
    cum = _dot3_lhs_exact(seen.astype(F32).astype(BF16), logw)
    p_inv = jnp.exp(-cum)
    tot = cum[0:1, :] if reverse else cum[CHUNK - 1:CHUNK, :]
    return dict(a_hat=jnp.exp(cum - logw) * kk, b_hat=(kk * a) * p_inv, k_hat=kt * p_inv,
                r_hat=r * jnp.exp(cum), v=v, p_tot=jnp.exp(tot))


def _rwkv_scan_kernel(*refs, n_chunks, n_par, has_state):
    u_refs = (refs[0:3], refs[3:6])
    pos = 6
    s0_ref = None
    if has_state:
        s0_ref = refs[pos]
        pos += 1
    mu_ref, w0_ref, w2_ref, a0_ref, a2_ref, kk_ref, ka_ref, bd_ref = refs[pos:pos + 8]
    y_refs = refs[pos + 8:pos + 10]
    sfin_ref, s_ref = refs[pos + 10], refs[pos + 11]
    c = pl.program_id(1)
    n_pairs = D_RWKV // LANES

    @pl.when(c == 0)
    def _():
        if has_state:
            s_ref[...] = s0_ref[...]
        else:
            s_ref[...] = jnp.zeros(s_ref.shape, F32)

    at_start = jnp.where(c == 0, 0.0, 1.0)
    at_end = jnp.where(c == n_chunks - 1, 0.0, 1.0)
    has_prev = (at_start, at_end)
    has_next = (at_end, at_start)

    row = lax.broadcasted_iota(jnp.int32, (CHUNK, LANES), 0)
    colm = lax.broadcasted_iota(jnp.int32, (CHUNK, LANES), 1) % CHUNK
    strict = (colm < row, colm > row)
    incl = (colm <= row, colm >= row)
    same_blk = (row // INV_BLOCK) == (colm // INV_BLOCK)
    eye = (row == colm).astype(F32)
    prow = lax.broadcasted_iota(jnp.int32, (LANES, LANES), 0)
    pcol = lax.broadcasted_iota(jnp.int32, (LANES, LANES), 1)
    blockdiag = (prow // HEAD_DIM) == (pcol // HEAD_DIM)
    eye_pair = (prow == pcol).astype(F32)

    chains = []
    for j in range(n_par):
        for d in range(2):
            u_ref, up_ref, un_ref = u_refs[d]
            q = _rwkv_chunk_inputs(
                u_ref[j], up_ref[j, 7:8, :] * has_prev[d], un_ref[j, 0:1, :] * has_next[d], d == 1,
                mu_ref[...], w0_ref[d], w2_ref[d], a0_ref[d], a2_ref[d], kk_ref[...], ka_ref[...], bd_ref[...])
            for p in range(n_pairs):
                sl = slice(p * LANES, (p + 1) * LANES)
                chains.append(dict(j=j, d=d, p=p, sl=sl, ah=q["a_hat"][:, sl], bh=q["b_hat"][:, sl],
                                   kh=q["k_hat"][:, sl], rh=q["r_hat"][:, sl], vv=q["v"][:, sl],
                                   pt=q["p_tot"][:, sl]))

    def stage(fn):
        for ch in chains:
            fn(ch)

    def s_scores(ch):
        ar = jnp.concatenate([ch["ah"], ch["rh"]], axis=0)
        x = _bdot_nt(ar, jnp.concatenate([_stack_blockdiag(ch["bh"]), _stack_blockdiag(ch["kh"])], axis=0))
        xb, xk = x[:, :LANES], x[:, LANES:]
        d = ch["d"]
        l_ab = jnp.where(strict[d], xb[:CHUNK], 0.0)
        ch["l_ak"] = jnp.where(strict[d], xk[:CHUNK], 0.0)
        ch["m_rb"] = jnp.where(incl[d], xb[CHUNK:], 0.0)
        ch["m_rk"] = jnp.where(incl[d], xk[CHUNK:], 0.0)
        ch["x1"] = -jnp.where(same_blk, l_ab, 0.0)
        ch["l_off"] = jnp.where(same_blk, 0.0, l_ab)
        ch["vs"] = _stack_blockdiag(ch["vv"])
    stage(s_scores)

    def s_x2(ch):
        ch["x2"] = _bdot(ch["x1"], _stack_blockdiag(ch["x1"]))
        ch["x2s"] = _stack_blockdiag(ch["x2"])
        ch["u"] = _bdot(ch["l_ak"], ch["vs"])
    stage(s_x2)

    def s_x4(ch):
        ch["x4"] = _bdot(ch["x2"], ch["x2s"])
        td = eye + ch["x1"]
        ch["td"] = td + _bdot(td, ch["x2s"])
    stage(s_x4)

    def s_x8(ch):
        x4s = _stack_blockdiag(ch["x4"])
        ch["x8s"] = _stack_blockdiag(_bdot(ch["x4"], x4s))
        ch["td"] = ch["td"] + _bdot(ch["td"], x4s)
    stage(s_x8)

    def s_td(ch):
        ch["td"] = ch["td"] + _bdot(ch["td"], ch["x8s"])
    stage(s_td)

    def s_mm(ch):
        ch["mm"] = _bdot(ch["td"], _stack_blockdiag(ch["l_off"]))
    stage(s_mm)

    def s_m2(ch):
        ch["m2s"] = _stack_blockdiag(_bdot(ch["mm"], _stack_blockdiag(ch["mm"])))
    stage(s_m2)

    def s_n2(ch):
        n1 = eye - ch["mm"]
        ch["n2"] = n1 + _bdot(n1, ch["m2s"])
    stage(s_n2)

    def s_tinv(ch):
        ch["t_inv"] = _bdot(ch["n2"], _stack_blockdiag(ch["td"]))
    stage(s_tinv)

    def s_tx(ch):
        ch["tx"] = _bdot(ch["t_inv"], _stack_blockdiag(jnp.concatenate([ch["ah"], ch["u"]], axis=1)))
    stage(s_tx)

    def s_out(ch):
        tx = ch["tx"]
        mx = _bdot(ch["m_rb"], _stack_blockdiag(tx))
        q_eff = ch["rh"] - mx[:, :LANES]
        y_loc = _bdot(ch["m_rk"], ch["vs"]) - mx[:, LANES:]
        g_mat = jnp.where(blockdiag, eye_pair - _bdot_tn(tx[:, :LANES], ch["bh"]), 0.0) * ch["pt"]
        h_mat = jnp.where(
            blockdiag,
            _bdot_tn(jnp.concatenate([ch["vv"], -tx[:, LANES:]], axis=0),
                     jnp.concatenate([ch["kh"], ch["bh"]], axis=0)),
            0.0) * ch["pt"]
        j, d, p = ch["j"], ch["d"], ch["p"]
        s_old = s_ref[j, d, p]
        y_refs[d][j, :, ch["sl"]] = _bdot_nt(q_eff, s_old) + y_loc
        s_ref[j, d, p] = _bdot(s_old, g_mat) + h_mat
    stage(s_out)

    @pl.when(c == n_chunks - 1)
    def _():
        sfin_ref[...] = s_ref[...]


def _rwkv_scan(u, s0p, prm, row_base, n_seq, seq_len):
    n_par = 2
    n_chunks = seq_len // CHUNK
    n_pairs = D_RWKV // LANES
    base_blk = row_base // (seq_len * n_par)
    n_blk8 = seq_len // 8
    has_state = s0p is not None
    u3 = u.reshape(M_ALL // seq_len, seq_len, P_RWKV)

    def chunk_pos(d, c):
        return c if d == 0 else n_chunks - 1 - c

    in_specs, args = [], []
    for d in range(2):
        in_specs += [
            pl.BlockSpec((n_par, CHUNK, P_RWKV), lambda g, c, d=d: (base_blk + g, chunk_pos(d, c), 0)),
            pl.BlockSpec((n_par, 8, P_RWKV),
                         lambda g, c, d=d: (base_blk + g, jnp.maximum(chunk_pos(d, c) * (CHUNK // 8) - 1, 0), 0)),
            pl.BlockSpec((n_par, 8, P_RWKV),
                         lambda g, c, d=d: (base_blk + g,
                                            jnp.minimum((chunk_pos(d, c) + 1) * (CHUNK // 8), n_blk8 - 1), 0)),
        ]
        args += [u3, u3, u3]
    state_spec = pl.BlockSpec((n_par, 2, n_pairs, LANES, LANES), lambda g, c: (g, 0, 0, 0, 0))
    if has_state:
        in_specs.append(state_spec)
        args.append(s0p)
    in_specs += [
        _full_spec((2, P_RWKV)),
        _full_spec((2, 1, D_RWKV)),
        _full_spec((2, LANES, D_RWKV)),
        _full_spec((2, 1, D_RWKV)),
        _full_spec((2, LANES, D_RWKV)),
        _full_spec((1, D_RWKV)),
        _full_spec((1, D_RWKV)),
        _full_spec((MXU_WIDTH, MXU_WIDTH)),
    ]
    args += [prm["mu"], prm["w0"], prm["w2p"], prm["a0"], prm["a2p"], prm["kk"], prm["ka"], prm["bd_ones"]]
    y_shape = jax.ShapeDtypeStruct((n_seq, seq_len, D_RWKV), F32)
    y_f, y_b, s_fin = pl.pallas_call(
        functools.partial(_rwkv_scan_kernel, n_chunks=n_chunks, n_par=n_par, has_state=has_state),
        grid=(n_seq // n_par, n_chunks),
        in_specs=in_specs,
        out_specs=[
            pl.BlockSpec((n_par, CHUNK, D_RWKV), lambda g, c: (g, chunk_pos(0, c), 0)),
            pl.BlockSpec((n_par, CHUNK, D_RWKV), lambda g, c: (g, chunk_pos(1, c), 0)),
            state_spec,
        ],
        out_shape=[y_shape, y_shape, jax.ShapeDtypeStruct((n_seq, 2, n_pairs, LANES, LANES), F32)],
        scratch_shapes=[pltpu.VMEM((n_par, 2, n_pairs, LANES, LANES), F32)],
        compiler_params=_cparams(("parallel", "arbitrary")),
        name="rwkv_scan_" + ("lat" if has_state else "ctx"),
    )(*args)
    n_rows = n_seq * seq_len
    return (y_f.reshape(n_rows, D_RWKV), y_b.reshape(n_rows, D_RWKV)), s_fin


def _rwkv_fin_kernel(u_ref, up_ref, un_ref, ycf_ref, ylf_ref, ycb_ref, ylb_ref, mu_ref, a0_ref, a2_ref,
                     ka_ref, bonus_ref, g2_ref, lnw_ref, lnb_ref, bd1_ref, bdm_ref, o_ref, *, tm):
    i = pl.program_id(0)
    n_ctx_tiles = N_CTX_ROWS // tm
    per_seq = DEC_SEQ // tm
    j = (i - n_ctx_tiles) % per_seq
    is_ctx = i < n_ctx_tiles
    not_first = jnp.where(is_ctx | (j == 0), 0.0, 1.0)
    not_last = jnp.where(is_ctx | (j == per_seq - 1), 0.0, 1.0)
    us = _token_shift(u_ref[...], up_ref[7:8, :] * not_first, un_ref[0:1, :] * not_last, mu_ref[...])
    r = us[:, 0:D_RWKV]
    v = us[:, 2 * D_RWKV:3 * D_RWKV]
    g_lo = us[:, 3 * D_RWKV + LANES:]
    rk = jnp.zeros((tm, D_RWKV), F32)
    for d in range(2):
        _, kt = _icl_rate_and_key(us, a0_ref[d], a2_ref[d], ka_ref[...])
        rk = rk + r * kt * bonus_ref[d]
    bon = _head_sums(rk, bd1_ref[...]) * v
    y = _pick_part(i, tm, ycf_ref, ylf_ref) + _pick_part(i, tm, ycb_ref, ylb_ref)
    mean = _head_sums(y, bdm_ref[...])
    yc = y - mean
    var = _head_sums(yc * yc, bdm_ref[...])
    yn = (yc * lax.rsqrt(var + GN_EPS)) * lnw_ref[...] + lnb_ref[...]
    gate = _bdot(_sigmoid(g_lo), g2_ref[...])
    o_ref[...] = ((yn + bon) * gate).astype(BF16)


def _rwkv_finish(u, y_ctx, y_lat, prm):
    tm = 256
    n_blk8 = M_ALL // 8
    return pl.pallas_call(
        functools.partial(_rwkv_fin_kernel, tm=tm),
        grid=(M_ALL // tm,),
        in_specs=[
            pl.BlockSpec((tm, P_RWKV), lambda i: (i, 0)),
            pl.BlockSpec((8, P_RWKV), lambda i: (jnp.maximum(i * (tm // 8) - 1, 0), 0)),
            pl.BlockSpec((8, P_RWKV), lambda i: (jnp.minimum((i + 1) * (tm // 8), n_blk8 - 1), 0)),
            *_part_specs(tm, D_RWKV),
            *_part_specs(tm, D_RWKV),
            _full_spec((2, P_RWKV)),
            _full_spec((2, 1, D_RWKV)),
            _full_spec((2, LANES, D_RWKV)),
            _full_spec((1, D_RWKV)),
            _full_spec((2, 1, D_RWKV)),
            _full_spec((GATE_LORA, D_RWKV)),
            _full_spec((1, D_RWKV)),
            _full_spec((1, D_RWKV)),
            _full_spec((MXU_WIDTH, MXU_WIDTH)),
            _full_spec((MXU_WIDTH, MXU_WIDTH)),
        ],
        out_specs=pl.BlockSpec((tm, D_RWKV), lambda i: (i, 0)),
        out_shape=jax.ShapeDtypeStruct((M_ALL, D_RWKV), BF16),
        compiler_params=_cparams(("parallel",)),
        name="rwkv_finish",
    )(u, u, u, y_ctx[0], y_lat[0], y_ctx[1], y_lat[1], prm["mu"], prm["a0"], prm["a2p"], prm["ka"],
      prm["bonus"], prm["g2"], prm["lnw"], prm["lnb"], prm["bd_ones"], prm["bd_mean"])


def _pack_state_pairs(s):
    lead = s.shape[:-3]
    s = s.reshape(lead + (N_HEADS // 2, 2, HEAD_DIM, HEAD_DIM))
    z = jnp.zeros_like(s[..., 0, :, :])
    top = jnp.concatenate([s[..., 0, :, :], z], axis=-1)
    bot = jnp.concatenate([z, s[..., 1, :, :]], axis=-1)
    return jnp.concatenate([top, bot], axis=-2)


def _unpack_state_pairs(sp):
    lead = sp.shape[:-3]
    a = sp[..., :HEAD_DIM, :HEAD_DIM]
    b = sp[..., HEAD_DIM:, HEAD_DIM:]
    return jnp.stack([a, b], axis=-3).reshape(lead + (N_HEADS, HEAD_DIM, HEAD_DIM))


def kernel(x_prompt, x_sample, c, cache_na_k, cache_na_v, state_rwkv, cache_diff_k, cache_diff_v, c_ctx, w_ada, b_ada, norm_mix, norm_ffn, norm_final, w_in_even, w_out_even, na_rpb, rw_mu, rw_w0, rw_w2, rw_a0, rw_a2, rw_kk, rw_ka, rw_bonus, rw_g2, rw_lnw, rw_lnb, w_qkv_diff, w_out_diff, diff_lam_q, diff_lam_k, diff_subln, ffn_w1, ffn_w3, ffn_w2):
    x = jnp.concatenate([x_prompt.reshape(N_CTX_ROWS, D_MODEL), x_sample.reshape(N_LAT_ROWS, D_MODEL)], axis=0)

    cv8 = jnp.concatenate([c_ctx[None, :], c, jnp.zeros((8 - 1 - DEC_BATCH, D_MODEL), F32)], axis=0)
    mods = _ada_all(cv8, w_ada, b_ada)
    mods = mods.reshape(DEPTH, 8, 6, D_MODEL).transpose(0, 2, 1, 3)[:, :, :1 + DEC_BATCH, None, :]

    hd_idx = jnp.arange(MXU_WIDTH) // HEAD_DIM
    bd_ones = (hd_idx[:, None] == hd_idx[None, :]).astype(BF16)
    bd_mean = (bd_ones.astype(F32) / HEAD_DIM).astype(BF16)
    cos_t, sin_t = _rope_tables()

    rw_out = []
    na_caches, diff_caches = (), ()
    for l in range(DEPTH):
        md = mods[l]
        if l % 2 == 0:
            e = l // 2
            (q, k, v, u), na_caches = _norm_proj(
                x, norm_mix[l], md[0], md[1], w_in_even[e].astype(BF16), (D_NA, D_NA, D_NA, P_RWKV),
                "proj_even", cache_cols=(1, 2), slot=e, prev_caches=na_caches)
            o_ctx = _na_ctx(q, k, v)
            o_lat = _na_lat(q, k, v, cache_na_k[:, e].reshape(DEC_BATCH, PAST_LEN, D_NA),
                            cache_na_v[:, e].reshape(DEC_BATCH, PAST_LEN, D_NA), _na_bias_table(na_rpb[e]))

            zpad = jnp.zeros((2, LANES - DECAY_LORA, D_RWKV), F32)
            prm = {
                "mu": rw_mu[e],
                "w0": rw_w0[e].reshape(2, 1, D_RWKV),
                "w2p": jnp.concatenate([rw_w2[e], zpad], axis=1),
                "a0": rw_a0[e].reshape(2, 1, D_RWKV),
                "a2p": jnp.concatenate([zpad, rw_a2[e]], axis=1),
                "kk": rw_kk[e].reshape(1, D_RWKV),
                "ka": rw_ka[e].reshape(1, D_RWKV),
                "bonus": rw_bonus[e].reshape(2, 1, D_RWKV),
                "g2": rw_g2[e],
                "lnw": rw_lnw[e].reshape(1, D_RWKV),
                "lnb": rw_lnb[e].reshape(1, D_RWKV),
                "bd_ones": bd_ones,
                "bd_mean": bd_mean,
            }
            y_ctx, s_ctx = _rwkv_scan(u, None, prm, 0, BATCH, SEQ)
            y_lat, _ = _rwkv_scan(u, _pack_state_pairs(state_rwkv[:, e]), prm, N_CTX_ROWS, DEC_BATCH, DEC_SEQ)
            o_rw = _rwkv_finish(u, y_ctx, y_lat, prm)

            w_out = w_out_even[e].astype(BF16)
            x = _out_proj(x, md[2], [(o_ctx, o_lat), o_rw], [w_out[:D_NA], w_out[D_NA:]], "out_even")

            rw_out.append(_unpack_state_pairs(s_ctx))
        else:
            o = l // 2
            lam_init = 0.8 - 0.6 * math.exp(-0.3 * l)
            (q, k, v), diff_caches = _norm_proj(
                x, norm_mix[l], md[0], md[1], w_qkv_diff[o].astype(BF16), (D_DIFF, D_DIFF, D_DIFF),
                "proj_odd", cache_cols=(1, 2), slot=o, prev_caches=diff_caches)
            o_ctx = _diff_ctx(q, k, v, diff_lam_q[o], diff_lam_k[o], diff_subln[o], lam_init)
            q_r, k_r, v_t = _rope_cast(q, k, v, cos_t, sin_t)
            o_lat = _diff_lat(q_r, k_r, v_t, cache_diff_k[:, o].reshape(DEC_BATCH, PAST_LEN, D_DIFF),
                              cache_diff_v[:, o].reshape(DEC_BATCH, PAST_LEN, D_DIFF).swapaxes(1, 2),
                              diff_lam_q[o], diff_lam_k[o], diff_subln[o], lam_init)
            x = _out_proj(x, md[2], [(o_ctx, o_lat)], [w_out_diff[o].astype(BF16)], "out_odd")
        x = _ffn(x, norm_ffn[l], md[3], md[4], md[5], ffn_w1[l].astype(BF16), ffn_w3[l].astype(BF16),
                 ffn_w2[l].astype(BF16), "ffn")

    y_ctx, y_lat = _final_norm(x, norm_final)
    y_prompt = y_ctx.reshape(BATCH, SEQ, D_MODEL)
    y_sample = y_lat.reshape(DEC_BATCH, DEC_SEQ, D_MODEL)
    na_shape = (BATCH, DEPTH // 2, SEQ, N_HEADS, HEAD_DIM)
    diff_shape = (BATCH, DEPTH // 2, SEQ, N_HEADS, 2 * HEAD_DIM)
    return (y_prompt, y_sample, na_caches[0].reshape(na_shape), na_caches[1].reshape(na_shape),
            jnp.stack(rw_out, axis=1), diff_caches[0].reshape(diff_shape), diff_caches[1].reshape(diff_shape))
```

```python
import functools
import math

import jax
import jax.numpy as jnp
from jax import lax
from jax.experimental import pallas as pl
from jax.experimental.pallas import tpu as pltpu

F32 = jnp.float32
BF16 = jnp.bfloat16

D_MODEL = 1024
BATCH = 32
SEQ = 256
DEPTH = 4
DEC_BATCH = 2
DEC_SEQ = 4096
PAST_LEN = 512
GRID_W = 64
GRID_ROWS = DEC_SEQ // GRID_W
HEAD_DIM = 64
N_HEADS = 8
D_NA = 512
D_RWKV = 512
D_DIFF = 1024
NA_KH = 8
NA_KW = 16
DECAY_LORA = 64
ICL_LORA = 64
GATE_LORA = 128
P_RWKV = 3 * D_RWKV + DECAY_LORA + ICL_LORA + GATE_LORA
P_EVEN = 3 * D_NA + P_RWKV
D_FF = 2816
ROPE_F = HEAD_DIM // 4
ROPE_BASE = 10000.0
NORM_EPS = 1e-6
GN_EPS = 64e-5
QK_SCALE = HEAD_DIM ** -0.5
LOG2_E = math.log2(math.e)

N_CTX_ROWS = BATCH * SEQ
N_LAT_ROWS = DEC_BATCH * DEC_SEQ
M_ALL = N_CTX_ROWS + N_LAT_ROWS

LANES = 128
MXU_WIDTH = 256
CHUNK = 64
INV_BLOCK = 16
NEG_BIG = -1e30
VMEM_LIMIT = 56 * 1024 * 1024


def _cparams(sem):
    return pltpu.CompilerParams(dimension_semantics=sem, vmem_limit_bytes=VMEM_LIMIT)


def _bdot(a, b):
    return jnp.dot(a.astype(BF16), b.astype(BF16), preferred_element_type=F32)


def _bdot_nt(a, b):
    return lax.dot_general(a.astype(BF16), b.astype(BF16), (((1,), (1,)), ((), ())),
                           preferred_element_type=F32)


def _bdot_tn(a, b):
    return lax.dot_general(a.astype(BF16), b.astype(BF16), (((0,), (0,)), ((), ())),
                           preferred_element_type=F32)


def _split3(x):
    hi = x.astype(BF16)
    r1 = x - hi.astype(F32)
    mid = r1.astype(BF16)
    lo = (r1 - mid.astype(F32)).astype(BF16)
    return hi, mid, lo


def _head_sums(x, bd):
    half = bd.shape[0]
    d = functools.partial(jnp.dot, preferred_element_type=F32)
    outs = []
    for j in range(x.shape[1] // half):
        xs = x[:, j * half:(j + 1) * half]
        hi = xs.astype(BF16)
        lo = (xs - hi.astype(F32)).astype(BF16)
        outs.append(d(hi, bd) + d(lo, bd))
    return jnp.concatenate(outs, axis=1)


def _dot3_lhs_exact(e, x):
    hi, mid, lo = _split3(x)
    d = functools.partial(jnp.dot, preferred_element_type=F32)
    return d(e, hi) + d(e, mid) + d(e, lo)


def _sigmoid(x):
    return 1.0 / (1.0 + jnp.exp(-x))


def _softplus(x):
    return jnp.maximum(x, 0.0) + jnp.log(1.0 + jnp.exp(-jnp.abs(x)))


def _norm_mod(x, g, shift, scale):
    r = lax.rsqrt(jnp.mean(x * x, axis=-1, keepdims=True) + NORM_EPS)
    return ((x * r) * g) * (1.0 + scale) + shift


def _group_of_tile(i, tm):
    n_ctx_tiles = N_CTX_ROWS // tm
    tiles_per_lat = DEC_SEQ // tm
    return jnp.where(i < n_ctx_tiles, 0, 1 + (i - n_ctx_tiles) // tiles_per_lat)


def _mod_spec(tm):
    return pl.BlockSpec((1, 1, D_MODEL), lambda i, *_: (_group_of_tile(i, tm), 0, 0))


def _full_spec(shape):
    n = len(shape)
    return pl.BlockSpec(shape, lambda *_: (0,) * n)


def _ada_kernel(cv_ref, w_ref, b_ref, o_ref):
    cv = cv_ref[...]
    s = cv * _sigmoid(cv)
    o_ref[0] = _bdot(s, w_ref[0]) + b_ref[0]


def _ada_all(cv8, w_ada, b_ada):
    tn = 512
    return pl.pallas_call(
        _ada_kernel,
        grid=(DEPTH, 6 * D_MODEL // tn),
        in_specs=[
            pl.BlockSpec((8, D_MODEL), lambda l, j: (0, 0)),
            pl.BlockSpec((1, D_MODEL, tn), lambda l, j: (l, 0, j)),
            pl.BlockSpec((1, 1, tn), lambda l, j: (l, 0, j)),
        ],
        out_specs=pl.BlockSpec((1, 8, tn), lambda l, j: (l, 0, j)),
        out_shape=jax.ShapeDtypeStruct((DEPTH, 8, 6 * D_MODEL), F32),
        compiler_params=_cparams(("parallel", "parallel")),
        name="adaln",
    )(cv8, w_ada, b_ada.reshape(DEPTH, 1, 6 * D_MODEL))


def _proj_kernel(*refs, splits, cache_cols, n_prev, slot, tm):
    x_ref, g_ref, sh_ref, sc_ref, w_ref = refs[:5]
    o_refs = refs[5 + n_prev:5 + n_prev + len(splits)]
    c_refs = refs[5 + n_prev + len(splits):]
    i = pl.program_id(0)
    h = _norm_mod(x_ref[...], g_ref[...], sh_ref[0], sc_ref[0]).astype(BF16)
    off = 0
    ys = []
    for o_ref, n in zip(o_refs, splits):
        y = jnp.dot(h, w_ref[:, off:off + n], preferred_element_type=F32)
        o_ref[...] = y
        ys.append(y)
        off += n

    @pl.when(i < N_CTX_ROWS // tm)
    def _():
        for c_ref, col in zip(c_refs, cache_cols):
            for s in range(tm // SEQ):
                c_ref[s, 0] = ys[col][s * SEQ:(s + 1) * SEQ]
                if slot == 0:
                    c_ref[s, 1] = jnp.zeros((SEQ, splits[col]), F32)


def _norm_proj(x, g, shift, scale, w_bf16, splits, name, cache_cols, slot, prev_caches):
    tm = 512
    n = w_bf16.shape[1]
    n_ctx_tiles = N_CTX_ROWS // tm
    seq_per_tile = tm // SEQ
    n_prev = len(prev_caches)
    n_slots = 2
    if slot == 0:
        cache_specs = [pl.BlockSpec((seq_per_tile, n_slots, SEQ, splits[c]),
                                    lambda i: (jnp.minimum(i, n_ctx_tiles - 1), 0, 0, 0)) for c in cache_cols]
    else:
        cache_specs = [pl.BlockSpec((seq_per_tile, 1, SEQ, splits[c]),
                                    lambda i: (jnp.minimum(i, n_ctx_tiles - 1), slot, 0, 0)) for c in cache_cols]
    outs = pl.pallas_call(
        functools.partial(_proj_kernel, splits=splits, cache_cols=tuple(cache_cols), n_prev=n_prev,
                          slot=slot, tm=tm),
        grid=(M_ALL // tm,),
        in_specs=[
            pl.BlockSpec((tm, D_MODEL), lambda i: (i, 0)),
            _full_spec((1, D_MODEL)),
            _mod_spec(tm),
            _mod_spec(tm),
            _full_spec((D_MODEL, n)),
        ] + [pl.BlockSpec(memory_space=pl.ANY)] * n_prev,
        out_specs=[pl.BlockSpec((tm, s), lambda i: (i, 0)) for s in splits] + cache_specs,
        out_shape=[jax.ShapeDtypeStruct((M_ALL, s), F32) for s in splits]
        + [jax.ShapeDtypeStruct((BATCH, n_slots, SEQ, splits[c]), F32) for c in cache_cols],
        input_output_aliases={5 + j: len(splits) + j for j in range(n_prev)},
        compiler_params=_cparams(("arbitrary",)),
        name=name,
    )(x, g.reshape(1, D_MODEL), shift, scale, w_bf16, *prev_caches)
    return outs[:len(splits)], outs[len(splits):]


def _part_specs(tm, ncols):
    n_ctx_tiles = N_CTX_ROWS // tm
    return [pl.BlockSpec((tm, ncols), lambda i, *_: (jnp.minimum(i, n_ctx_tiles - 1), 0)),
            pl.BlockSpec((tm, ncols), lambda i, *_: (jnp.maximum(i - n_ctx_tiles, 0), 0))]


def _pick_part(i, tm, ctx_ref, lat_ref):
    return jnp.where(i < N_CTX_ROWS // tm, ctx_ref[...], lat_ref[...])


def _out_kernel(x_ref, gate_ref, *refs, split, tm):
    i = pl.program_id(0)
    n_act = sum(2 if s else 1 for s in split)
    a_refs, w_refs, o_ref = refs[:n_act], refs[n_act:n_act + len(split)], refs[n_act + len(split)]
    acc = None
    pos = 0
    for is_split, w_ref in zip(split, w_refs):
        if is_split:
            a = _pick_part(i, tm, a_refs[pos], a_refs[pos + 1])
            pos += 2
        else:
            a = a_refs[pos][...]
            pos += 1
        t = jnp.dot(a, w_ref[...], preferred_element_type=F32)
        acc = t if acc is None else acc + t
    o_ref[...] = x_ref[...] + gate_ref[0] * acc


def _out_proj(x, gate, acts, ws, name):
    tm = 512
    in_specs = [pl.BlockSpec((tm, D_MODEL), lambda i: (i, 0)), _mod_spec(tm)]
    flat, split = [], []
    for a in acts:
        if isinstance(a, tuple):
            in_specs += _part_specs(tm, a[0].shape[1])
            flat += list(a)
            split.append(True)
        else:
            in_specs.append(pl.BlockSpec((tm, a.shape[1]), lambda i: (i, 0)))
            flat.append(a)
            split.append(False)
    return pl.pallas_call(
        functools.partial(_out_kernel, split=tuple(split), tm=tm),
        grid=(M_ALL // tm,),
        in_specs=in_specs + [_full_spec(w.shape) for w in ws],
        out_specs=pl.BlockSpec((tm, D_MODEL), lambda i: (i, 0)),
        out_shape=jax.ShapeDtypeStruct((M_ALL, D_MODEL), F32),
        compiler_params=_cparams(("parallel",)),
        name=name,
    )(x, gate, *flat, *ws)


def _ffn_kernel(x_ref, g_ref, sh_ref, sc_ref, gate_ref, w1_ref, w3_ref, w2_ref, o_ref):
    x = x_ref[...]
    h = _norm_mod(x, g_ref[...], sh_ref[0], sc_ref[0]).astype(BF16)
    a = jnp.dot(h, w1_ref[...], preferred_element_type=F32)
    b = jnp.dot(h, w3_ref[...], preferred_element_type=F32)
    gated = ((a * _sigmoid(a)) * b).astype(BF16)
    o_ref[...] = x + gate_ref[0] * jnp.dot(gated, w2_ref[...], preferred_element_type=F32)


def _resident_spec(shape):
    n = len(shape)
    return pl.BlockSpec(shape, lambda *_: (0,) * n, pipeline_mode=pl.Buffered(1))


def _ffn(x, g, shift, scale, gate, w1, w3, w2, name):
    tm = 512
    return pl.pallas_call(
        _ffn_kernel,
        grid=(M_ALL // tm,),
        in_specs=[
            pl.BlockSpec((tm, D_MODEL), lambda i: (i, 0)),
            _full_spec((1, D_MODEL)),
            _mod_spec(tm),
            _mod_spec(tm),
            _mod_spec(tm),
            _resident_spec((D_MODEL, D_FF)),
            _resident_spec((D_MODEL, D_FF)),
            _resident_spec((D_FF, D_MODEL)),
        ],
        out_specs=pl.BlockSpec((tm, D_MODEL), lambda i: (i, 0)),
        out_shape=jax.ShapeDtypeStruct((M_ALL, D_MODEL), F32),
        compiler_params=_cparams(("parallel",)),
        name=name,
    )(x, g.reshape(1, D_MODEL), shift, scale, gate, w1, w3, w2)


def _final_norm_kernel(x_ref, g_ref, oc_ref, ol_ref, *, tm):
    i = pl.program_id(0)
    x = x_ref[...]
    r = lax.rsqrt(jnp.mean(x * x, axis=-1, keepdims=True) + NORM_EPS)
    y = (x * r) * g_ref[...]

    @pl.when(i < N_CTX_ROWS // tm)
    def _():
        oc_ref[...] = y

    @pl.when(i >= N_CTX_ROWS // tm)
    def _():
        ol_ref[...] = y


def _final_norm(x, g):
    tm = 1024
    return pl.pallas_call(
        functools.partial(_final_norm_kernel, tm=tm),
        grid=(M_ALL // tm,),
        in_specs=[pl.BlockSpec((tm, D_MODEL), lambda i: (i, 0)), _full_spec((1, D_MODEL))],
        out_specs=_part_specs(tm, D_MODEL),
        out_shape=[jax.ShapeDtypeStruct((N_CTX_ROWS, D_MODEL), F32),
                   jax.ShapeDtypeStruct((N_LAT_ROWS, D_MODEL), F32)],
        compiler_params=_cparams(("arbitrary",)),
        name="final_norm",
    )(x, g.reshape(1, D_MODEL))


def _half_masks():
    lane = lax.broadcasted_iota(jnp.int32, (1, LANES), 1)
    return (lane < HEAD_DIM, lane >= HEAD_DIM)


def _na_ctx_kernel(q_ref, k_ref, v_ref, o_ref):
    masks = _half_masks()
    work = []
    for p in range(D_NA // LANES):
        sl = slice(p * LANES, (p + 1) * LANES)
        q = q_ref[:, sl] * (QK_SCALE * LOG2_E)
        k = k_ref[:, sl].astype(BF16)
        v = v_ref[:, sl]
        for m in masks:
            work.append(dict(s=_bdot_nt(jnp.where(m, q, 0.0), k), vm=jnp.where(m, v, 0.0).astype(BF16)))
    for w in work:
        e = jnp.exp2(w["s"] - jnp.max(w["s"], axis=-1, keepdims=True))
        w["inv_l"] = 1.0 / jnp.sum(e, axis=-1, keepdims=True)
        w["e"] = e.astype(BF16)
    for w in work:
        w["o"] = jnp.dot(w["e"], w["vm"], preferred_element_type=F32) * w["inv_l"]
    for p in range(D_NA // LANES):
        o_ref[:, p * LANES:(p + 1) * LANES] = (work[2 * p]["o"] + work[2 * p + 1]["o"]).astype(BF16)


def _na_ctx(q, k, v):
    spec = pl.BlockSpec((SEQ, D_NA), lambda b: (b, 0))
    return pl.pallas_call(
        _na_ctx_kernel,
        grid=(BATCH,),
        in_specs=[spec, spec, spec],
        out_specs=spec,
        out_shape=jax.ShapeDtypeStruct((N_CTX_ROWS, D_NA), BF16),
        compiler_params=_cparams(("parallel",)),
        name="na_ctx",
    )(q, k, v)


def _na_lat_kernel(q_ref, k_ref, v_ref, kc_ref, vc_ref, bias_ref, o_ref, *, rows_per_step):
    n_loc = NA_KH * GRID_W
    masks = _half_masks()
    kc = kc_ref[0].astype(BF16)
    vc = vc_ref[0]
    vcm = [jnp.where(m, vc, 0.0).astype(BF16) for m in masks]
    work = []
    for rr in range(rows_per_step):
        i = pl.program_id(2) * rows_per_step + rr
        r0 = jnp.clip(i - NA_KH // 2, 0, GRID_ROWS - NA_KH)
        start = pl.multiple_of(r0 * GRID_W, GRID_W)
        q = q_ref[rr * GRID_W:(rr + 1) * GRID_W, :] * QK_SCALE
        kl = k_ref[pl.ds(start, n_loc), :].astype(BF16)
        vl = v_ref[pl.ds(start, n_loc), :]
        for hh, m in enumerate(masks):
            qm = jnp.where(m, q, 0.0).astype(BF16)
            work.append(dict(rr=rr, hh=hh, vlm=jnp.where(m, vl, 0.0).astype(BF16),
                             sl=_bdot_nt(qm, kl) + bias_ref[r0 - i + NA_KH - 1, hh], sc=_bdot_nt(qm, kc)))
    for w in work:
        mx = jnp.maximum(jnp.max(w["sl"], axis=-1, keepdims=True), jnp.max(w["sc"], axis=-1, keepdims=True))
        w["el"] = jnp.exp(w["sl"] - mx)
        w["ec"] = jnp.exp(w["sc"] - mx)
    for w in work:
        inv_l = 1.0 / (jnp.sum(w["el"], axis=-1, keepdims=True) + jnp.sum(w["ec"], axis=-1, keepdims=True))
        w["o"] = (_bdot(w["el"], w["vlm"]) + _bdot(w["ec"], vcm[w["hh"]])) * inv_l
    for rr in range(rows_per_step):
        a, b = [w["o"] for w in work if w["rr"] == rr]
        o_ref[rr * GRID_W:(rr + 1) * GRID_W, :] = (a + b).astype(BF16)


def _na_bias_table(rpb):
    jj = jnp.arange(GRID_W)[:, None]
    cc = jnp.arange(GRID_W)[None, :]
    c0 = jnp.clip(jj - NA_KW // 2, 0, GRID_W - NA_KW)
    inwin = (cc >= c0) & (cc < c0 + NA_KW)
    idx = jnp.clip(cc - jj + NA_KW - 1, 0, 2 * NA_KW - 2)
    full = rpb[:, :, idx]
    full = jnp.where(inwin[None, None], full, NEG_BIG)
    tabs = []
    for v in range(NA_KH):
        t = full[:, v:v + NA_KH]
        tabs.append(t.transpose(0, 2, 1, 3).reshape(N_HEADS, GRID_W, NA_KH * GRID_W))
    return jnp.stack(tabs, axis=0)


def _na_lat(q, k, v, kc, vc, bias_tab):
    rows_per_step = 4
    tq = rows_per_step * GRID_W
    steps = GRID_ROWS // rows_per_step
    lat_blk = N_CTX_ROWS // DEC_SEQ
    row_blk = N_CTX_ROWS // tq
    return pl.pallas_call(
        functools.partial(_na_lat_kernel, rows_per_step=rows_per_step),
        grid=(DEC_BATCH, D_NA // LANES, steps),
        in_specs=[
            pl.BlockSpec((tq, LANES), lambda b, p, i: (row_blk + b * steps + i, p)),
            pl.BlockSpec((DEC_SEQ, LANES), lambda b, p, i: (lat_blk + b, p)),
            pl.BlockSpec((DEC_SEQ, LANES), lambda b, p, i: (lat_blk + b, p)),
            pl.BlockSpec((1, PAST_LEN, LANES), lambda b, p, i: (b, 0, p)),
            pl.BlockSpec((1, PAST_LEN, LANES), lambda b, p, i: (b, 0, p)),
            pl.BlockSpec((NA_KH, 2, GRID_W, NA_KH * GRID_W), lambda b, p, i: (0, p, 0, 0)),
        ],
        out_specs=pl.BlockSpec((tq, LANES), lambda b, p, i: (b * steps + i, p)),
        out_shape=jax.ShapeDtypeStruct((N_LAT_ROWS, D_NA), BF16),
        compiler_params=_cparams(("parallel", "parallel", "arbitrary")),
        name="na_lat",
    )(q, k, v, kc, vc, bias_tab)


def _lam_value(lq_ref, lk_ref, lam_init):
    s = jnp.sum(lq_ref[...] * lk_ref[...], axis=-1, keepdims=True)
    return jnp.exp(s[0:1]) - jnp.exp(s[1:2]) + lam_init


def _subln(o, sub, lam_init):
    r = lax.rsqrt(jnp.mean(o * o, axis=-1, keepdims=True) + NORM_EPS)
    return ((o * r) * sub) * (1.0 - lam_init)


def _diff_ctx_kernel(lq_ref, lk_ref, sub_ref, q_ref, k_ref, v_ref, o_ref, *, lam_init):
    lam = _lam_value(lq_ref, lk_ref, lam_init)
    masks = _half_masks()
    work = []
    for h in range(N_HEADS):
        sl = slice(h * LANES, (h + 1) * LANES)
        q = q_ref[:, sl] * (QK_SCALE * LOG2_E)
        k = k_ref[:, sl].astype(BF16)
        for m in masks:
            work.append(dict(s=_bdot_nt(jnp.where(m, q, 0.0), k)))
    for w in work:
        e = jnp.exp2(w["s"] - jnp.max(w["s"], axis=-1, keepdims=True))
        w["p"] = e * (1.0 / jnp.sum(e, axis=-1, keepdims=True))
    outs = []
    for h in range(N_HEADS):
        att = work[2 * h]["p"] - lam * work[2 * h + 1]["p"]
        outs.append(_bdot(att, v_ref[:, h * LANES:(h + 1) * LANES]))
    for h in range(N_HEADS):
        o_ref[:, h * LANES:(h + 1) * LANES] = _subln(outs[h], sub_ref[...], lam_init).astype(BF16)


def _diff_ctx(q, k, v, lam_q, lam_k, subln, lam_init):
    spec = pl.BlockSpec((SEQ, D_DIFF), lambda b: (b, 0))
    return pl.pallas_call(
        functools.partial(_diff_ctx_kernel, lam_init=lam_init),
        grid=(BATCH,),
        in_specs=[_full_spec((2, HEAD_DIM)), _full_spec((2, HEAD_DIM)), _full_spec((1, LANES)),
                  spec, spec, spec],
        out_specs=spec,
        out_shape=jax.ShapeDtypeStruct((N_CTX_ROWS, D_DIFF), BF16),
        compiler_params=_cparams(("parallel",)),
        name="diff_ctx",
    )(lam_q, lam_k, subln.reshape(1, LANES), q, k, v)


def _proj_diff_kernel(*refs, n_prev, slot, tm):
    x_ref, g_ref, sh_ref, sc_ref, w_ref, cos_ref, sin_ref = refs[:7]
    q_ref, k_ref, v_ref, kc_ref, vc_ref, qr_ref, kr_ref, vt_ref = refs[7 + n_prev:]
    i = pl.program_id(0)
    n_ctx_tiles = N_CTX_ROWS // tm
    h = _norm_mod(x_ref[...], g_ref[...], sh_ref[0], sc_ref[0]).astype(BF16)
    ys = [jnp.dot(h, w_ref[:, j * D_DIFF:(j + 1) * D_DIFF], preferred_element_type=F32) for j in range(3)]

    @pl.when(i < n_ctx_tiles)
    def _():
        for o_ref, y in zip((q_ref, k_ref, v_ref), ys):
            o_ref[...] = y
        for c_ref, y in ((kc_ref, ys[1]), (vc_ref, ys[2])):
            for s in range(tm // SEQ):
                c_ref[s, 0] = y[s * SEQ:(s + 1) * SEQ]
                if slot == 0:
                    c_ref[s, 1] = jnp.zeros((SEQ, D_DIFF), F32)

    @pl.when(i >= n_ctx_tiles)
    def _():
        cos = cos_ref[...]
        sin = sin_ref[...]
        lane = lax.broadcasted_iota(jnp.int32, (1, LANES), 1)
        first = (lane % (2 * ROPE_F)) < ROPE_F
        for y, o_ref, scale in ((ys[0], qr_ref, QK_SCALE * LOG2_E), (ys[1], kr_ref, None)):
            for j in range(D_DIFF // LANES):
                x = y[:, j * LANES:(j + 1) * LANES]
                partner = jnp.where(first, pltpu.roll(x, LANES - ROPE_F, 1), pltpu.roll(x, ROPE_F, 1))
                r = x * cos + partner * sin
                if scale is not None:
                    r = r * scale
                o_ref[:, j * LANES:(j + 1) * LANES] = r.astype(BF16)
        vt_ref[0] = ys[2].T.astype(BF16)


def _rope_tables():
    t = jnp.arange(DEC_SEQ)
    pos = jnp.stack([t // GRID_W, t % GRID_W], -1).astype(F32)
    inv = ROPE_BASE ** (-jnp.arange(ROPE_F, dtype=F32) / ROPE_F)
    ang = pos[:, :, None] * inv
    cos, sin = jnp.cos(ang), jnp.sin(ang)
    cos64 = jnp.concatenate([cos, cos], axis=-1).reshape(DEC_SEQ, HEAD_DIM)
    sin64 = jnp.concatenate([-sin, sin], axis=-1).reshape(DEC_SEQ, HEAD_DIM)
    return jnp.tile(cos64, (1, 2)), jnp.tile(sin64, (1, 2))


def _proj_diff(x, g, shift, scale, w_bf16, cos_t, sin_t, slot, prev_caches):
    tm = 512
    n_ctx_tiles = N_CTX_ROWS // tm
    per_seq = DEC_SEQ // tm
    seq_per_tile = tm // SEQ
    n_prev = len(prev_caches)

    def lat(i):
        return jnp.maximum(i - n_ctx_tiles, 0)

    ctx_spec, lat_spec = _part_specs(tm, D_DIFF)
    tab_spec = pl.BlockSpec((tm, LANES), lambda i: (lat(i) % per_seq, 0))
    if slot == 0:
        cache_spec = pl.BlockSpec((seq_per_tile, 2, SEQ, D_DIFF),
                                  lambda i: (jnp.minimum(i, n_ctx_tiles - 1), 0, 0, 0))
    else:
        cache_spec = pl.BlockSpec((seq_per_tile, 1, SEQ, D_DIFF),
                                  lambda i: (jnp.minimum(i, n_ctx_tiles - 1), slot, 0, 0))
    ctx_shape = jax.ShapeDtypeStruct((N_CTX_ROWS, D_DIFF), F32)
    cache_shape = jax.ShapeDtypeStruct((BATCH, 2, SEQ, D_DIFF), F32)
    lat_shape = jax.ShapeDtypeStruct((N_LAT_ROWS, D_DIFF), BF16)
    outs = pl.pallas_call(
        functools.partial(_proj_diff_kernel, n_prev=n_prev, slot=slot, tm=tm),
        grid=(M_ALL // tm,),
        in_specs=[
            pl.BlockSpec((tm, D_MODEL), lambda i: (i, 0)),
            _full_spec((1, D_MODEL)),
            _mod_spec(tm),
            _mod_spec(tm),
            _resident_spec((D_MODEL, 3 * D_DIFF)),
            tab_spec,
            tab_spec,
        ] + [pl.BlockSpec(memory_space=pl.ANY)] * n_prev,
        out_specs=[ctx_spec, ctx_spec, ctx_spec, cache_spec, cache_spec, lat_spec, lat_spec,
                   pl.BlockSpec((1, D_DIFF, tm), lambda i: (lat(i) // per_seq, 0, lat(i) % per_seq))],
        out_shape=[ctx_shape, ctx_shape, ctx_shape, cache_shape, cache_shape, lat_shape, lat_shape,
                   jax.ShapeDtypeStruct((DEC_BATCH, D_DIFF, DEC_SEQ), BF16)],
        input_output_aliases={7 + j: 3 + j for j in range(n_prev)},
        compiler_params=_cparams(("arbitrary",)),
        name="proj_odd",
    )(x, g.reshape(1, D_MODEL), shift, scale, w_bf16, cos_t, sin_t, *prev_caches)
    return outs[:3], outs[3:5], outs[5:]


def _diff_lat_kernel(lq_ref, lk_ref, sub_ref, q_ref, kl_ref, vlt_ref, kc_ref, vct_ref, o_ref, *,
                     lam_init, sub_q, n_sub):
    lam = _lam_value(lq_ref, lk_ref, lam_init)
    kl = kl_ref[...]
    kc = kc_ref[0].astype(BF16)
    vlt = vlt_ref[0]
    vct = vct_ref[0].astype(BF16)
    work = []
    for t in range(n_sub):
        q = q_ref[t * sub_q:(t + 1) * sub_q, :]
        for m in _half_masks():
            qm = jnp.where(m, q, jnp.zeros_like(q))
            work.append(dict(sl=_bdot_nt(kl, qm), sc=_bdot_nt(kc, qm)))
    for w in work:
        mx = jnp.maximum(jnp.max(w["sl"], axis=0, keepdims=True), jnp.max(w["sc"], axis=0, keepdims=True))
        el = jnp.exp2(w["sl"] - mx)
        ec = jnp.exp2(w["sc"] - mx)
        w["inv_l"] = 1.0 / (jnp.sum(el, axis=0, keepdims=True) + jnp.sum(ec, axis=0, keepdims=True))
        w["el"] = el.astype(BF16)
        w["ec"] = ec.astype(BF16)
    for w in work:
        w["o"] = (jnp.dot(vlt, w["el"], preferred_element_type=F32)
                  + jnp.dot(vct, w["ec"], preferred_element_type=F32)) * w["inv_l"]
    for t in range(n_sub):
        ot = work[2 * t]["o"] - lam * work[2 * t + 1]["o"]
        r = lax.rsqrt(jnp.mean(ot * ot, axis=0, keepdims=True) + NORM_EPS)
        on = ((ot * r) * sub_ref[...]) * (1.0 - lam_init)
        o_ref[t * sub_q:(t + 1) * sub_q, :] = on.T.astype(BF16)


def _diff_lat(q_r, k_r, v_t, kc, vc_t, lam_q, lam_k, subln, lam_init):
    sub_q, n_sub = 256, 2
    tq = sub_q * n_sub
    per_seq = DEC_SEQ // tq
    return pl.pallas_call(
        functools.partial(_diff_lat_kernel, lam_init=lam_init, sub_q=sub_q, n_sub=n_sub),
        grid=(DEC_BATCH, N_HEADS, per_seq),
        in_specs=[
            _full_spec((2, HEAD_DIM)),
            _full_spec((2, HEAD_DIM)),
            _full_spec((LANES, 1)),
            pl.BlockSpec((tq, LANES), lambda b, h, i: (b * per_seq + i, h)),
            pl.BlockSpec((DEC_SEQ, LANES), lambda b, h, i: (b, h)),
            pl.BlockSpec((1, LANES, DEC_SEQ), lambda b, h, i: (b, h, 0)),
            pl.BlockSpec((1, PAST_LEN, LANES), lambda b, h, i: (b, 0, h)),
            pl.BlockSpec((1, LANES, PAST_LEN), lambda b, h, i: (b, h, 0)),
        ],
        out_specs=pl.BlockSpec((tq, LANES), lambda b, h, i: (b * per_seq + i, h)),
        out_shape=jax.ShapeDtypeStruct((N_LAT_ROWS, D_DIFF), BF16),
        compiler_params=_cparams(("parallel", "parallel", "arbitrary")),
        name="diff_lat",
    )(lam_q, lam_k, subln.reshape(LANES, 1), q_r, k_r, v_t, kc, vc_t)


def _token_shift(u, prev_row, next_row, mu):
    n = u.shape[0]
    rows = lax.broadcasted_iota(jnp.int32, (n, 1), 0)
    up = jnp.where(rows == 0, prev_row, pltpu.roll(u, 1, 0))
    un = jnp.where(rows == n - 1, next_row, pltpu.roll(u, n - 1, 0))
    return u + mu[0:1] * (up - u) + mu[1:2] * (un - u)


def _icl_rate_and_key(us, a0, a2p, k_a):
    lo = us[:, 3 * D_RWKV:3 * D_RWKV + LANES]
    k = us[:, D_RWKV:2 * D_RWKV]
    a = _sigmoid(a0 + _bdot(lo, a2p))
    return a, k * (1.0 + (a - 1.0) * k_a)


def _stack_blockdiag(x):
    shape = (2 * x.shape[0], x.shape[1])
    row = lax.broadcasted_iota(jnp.int32, shape, 0)
    col = lax.broadcasted_iota(jnp.int32, shape, 1)
    keep = (row // HEAD_DIM) == ((col % LANES) // HEAD_DIM)
    return jnp.where(keep, jnp.concatenate([x, x], axis=0), 0.0).astype(BF16)


def _rwkv_chunk_inputs(u, prev_row, next_row, reverse, mu, w0, w2p, a0, a2p, k_k, k_a, bd_ones):
    us = _token_shift(u, prev_row, next_row, mu)
    r = us[:, 0:D_RWKV]
    k = us[:, D_RWKV:2 * D_RWKV]
    v = us[:, 2 * D_RWKV:3 * D_RWKV]
    lo = us[:, 3 * D_RWKV:3 * D_RWKV + LANES]
    w_raw = w0 + _bdot(jnp.tanh(lo), w2p)
    logw = -jnp.exp(-_softplus(-w_raw) - 0.5)
    a, kt = _icl_rate_and_key(us, a0, a2p, k_a)
    kk_raw = k * k_k
    ss = _head_sums(kk_raw * kk_raw, bd_ones)
    kk = kk_raw * lax.rsqrt(jnp.maximum(ss, 1e-12))
    row = lax.broadcasted_iota(jnp.int32, (CHUNK, CHUNK), 0)
    col = lax.broadcasted_iota(jnp.int32, (CHUNK, CHUNK), 1)
    seen = (col >= row) if reverse else (col <= row)
    cum = _dot3_lhs_exact(seen.astype(F32).astype(BF16), logw)
    p_inv = jnp.exp(-cum)
    tot = cum[0:1, :] if reverse else cum[CHUNK - 1:CHUNK, :]
    return dict(a_hat=jnp.exp(cum - logw) * kk, b_hat=(kk * a) * p_inv, k_hat=kt * p_inv,
                r_hat=r * jnp.exp(cum), v=v, p_tot=jnp.exp(tot))


def _rwkv_scan_kernel(*refs, n_chunks, n_par, has_state):
    u_refs = (refs[0:3], refs[3:6])
    pos = 6
    s0_ref = None
    if has_state:
        s0_ref = refs[pos]
        pos += 1
    mu_ref, w0_ref, w2_ref, a0_ref, a2_ref, kk_ref, ka_ref, bd_ref = refs[pos:pos + 8]
    y_refs = refs[pos + 8:pos + 10]
    sfin_ref, s_ref = refs[pos + 10], refs[pos + 11]
    c = pl.program_id(1)
    n_pairs = D_RWKV // LANES

    @pl.when(c == 0)
    def _():
        if has_state:
            s_ref[...] = s0_ref[...]
        else:
            s_ref[...] = jnp.zeros(s_ref.shape, F32)

    at_start = jnp.where(c == 0, 0.0, 1.0)
    at_end = jnp.where(c == n_chunks - 1, 0.0, 1.0)
    has_prev = (at_start, at_end)
    has_next = (at_end, at_start)

    row = lax.broadcasted_iota(jnp.int32, (CHUNK, LANES), 0)
    colm = lax.broadcasted_iota(jnp.int32, (CHUNK, LANES), 1) % CHUNK
    strict = (colm < row, colm > row)
    incl = (colm <= row, colm >= row)
    same_blk = (row // INV_BLOCK) == (colm // INV_BLOCK)
    eye = (row == colm).astype(F32)
    prow = lax.broadcasted_iota(jnp.int32, (LANES, LANES), 0)
    pcol = lax.broadcasted_iota(jnp.int32, (LANES, LANES), 1)
    blockdiag = (prow // HEAD_DIM) == (pcol // HEAD_DIM)
    eye_pair = (prow == pcol).astype(F32)

    chains = []
    for j in range(n_par):
        for d in range(2):
            u_ref, up_ref, un_ref = u_refs[d]
            q = _rwkv_chunk_inputs(
                u_ref[j], up_ref[j, 7:8, :] * has_prev[d], un_ref[j, 0:1, :] * has_next[d], d == 1,
                mu_ref[...], w0_ref[d], w2_ref[d], a0_ref[d], a2_ref[d], kk_ref[...], ka_ref[...], bd_ref[...])
            for p in range(n_pairs):
                sl = slice(p * LANES, (p + 1) * LANES)
                chains.append(dict(j=j, d=d, p=p, sl=sl, ah=q["a_hat"][:, sl], bh=q["b_hat"][:, sl],
                                   kh=q["k_hat"][:, sl], rh=q["r_hat"][:, sl], vv=q["v"][:, sl],
                                   pt=q["p_tot"][:, sl]))

    def stage(fn):
        for ch in chains:
            fn(ch)

    def s_scores(ch):
        ar = jnp.concatenate([ch["ah"], ch["rh"]], axis=0)
        x = _bdot_nt(ar, jnp.concatenate([_stack_blockdiag(ch["bh"]), _stack_blockdiag(ch["kh"])], axis=0))
        xb, xk = x[:, :LANES], x[:, LANES:]
        d = ch["d"]
        l_ab = jnp.where(strict[d], xb[:CHUNK], 0.0)
        ch["l_ak"] = jnp.where(strict[d], xk[:CHUNK], 0.0)
        ch["m_rb"] = jnp.where(incl[d], xb[CHUNK:], 0.0)
        ch["m_rk"] = jnp.where(incl[d], xk[CHUNK:], 0.0)
        ch["x1"] = -jnp.where(same_blk, l_ab, 0.0)
        ch["l_off"] = jnp.where(same_blk, 0.0, l_ab)
        ch["vs"] = _stack_blockdiag(ch["vv"])
    stage(s_scores)

    def s_x2(ch):
        ch["x2"] = _bdot(ch["x1"], _stack_blockdiag(ch["x1"]))
        ch["x2s"] = _stack_blockdiag(ch["x2"])
        ch["u"] = _bdot(ch["l_ak"], ch["vs"])
    stage(s_x2)

    def s_x4(ch):
        td = eye + ch["x1"]
        both = _bdot(jnp.concatenate([ch["x2"], td], axis=0), ch["x2s"])
        ch["x4"] = both[:CHUNK]
        ch["td"] = td + both[CHUNK:]
    stage(s_x4)

    def s_x8(ch):
        both = _bdot(jnp.concatenate([ch["x4"], ch["td"]], axis=0), _stack_blockdiag(ch["x4"]))
        ch["x8s"] = _stack_blockdiag(both[:CHUNK])
        ch["td"] = ch["td"] + both[CHUNK:]
    stage(s_x8)

    def s_td(ch):
        ch["td"] = ch["td"] + _bdot(ch["td"], ch["x8s"])
    stage(s_td)

    def s_mm(ch):
        ch["mm"] = _bdot(ch["td"], _stack_blockdiag(ch["l_off"]))
    stage(s_mm)

    def s_m2(ch):
        ch["m2s"] = _stack_blockdiag(_bdot(ch["mm"], _stack_blockdiag(ch["mm"])))
    stage(s_m2)

    def s_n2(ch):
        n1 = eye - ch["mm"]
        ch["n2"] = n1 + _bdot(n1, ch["m2s"])
    stage(s_n2)

    def s_tinv(ch):
        ch["t_inv"] = _bdot(ch["n2"], _stack_blockdiag(ch["td"]))
    stage(s_tinv)

    def s_tx(ch):
        ch["tx"] = _bdot(ch["t_inv"], _stack_blockdiag(jnp.concatenate([ch["ah"], ch["u"]], axis=1)))
    stage(s_tx)

    def s_out(ch):
        tx = ch["tx"]
        mx = _bdot(ch["m_rb"], _stack_blockdiag(tx))
        q_eff = ch["rh"] - mx[:, :LANES]
        y_loc = _bdot(ch["m_rk"], ch["vs"]) - mx[:, LANES:]
        g_mat = jnp.where(blockdiag, eye_pair - _bdot_tn(tx[:, :LANES], ch["bh"]), 0.0) * ch["pt"]
        h_mat = jnp.where(
            blockdiag,
            _bdot_tn(jnp.concatenate([ch["vv"], -tx[:, LANES:]], axis=0),
                     jnp.concatenate([ch["kh"], ch["bh"]], axis=0)),
            0.0) * ch["pt"]
        j, d, p = ch["j"], ch["d"], ch["p"]
        s_old = s_ref[j, d, p]
        y_refs[d][j, :, ch["sl"]] = _bdot_nt(q_eff, s_old) + y_loc
        s_ref[j, d, p] = _bdot(s_old, g_mat) + h_mat
    stage(s_out)

    @pl.when(c == n_chunks - 1)
    def _():
        sfin_ref[...] = s_ref[...]


def _rwkv_scan(u, s0p, prm, row_base, n_seq, seq_len):
    n_par = 2
    n_chunks = seq_len // CHUNK
    n_pairs = D_RWKV // LANES
    base_blk = row_base // (seq_len * n_par)
    n_blk8 = seq_len // 8
    has_state = s0p is not None
    u3 = u.reshape(M_ALL // seq_len, seq_len, P_RWKV)

    def chunk_pos(d, c):
        return c if d == 0 else n_chunks - 1 - c

    in_specs, args = [], []
    for d in range(2):
        in_specs += [
            pl.BlockSpec((n_par, CHUNK, P_RWKV), lambda g, c, d=d: (base_blk + g, chunk_pos(d, c), 0)),
            pl.BlockSpec((n_par, 8, P_RWKV),
                         lambda g, c, d=d: (base_blk + g, jnp.maximum(chunk_pos(d, c) * (CHUNK // 8) - 1, 0), 0)),
            pl.BlockSpec((n_par, 8, P_RWKV),
                         lambda g, c, d=d: (base_blk + g,
                                            jnp.minimum((chunk_pos(d, c) + 1) * (CHUNK // 8), n_blk8 - 1), 0)),
        ]
        args += [u3, u3, u3]
    state_spec = pl.BlockSpec((n_par, 2, n_pairs, LANES, LANES), lambda g, c: (g, 0, 0, 0, 0))
    if has_state:
        in_specs.append(state_spec)
        args.append(s0p)
    in_specs += [
        _full_spec((2, P_RWKV)),
        _full_spec((2, 1, D_RWKV)),
        _full_spec((2, LANES, D_RWKV)),
        _full_spec((2, 1, D_RWKV)),
        _full_spec((2, LANES, D_RWKV)),
        _full_spec((1, D_RWKV)),
        _full_spec((1, D_RWKV)),
        _full_spec((MXU_WIDTH, MXU_WIDTH)),
    ]
    args += [prm["mu"], prm["w0"], prm["w2p"], prm["a0"], prm["a2p"], prm["kk"], prm["ka"], prm["bd_ones"]]
    y_shape = jax.ShapeDtypeStruct((n_seq, seq_len, D_RWKV), F32)
    y_f, y_b, s_fin = pl.pallas_call(
        functools.partial(_rwkv_scan_kernel, n_chunks=n_chunks, n_par=n_par, has_state=has_state),
        grid=(n_seq // n_par, n_chunks),
        in_specs=in_specs,
        out_specs=[
            pl.BlockSpec((n_par, CHUNK, D_RWKV), lambda g, c: (g, chunk_pos(0, c), 0)),
            pl.BlockSpec((n_par, CHUNK, D_RWKV), lambda g, c: (g, chunk_pos(1, c), 0)),
            state_spec,
        ],
        out_shape=[y_shape, y_shape, jax.ShapeDtypeStruct((n_seq, 2, n_pairs, LANES, LANES), F32)],
        scratch_shapes=[pltpu.VMEM((n_par, 2, n_pairs, LANES, LANES), F32)],
        compiler_params=_cparams(("parallel", "arbitrary")),
        name="rwkv_scan_" + ("lat" if has_state else "ctx"),
    )(*args)
    n_rows = n_seq * seq_len
    return (y_f.reshape(n_rows, D_RWKV), y_b.reshape(n_rows, D_RWKV)), s_fin


def _rwkv_fin_kernel(u_ref, up_ref, un_ref, ycf_ref, ylf_ref, ycb_ref, ylb_ref, mu_ref, a0_ref, a2_ref,
                     ka_ref, bonus_ref, g2_ref, lnw_ref, lnb_ref, bd1_ref, bdm_ref, o_ref, *, tm):
    i = pl.program_id(0)
    n_ctx_tiles = N_CTX_ROWS // tm
    per_seq = DEC_SEQ // tm
    j = (i - n_ctx_tiles) % per_seq
    is_ctx = i < n_ctx_tiles
    not_first = jnp.where(is_ctx | (j == 0), 0.0, 1.0)
    not_last = jnp.where(is_ctx | (j == per_seq - 1), 0.0, 1.0)
    us = _token_shift(u_ref[...], up_ref[7:8, :] * not_first, un_ref[0:1, :] * not_last, mu_ref[...])
    r = us[:, 0:D_RWKV]
    v = us[:, 2 * D_RWKV:3 * D_RWKV]
    g_lo = us[:, 3 * D_RWKV + LANES:]
    rk = jnp.zeros((tm, D_RWKV), F32)
    for d in range(2):
        _, kt = _icl_rate_and_key(us, a0_ref[d], a2_ref[d], ka_ref[...])
        rk = rk + r * kt * bonus_ref[d]
    bon = _head_sums(rk, bd1_ref[...]) * v
    y = _pick_part(i, tm, ycf_ref, ylf_ref) + _pick_part(i, tm, ycb_ref, ylb_ref)
    mean = _head_sums(y, bdm_ref[...])
    yc = y - mean
    var = _head_sums(yc * yc, bdm_ref[...])
    yn = (yc * lax.rsqrt(var + GN_EPS)) * lnw_ref[...] + lnb_ref[...]
    gate = _bdot(_sigmoid(g_lo), g2_ref[...])
    o_ref[...] = ((yn + bon) * gate).astype(BF16)


def _rwkv_finish(u, y_ctx, y_lat, prm):
    tm = 256
    n_blk8 = M_ALL // 8
    return pl.pallas_call(
        functools.partial(_rwkv_fin_kernel, tm=tm),
        grid=(M_ALL // tm,),
        in_specs=[
            pl.BlockSpec((tm, P_RWKV), lambda i: (i, 0)),
            pl.BlockSpec((8, P_RWKV), lambda i: (jnp.maximum(i * (tm // 8) - 1, 0), 0)),
            pl.BlockSpec((8, P_RWKV), lambda i: (jnp.minimum((i + 1) * (tm // 8), n_blk8 - 1), 0)),
            *_part_specs(tm, D_RWKV),
            *_part_specs(tm, D_RWKV),
            _full_spec((2, P_RWKV)),
            _full_spec((2, 1, D_RWKV)),
            _full_spec((2, LANES, D_RWKV)),
            _full_spec((1, D_RWKV)),
            _full_spec((2, 1, D_RWKV)),
            _full_spec((GATE_LORA, D_RWKV)),
            _full_spec((1, D_RWKV)),
            _full_spec((1, D_RWKV)),
            _full_spec((MXU_WIDTH, MXU_WIDTH)),
            _full_spec((MXU_WIDTH, MXU_WIDTH)),
        ],
        out_specs=pl.BlockSpec((tm, D_RWKV), lambda i: (i, 0)),
        out_shape=jax.ShapeDtypeStruct((M_ALL, D_RWKV), BF16),
        compiler_params=_cparams(("parallel",)),
        name="rwkv_finish",
    )(u, u, u, y_ctx[0], y_lat[0], y_ctx[1], y_lat[1], prm["mu"], prm["a0"], prm["a2p"], prm["ka"],
      prm["bonus"], prm["g2"], prm["lnw"], prm["lnb"], prm["bd_ones"], prm["bd_mean"])


def _pack_state_pairs(s):
    lead = s.shape[:-3]
    s = s.reshape(lead + (N_HEADS // 2, 2, HEAD_DIM, HEAD_DIM))
    z = jnp.zeros_like(s[..., 0, :, :])
    top = jnp.concatenate([s[..., 0, :, :], z], axis=-1)
    bot = jnp.concatenate([z, s[..., 1, :, :]], axis=-1)
    return jnp.concatenate([top, bot], axis=-2)


def _unpack_state_pairs(sp):
    lead = sp.shape[:-3]
    a = sp[..., :HEAD_DIM, :HEAD_DIM]
    b = sp[..., HEAD_DIM:, HEAD_DIM:]
    return jnp.stack([a, b], axis=-3).reshape(lead + (N_HEADS, HEAD_DIM, HEAD_DIM))


def kernel(x_prompt, x_sample, c, cache_na_k, cache_na_v, state_rwkv, cache_diff_k, cache_diff_v, c_ctx, w_ada, b_ada, norm_mix, norm_ffn, norm_final, w_in_even, w_out_even, na_rpb, rw_mu, rw_w0, rw_w2, rw_a0, rw_a2, rw_kk, rw_ka, rw_bonus, rw_g2, rw_lnw, rw_lnb, w_qkv_diff, w_out_diff, diff_lam_q, diff_lam_k, diff_subln, ffn_w1, ffn_w3, ffn_w2):
    x = jnp.concatenate([x_prompt.reshape(N_CTX_ROWS, D_MODEL), x_sample.reshape(N_LAT_ROWS, D_MODEL)], axis=0)

    cv8 = jnp.concatenate([c_ctx[None, :], c, jnp.zeros((8 - 1 - DEC_BATCH, D_MODEL), F32)], axis=0)
    mods = _ada_all(cv8, w_ada, b_ada)
    mods = mods.reshape(DEPTH, 8, 6, D_MODEL).transpose(0, 2, 1, 3)[:, :, :1 + DEC_BATCH, None, :]

    hd_idx = jnp.arange(MXU_WIDTH) // HEAD_DIM
    bd_ones = (hd_idx[:, None] == hd_idx[None, :]).astype(BF16)
    bd_mean = (bd_ones.astype(F32) / HEAD_DIM).astype(BF16)
    cos_t, sin_t = _rope_tables()

    rw_out = []
    na_caches, diff_caches = (), ()
    for l in range(DEPTH):
        md = mods[l]
        if l % 2 == 0:
            e = l // 2
            (q, k, v, u), na_caches = _norm_proj(
                x, norm_mix[l], md[0], md[1], w_in_even[e].astype(BF16), (D_NA, D_NA, D_NA, P_RWKV),
                "proj_even", cache_cols=(1, 2), slot=e, prev_caches=na_caches)
            o_ctx = _na_ctx(q, k, v)
            o_lat = _na_lat(q, k, v, cache_na_k[:, e].reshape(DEC_BATCH, PAST_LEN, D_NA),
                            cache_na_v[:, e].reshape(DEC_BATCH, PAST_LEN, D_NA), _na_bias_table(na_rpb[e]))

            zpad = jnp.zeros((2, LANES - DECAY_LORA, D_RWKV), F32)
            prm = {
                "mu": rw_mu[e],
                "w0": rw_w0[e].reshape(2, 1, D_RWKV),
                "w2p": jnp.concatenate([rw_w2[e], zpad], axis=1),
                "a0": rw_a0[e].reshape(2, 1, D_RWKV),
                "a2p": jnp.concatenate([zpad, rw_a2[e]], axis=1),
                "kk": rw_kk[e].reshape(1, D_RWKV),
                "ka": rw_ka[e].reshape(1, D_RWKV),
                "bonus": rw_bonus[e].reshape(2, 1, D_RWKV),
                "g2": rw_g2[e],
                "lnw": rw_lnw[e].reshape(1, D_RWKV),
                "lnb": rw_lnb[e].reshape(1, D_RWKV),
                "bd_ones": bd_ones,
                "bd_mean": bd_mean,
            }
            y_ctx, s_ctx = _rwkv_scan(u, None, prm, 0, BATCH, SEQ)
            y_lat, _ = _rwkv_scan(u, _pack_state_pairs(state_rwkv[:, e]), prm, N_CTX_ROWS, DEC_BATCH, DEC_SEQ)
            o_rw = _rwkv_finish(u, y_ctx, y_lat, prm)

            w_out = w_out_even[e].astype(BF16)
            x = _out_proj(x, md[2], [(o_ctx, o_lat), o_rw], [w_out[:D_NA], w_out[D_NA:]], "out_even")

            rw_out.append(_unpack_state_pairs(s_ctx))
        else:
            o = l // 2
            lam_init = 0.8 - 0.6 * math.exp(-0.3 * l)
            (q, k, v), diff_caches, (q_r, k_r, v_t) = _proj_diff(
                x, norm_mix[l], md[0], md[1], w_qkv_diff[o].astype(BF16), cos_t, sin_t, o, diff_caches)
            o_ctx = _diff_ctx(q, k, v, diff_lam_q[o], diff_lam_k[o], diff_subln[o], lam_init)
            o_lat = _diff_lat(q_r, k_r, v_t, cache_diff_k[:, o].reshape(DEC_BATCH, PAST_LEN, D_DIFF),
                              cache_diff_v[:, o].reshape(DEC_BATCH, PAST_LEN, D_DIFF).swapaxes(1, 2),
                              diff_lam_q[o], diff_lam_k[o], diff_subln[o], lam_init)
            x = _out_proj(x, md[2], [(o_ctx, o_lat)], [w_out_diff[o].astype(BF16)], "out_odd")
        x = _ffn(x, norm_ffn[l], md[3], md[4], md[5], ffn_w1[l].astype(BF16), ffn_w3[l].astype(BF16),
                 ffn_w2[l].astype(BF16), "ffn")

    y_ctx, y_lat = _final_norm(x, norm_final)
    y_prompt = y_ctx.reshape(BATCH, SEQ, D_MODEL)
    y_sample = y_lat.reshape(DEC_BATCH, DEC_SEQ, D_MODEL)
    na_shape = (BATCH, DEPTH // 2, SEQ, N_HEADS, HEAD_DIM)
    diff_shape = (BATCH, DEPTH // 2, SEQ, N_HEADS, 2 * HEAD_DIM)
    return (y_prompt, y_sample, na_caches[0].reshape(na_shape), na_caches[1].reshape(na_shape),
            jnp.stack(rw_out, axis=1), diff_caches[0].reshape(diff_shape), diff_caches[1].reshape(diff_shape))
```

```python
import functools
import math

import jax
import jax.numpy as jnp
from jax import lax
from jax.experimental import pallas as pl
from jax.experimental.pallas import tpu as pltpu

F32 = jnp.float32
BF16 = jnp.bfloat16

D_MODEL = 1024
BATCH = 32
SEQ = 256
DEPTH = 4
DEC_BATCH = 2
DEC_SEQ = 4096
PAST_LEN = 512
GRID_W = 64
GRID_ROWS = DEC_SEQ // GRID_W
HEAD_DIM = 64
N_HEADS = 8
D_NA = 512
D_RWKV = 512
D_DIFF = 1024
NA_KH = 8
NA_KW = 16
DECAY_LORA = 64
ICL_LORA = 64
GATE_LORA = 128
P_RWKV = 3 * D_RWKV + DECAY_LORA + ICL_LORA + GATE_LORA
P_EVEN = 3 * D_NA + P_RWKV
D_FF = 2816
ROPE_F = HEAD_DIM // 4
ROPE_BASE = 10000.0
NORM_EPS = 1e-6
GN_EPS = 64e-5
QK_SCALE = HEAD_DIM ** -0.5
LOG2_E = math.log2(math.e)

N_CTX_ROWS = BATCH * SEQ
N_LAT_ROWS = DEC_BATCH * DEC_SEQ
M_ALL = N_CTX_ROWS + N_LAT_ROWS

LANES = 128
MXU_WIDTH = 256
CHUNK = 64
INV_BLOCK = 16
NEG_BIG = -1e30
VMEM_LIMIT = 56 * 1024 * 1024


def _cparams(sem):
    return pltpu.CompilerParams(dimension_semantics=sem, vmem_limit_bytes=VMEM_LIMIT)


def _bdot(a, b):
    return jnp.dot(a.astype(BF16), b.astype(BF16), preferred_element_type=F32)


def _bdot_nt(a, b):
    return lax.dot_general(a.astype(BF16), b.astype(BF16), (((1,), (1,)), ((), ())),
                           preferred_element_type=F32)


def _bdot_tn(a, b):
    return lax.dot_general(a.astype(BF16), b.astype(BF16), (((0,), (0,)), ((), ())),
                           preferred_element_type=F32)


def _split3(x):
    hi = x.astype(BF16)
    r1 = x - hi.astype(F32)
    mid = r1.astype(BF16)
    lo = (r1 - mid.astype(F32)).astype(BF16)
    return hi, mid, lo


def _head_sums(x, bd):
    half = bd.shape[0]
    d = functools.partial(jnp.dot, preferred_element_type=F32)
    outs = []
    for j in range(x.shape[1] // half):
        xs = x[:, j * half:(j + 1) * half]
        hi = xs.astype(BF16)
        lo = (xs - hi.astype(F32)).astype(BF16)
        outs.append(d(hi, bd) + d(lo, bd))
    return jnp.concatenate(outs, axis=1)


def _dot3_lhs_exact(e, x):
    hi, mid, lo = _split3(x)
    d = functools.partial(jnp.dot, preferred_element_type=F32)
    return d(e, hi) + d(e, mid) + d(e, lo)


def _sigmoid(x):
    return 1.0 / (1.0 + jnp.exp(-x))


def _softplus(x):
    return jnp.maximum(x, 0.0) + jnp.log(1.0 + jnp.exp(-jnp.abs(x)))


def _norm_mod(x, g, shift, scale):
    r = lax.rsqrt(jnp.mean(x * x, axis=-1, keepdims=True) + NORM_EPS)
    return ((x * r) * g) * (1.0 + scale) + shift


def _group_of_tile(i, tm):
    n_ctx_tiles = N_CTX_ROWS // tm
    tiles_per_lat = DEC_SEQ // tm
    return jnp.where(i < n_ctx_tiles, 0, 1 + (i - n_ctx_tiles) // tiles_per_lat)


def _mod_spec(tm):
    return pl.BlockSpec((1, 1, D_MODEL), lambda i, *_: (_group_of_tile(i, tm), 0, 0))


def _full_spec(shape):
    n = len(shape)
    return pl.BlockSpec(shape, lambda *_: (0,) * n)


def _ada_kernel(cv_ref, w_ref, b_ref, o_ref):
    cv = cv_ref[...]
    s = cv * _sigmoid(cv)
    o_ref[0] = _bdot(s, w_ref[0]) + b_ref[0]


def _ada_all(cv8, w_ada, b_ada):
    tn = 512
    return pl.pallas_call(
        _ada_kernel,
        grid=(DEPTH, 6 * D_MODEL // tn),
        in_specs=[
            pl.BlockSpec((8, D_MODEL), lambda l, j: (0, 0)),
            pl.BlockSpec((1, D_MODEL, tn), lambda l, j: (l, 0, j)),
            pl.BlockSpec((1, 1, tn), lambda l, j: (l, 0, j)),
        ],
        out_specs=pl.BlockSpec((1, 8, tn), lambda l, j: (l, 0, j)),
        out_shape=jax.ShapeDtypeStruct((DEPTH, 8, 6 * D_MODEL), F32),
        compiler_params=_cparams(("parallel", "parallel")),
        name="adaln",
    )(cv8, w_ada, b_ada.reshape(DEPTH, 1, 6 * D_MODEL))


def _proj_kernel(*refs, splits, cache_cols, n_prev, slot, tm):
    x_ref, g_ref, sh_ref, sc_ref, w_ref = refs[:5]
    o_refs = refs[5 + n_prev:5 + n_prev + len(splits)]
    c_refs = refs[5 + n_prev + len(splits):]
    i = pl.program_id(0)
    h = _norm_mod(x_ref[...], g_ref[...], sh_ref[0], sc_ref[0]).astype(BF16)
    off = 0
    ys = []
    for o_ref, n in zip(o_refs, splits):
        y = jnp.dot(h, w_ref[:, off:off + n], preferred_element_type=F32)
        o_ref[...] = y
        ys.append(y)
        off += n

    @pl.when(i < N_CTX_ROWS // tm)
    def _():
        for c_ref, col in zip(c_refs, cache_cols):
            for s in range(tm // SEQ):
                c_ref[s, 0] = ys[col][s * SEQ:(s + 1) * SEQ]
                if slot == 0:
                    c_ref[s, 1] = jnp.zeros((SEQ, splits[col]), F32)


def _norm_proj(x, g, shift, scale, w_bf16, splits, name, cache_cols, slot, prev_caches):
    tm = 512
    n = w_bf16.shape[1]
    n_ctx_tiles = N_CTX_ROWS // tm
    seq_per_tile = tm // SEQ
    n_prev = len(prev_caches)
    n_slots = 2
    if slot == 0:
        cache_specs = [pl.BlockSpec((seq_per_tile, n_slots, SEQ, splits[c]),
                                    lambda i: (jnp.minimum(i, n_ctx_tiles - 1), 0, 0, 0)) for c in cache_cols]
    else:
        cache_specs = [pl.BlockSpec((seq_per_tile, 1, SEQ, splits[c]),
                                    lambda i: (jnp.minimum(i, n_ctx_tiles - 1), slot, 0, 0)) for c in cache_cols]
    outs = pl.pallas_call(
        functools.partial(_proj_kernel, splits=splits, cache_cols=tuple(cache_cols), n_prev=n_prev,
                          slot=slot, tm=tm),
        grid=(M_ALL // tm,),
        in_specs=[
            pl.BlockSpec((tm, D_MODEL), lambda i: (i, 0)),
            _full_spec((1, D_MODEL)),
            _mod_spec(tm),
            _mod_spec(tm),
            _full_spec((D_MODEL, n)),
        ] + [pl.BlockSpec(memory_space=pl.ANY)] * n_prev,
        out_specs=[pl.BlockSpec((tm, s), lambda i: (i, 0)) for s in splits] + cache_specs,
        out_shape=[jax.ShapeDtypeStruct((M_ALL, s), F32) for s in splits]
        + [jax.ShapeDtypeStruct((BATCH, n_slots, SEQ, splits[c]), F32) for c in cache_cols],
        input_output_aliases={5 + j: len(splits) + j for j in range(n_prev)},
        compiler_params=_cparams(("arbitrary",)),
        name=name,
    )(x, g.reshape(1, D_MODEL), shift, scale, w_bf16, *prev_caches)
    return outs[:len(splits)], outs[len(splits):]


def _part_specs(tm, ncols):
    n_ctx_tiles = N_CTX_ROWS // tm
    return [pl.BlockSpec((tm, ncols), lambda i, *_: (jnp.minimum(i, n_ctx_tiles - 1), 0)),
            pl.BlockSpec((tm, ncols), lambda i, *_: (jnp.maximum(i - n_ctx_tiles, 0), 0))]


def _pick_part(i, tm, ctx_ref, lat_ref):
    return jnp.where(i < N_CTX_ROWS // tm, ctx_ref[...], lat_ref[...])


def _out_kernel(x_ref, gate_ref, *refs, split, tm):
    i = pl.program_id(0)
    n_act = sum(2 if s else 1 for s in split)
    a_refs, w_refs, o_ref = refs[:n_act], refs[n_act:n_act + len(split)], refs[n_act + len(split)]
    acc = None
    pos = 0
    for is_split, w_ref in zip(split, w_refs):
        if is_split:
            a = _pick_part(i, tm, a_refs[pos], a_refs[pos + 1])
            pos += 2
        else:
            a = a_refs[pos][...]
            pos += 1
        t = jnp.dot(a, w_ref[...], preferred_element_type=F32)
        acc = t if acc is None else acc + t
    o_ref[...] = x_ref[...] + gate_ref[0] * acc


def _out_proj(x, gate, acts, ws, name):
    tm = 512
    in_specs = [pl.BlockSpec((tm, D_MODEL), lambda i: (i, 0)), _mod_spec(tm)]
    flat, split = [], []
    for a in acts:
        if isinstance(a, tuple):
            in_specs += _part_specs(tm, a[0].shape[1])
            flat += list(a)
            split.append(True)
        else:
            in_specs.append(pl.BlockSpec((tm, a.shape[1]), lambda i: (i, 0)))
            flat.append(a)
            split.append(False)
    return pl.pallas_call(
        functools.partial(_out_kernel, split=tuple(split), tm=tm),
        grid=(M_ALL // tm,),
        in_specs=in_specs + [_full_spec(w.shape) for w in ws],
        out_specs=pl.BlockSpec((tm, D_MODEL), lambda i: (i, 0)),
        out_shape=jax.ShapeDtypeStruct((M_ALL, D_MODEL), F32),
        compiler_params=_cparams(("parallel",)),
        name=name,
    )(x, gate, *flat, *ws)


def _ffn_kernel(x_ref, g_ref, sh_ref, sc_ref, gate_ref, w1_ref, w3_ref, w2_ref, o_ref):
    x = x_ref[...]
    h = _norm_mod(x, g_ref[...], sh_ref[0], sc_ref[0]).astype(BF16)
    a = jnp.dot(h, w1_ref[...], preferred_element_type=F32)
    b = jnp.dot(h, w3_ref[...], preferred_element_type=F32)
    gated = ((a * _sigmoid(a)) * b).astype(BF16)
    o_ref[...] = x + gate_ref[0] * jnp.dot(gated, w2_ref[...], preferred_element_type=F32)


def _resident_spec(shape):
    n = len(shape)
    return pl.BlockSpec(shape, lambda *_: (0,) * n, pipeline_mode=pl.Buffered(1))


def _ffn(x, g, shift, scale, gate, w1, w3, w2, name):
    tm = 512
    return pl.pallas_call(
        _ffn_kernel,
        grid=(M_ALL // tm,),
        in_specs=[
            pl.BlockSpec((tm, D_MODEL), lambda i: (i, 0)),
            _full_spec((1, D_MODEL)),
            _mod_spec(tm),
            _mod_spec(tm),
            _mod_spec(tm),
            _resident_spec((D_MODEL, D_FF)),
            _resident_spec((D_MODEL, D_FF)),
            _resident_spec((D_FF, D_MODEL)),
        ],
        out_specs=pl.BlockSpec((tm, D_MODEL), lambda i: (i, 0)),
        out_shape=jax.ShapeDtypeStruct((M_ALL, D_MODEL), F32),
        compiler_params=_cparams(("parallel",)),
        name=name,
    )(x, g.reshape(1, D_MODEL), shift, scale, gate, w1, w3, w2)


def _final_norm_kernel(x_ref, g_ref, oc_ref, ol_ref, *, tm):
    i = pl.program_id(0)
    x = x_ref[...]
    r = lax.rsqrt(jnp.mean(x * x, axis=-1, keepdims=True) + NORM_EPS)
    y = (x * r) * g_ref[...]

    @pl.when(i < N_CTX_ROWS // tm)
    def _():
        oc_ref[...] = y

    @pl.when(i >= N_CTX_ROWS // tm)
    def _():
        ol_ref[...] = y


def _final_norm(x, g):
    tm = 1024
    return pl.pallas_call(
        functools.partial(_final_norm_kernel, tm=tm),
        grid=(M_ALL // tm,),
        in_specs=[pl.BlockSpec((tm, D_MODEL), lambda i: (i, 0)), _full_spec((1, D_MODEL))],
        out_specs=_part_specs(tm, D_MODEL),
        out_shape=[jax.ShapeDtypeStruct((N_CTX_ROWS, D_MODEL), F32),
                   jax.ShapeDtypeStruct((N_LAT_ROWS, D_MODEL), F32)],
        compiler_params=_cparams(("arbitrary",)),
        name="final_norm",
    )(x, g.reshape(1, D_MODEL))


def _half_masks():
    lane = lax.broadcasted_iota(jnp.int32, (1, LANES), 1)
    return (lane < HEAD_DIM, lane >= HEAD_DIM)


def _na_ctx_kernel(q_ref, k_ref, v_ref, o_ref):
    masks = _half_masks()
    work = []
    for p in range(D_NA // LANES):
        sl = slice(p * LANES, (p + 1) * LANES)
        q = q_ref[:, sl] * (QK_SCALE * LOG2_E)
        k = k_ref[:, sl].astype(BF16)
        v = v_ref[:, sl]
        for m in masks:
            work.append(dict(s=_bdot_nt(jnp.where(m, q, 0.0), k), vm=jnp.where(m, v, 0.0).astype(BF16)))
    for w in work:
        e = jnp.exp2(w["s"] - jnp.max(w["s"], axis=-1, keepdims=True))
        w["inv_l"] = 1.0 / jnp.sum(e, axis=-1, keepdims=True)
        w["e"] = e.astype(BF16)
    for w in work:
        w["o"] = jnp.dot(w["e"], w["vm"], preferred_element_type=F32) * w["inv_l"]
    for p in range(D_NA // LANES):
        o_ref[:, p * LANES:(p + 1) * LANES] = (work[2 * p]["o"] + work[2 * p + 1]["o"]).astype(BF16)


def _na_ctx(q, k, v):
    spec = pl.BlockSpec((SEQ, D_NA), lambda b: (b, 0))
    return pl.pallas_call(
        _na_ctx_kernel,
        grid=(BATCH,),
        in_specs=[spec, spec, spec],
        out_specs=spec,
        out_shape=jax.ShapeDtypeStruct((N_CTX_ROWS, D_NA), BF16),
        compiler_params=_cparams(("parallel",)),
        name="na_ctx",
    )(q, k, v)


def _na_lat_kernel(q_ref, k_ref, v_ref, kc_ref, vc_ref, bias_ref, o_ref, *, rows_per_step):
    n_loc = NA_KH * GRID_W
    masks = _half_masks()
    kc = kc_ref[0].astype(BF16)
    vc = vc_ref[0]
    vcm = [jnp.where(m, vc, 0.0).astype(BF16) for m in masks]
    work = []
    for rr in range(rows_per_step):
        i = pl.program_id(2) * rows_per_step + rr
        r0 = jnp.clip(i - NA_KH // 2, 0, GRID_ROWS - NA_KH)
        start = pl.multiple_of(r0 * GRID_W, GRID_W)
        q = q_ref[rr * GRID_W:(rr + 1) * GRID_W, :] * QK_SCALE
        kl = k_ref[pl.ds(start, n_loc), :].astype(BF16)
        vl = v_ref[pl.ds(start, n_loc), :]
        for hh, m in enumerate(masks):
            qm = jnp.where(m, q, 0.0).astype(BF16)
            work.append(dict(rr=rr, hh=hh, vlm=jnp.where(m, vl, 0.0).astype(BF16),
                             sl=_bdot_nt(qm, kl) + bias_ref[r0 - i + NA_KH - 1, hh], sc=_bdot_nt(qm, kc)))
    for w in work:
        mx = jnp.maximum(jnp.max(w["sl"], axis=-1, keepdims=True), jnp.max(w["sc"], axis=-1, keepdims=True))
        w["el"] = jnp.exp(w["sl"] - mx)
        w["ec"] = jnp.exp(w["sc"] - mx)
    for w in work:
        inv_l = 1.0 / (jnp.sum(w["el"], axis=-1, keepdims=True) + jnp.sum(w["ec"], axis=-1, keepdims=True))
        w["o"] = (_bdot(w["el"], w["vlm"]) + _bdot(w["ec"], vcm[w["hh"]])) * inv_l
    for rr in range(rows_per_step):
        a, b = [w["o"] for w in work if w["rr"] == rr]
        o_ref[rr * GRID_W:(rr + 1) * GRID_W, :] = (a + b).astype(BF16)


def _na_bias_table(rpb):
    jj = jnp.arange(GRID_W)[:, None]
    cc = jnp.arange(GRID_W)[None, :]
    c0 = jnp.clip(jj - NA_KW // 2, 0, GRID_W - NA_KW)
    inwin = (cc >= c0) & (cc < c0 + NA_KW)
    idx = jnp.clip(cc - jj + NA_KW - 1, 0, 2 * NA_KW - 2)
    full = rpb[:, :, idx]
    full = jnp.where(inwin[None, None], full, NEG_BIG)
    tabs = []
    for v in range(NA_KH):
        t = full[:, v:v + NA_KH]
        tabs.append(t.transpose(0, 2, 1, 3).reshape(N_HEADS, GRID_W, NA_KH * GRID_W))
    return jnp.stack(tabs, axis=0)


def _na_lat(q, k, v, kc, vc, bias_tab):
    rows_per_step = 4
    tq = rows_per_step * GRID_W
    steps = GRID_ROWS // rows_per_step
    lat_blk = N_CTX_ROWS // DEC_SEQ
    row_blk = N_CTX_ROWS // tq
    return pl.pallas_call(
        functools.partial(_na_lat_kernel, rows_per_step=rows_per_step),
        grid=(DEC_BATCH, D_NA // LANES, steps),
        in_specs=[
            pl.BlockSpec((tq, LANES), lambda b, p, i: (row_blk + b * steps + i, p)),
            pl.BlockSpec((DEC_SEQ, LANES), lambda b, p, i: (lat_blk + b, p)),
            pl.BlockSpec((DEC_SEQ, LANES), lambda b, p, i: (lat_blk + b, p)),
            pl.BlockSpec((1, PAST_LEN, LANES), lambda b, p, i: (b, 0, p)),
            pl.BlockSpec((1, PAST_LEN, LANES), lambda b, p, i: (b, 0, p)),
            pl.BlockSpec((NA_KH, 2, GRID_W, NA_KH * GRID_W), lambda b, p, i: (0, p, 0, 0)),
        ],
        out_specs=pl.BlockSpec((tq, LANES), lambda b, p, i: (b * steps + i, p)),
        out_shape=jax.ShapeDtypeStruct((N_LAT_ROWS, D_NA), BF16),
        compiler_params=_cparams(("parallel", "parallel", "arbitrary")),
        name="na_lat",
    )(q, k, v, kc, vc, bias_tab)


def _lam_value(lq_ref, lk_ref, lam_init):
    s = jnp.sum(lq_ref[...] * lk_ref[...], axis=-1, keepdims=True)
    return jnp.exp(s[0:1]) - jnp.exp(s[1:2]) + lam_init


def _subln(o, sub, lam_init):
    r = lax.rsqrt(jnp.mean(o * o, axis=-1, keepdims=True) + NORM_EPS)
    return ((o * r) * sub) * (1.0 - lam_init)


def _diff_ctx_kernel(lq_ref, lk_ref, sub_ref, q_ref, k_ref, v_ref, o_ref, *, lam_init):
    lam = _lam_value(lq_ref, lk_ref, lam_init)
    masks = _half_masks()
    work = []
    for h in range(N_HEADS):
        sl = slice(h * LANES, (h + 1) * LANES)
        q = q_ref[:, sl] * (QK_SCALE * LOG2_E)
        k = k_ref[:, sl].astype(BF16)
        for m in masks:
            work.append(dict(s=_bdot_nt(jnp.where(m, q, 0.0), k)))
    for w in work:
        e = jnp.exp2(w["s"] - jnp.max(w["s"], axis=-1, keepdims=True))
        w["p"] = e * (1.0 / jnp.sum(e, axis=-1, keepdims=True))
    outs = []
    for h in range(N_HEADS):
        att = work[2 * h]["p"] - lam * work[2 * h + 1]["p"]
        outs.append(_bdot(att, v_ref[:, h * LANES:(h + 1) * LANES]))
    for h in range(N_HEADS):
        o_ref[:, h * LANES:(h + 1) * LANES] = _subln(outs[h], sub_ref[...], lam_init).astype(BF16)


def _diff_ctx(q, k, v, lam_q, lam_k, subln, lam_init):
    spec = pl.BlockSpec((SEQ, D_DIFF), lambda b: (b, 0))
    return pl.pallas_call(
        functools.partial(_diff_ctx_kernel, lam_init=lam_init),
        grid=(BATCH,),
        in_specs=[_full_spec((2, HEAD_DIM)), _full_spec((2, HEAD_DIM)), _full_spec((1, LANES)),
                  spec, spec, spec],
        out_specs=spec,
        out_shape=jax.ShapeDtypeStruct((N_CTX_ROWS, D_DIFF), BF16),
        compiler_params=_cparams(("parallel",)),
        name="diff_ctx",
    )(lam_q, lam_k, subln.reshape(1, LANES), q, k, v)


def _proj_diff_kernel(*refs, n_prev, slot, tm):
    x_ref, g_ref, sh_ref, sc_ref, w_ref, cos_ref, sin_ref = refs[:7]
    q_ref, k_ref, v_ref, kc_ref, vc_ref, qr_ref, kr_ref, vt_ref = refs[7 + n_prev:]
    i = pl.program_id(0)
    n_ctx_tiles = N_CTX_ROWS // tm

    def project():
        h = _norm_mod(x_ref[...], g_ref[...], sh_ref[0], sc_ref[0]).astype(BF16)
        return [jnp.dot(h, w_ref[:, j * D_DIFF:(j + 1) * D_DIFF], preferred_element_type=F32) for j in range(3)]

    @pl.when(i < n_ctx_tiles)
    def _():
        ys = project()
        for o_ref, y in zip((q_ref, k_ref, v_ref), ys):
            o_ref[...] = y
        for c_ref, y in ((kc_ref, ys[1]), (vc_ref, ys[2])):
            for s in range(tm // SEQ):
                c_ref[s, 0] = y[s * SEQ:(s + 1) * SEQ]
                if slot == 0:
                    c_ref[s, 1] = jnp.zeros((SEQ, D_DIFF), F32)

    @pl.when(i >= n_ctx_tiles)
    def _():
        ys = project()
        cos = cos_ref[...]
        sin = sin_ref[...]
        lane = lax.broadcasted_iota(jnp.int32, (1, LANES), 1)
        first = (lane % (2 * ROPE_F)) < ROPE_F
        for y, o_ref, scale in ((ys[0], qr_ref, QK_SCALE * LOG2_E), (ys[1], kr_ref, None)):
            for j in range(D_DIFF // LANES):
                x = y[:, j * LANES:(j + 1) * LANES]
                partner = jnp.where(first, pltpu.roll(x, LANES - ROPE_F, 1), pltpu.roll(x, ROPE_F, 1))
                r = x * cos + partner * sin
                if scale is not None:
                    r = r * scale
                o_ref[:, j * LANES:(j + 1) * LANES] = r.astype(BF16)
        vt_ref[0] = ys[2].T.astype(BF16)


def _rope_tables():
    t = jnp.arange(DEC_SEQ)
    pos = jnp.stack([t // GRID_W, t % GRID_W], -1).astype(F32)
    inv = ROPE_BASE ** (-jnp.arange(ROPE_F, dtype=F32) / ROPE_F)
    ang = pos[:, :, None] * inv
    cos, sin = jnp.cos(ang), jnp.sin(ang)
    cos64 = jnp.concatenate([cos, cos], axis=-1).reshape(DEC_SEQ, HEAD_DIM)
    sin64 = jnp.concatenate([-sin, sin], axis=-1).reshape(DEC_SEQ, HEAD_DIM)
    return jnp.tile(cos64, (1, 2)), jnp.tile(sin64, (1, 2))


def _proj_diff(x, g, shift, scale, w_bf16, cos_t, sin_t, slot, prev_caches):
    tm = 512
    n_ctx_tiles = N_CTX_ROWS // tm
    per_seq = DEC_SEQ // tm
    seq_per_tile = tm // SEQ
    n_prev = len(prev_caches)

    def lat(i):
        return jnp.maximum(i - n_ctx_tiles, 0)

    ctx_spec, lat_spec = _part_specs(tm, D_DIFF)
    tab_spec = pl.BlockSpec((tm, LANES), lambda i: (lat(i) % per_seq, 0))
    if slot == 0:
        cache_spec = pl.BlockSpec((seq_per_tile, 2, SEQ, D_DIFF),
                                  lambda i: (jnp.minimum(i, n_ctx_tiles - 1), 0, 0, 0))
    else:
        cache_spec = pl.BlockSpec((seq_per_tile, 1, SEQ, D_DIFF),
                                  lambda i: (jnp.minimum(i, n_ctx_tiles - 1), slot, 0, 0))
    ctx_shape = jax.ShapeDtypeStruct((N_CTX_ROWS, D_DIFF), F32)
    cache_shape = jax.ShapeDtypeStruct((BATCH, 2, SEQ, D_DIFF), F32)
    lat_shape = jax.ShapeDtypeStruct((N_LAT_ROWS, D_DIFF), BF16)
    outs = pl.pallas_call(
        functools.partial(_proj_diff_kernel, n_prev=n_prev, slot=slot, tm=tm),
        grid=(M_ALL // tm,),
        in_specs=[
            pl.BlockSpec((tm, D_MODEL), lambda i: (i, 0)),
            _full_spec((1, D_MODEL)),
            _mod_spec(tm),
            _mod_spec(tm),
            _resident_spec((D_MODEL, 3 * D_DIFF)),
            tab_spec,
            tab_spec,
        ] + [pl.BlockSpec(memory_space=pl.ANY)] * n_prev,
        out_specs=[ctx_spec, ctx_spec, ctx_spec, cache_spec, cache_spec, lat_spec, lat_spec,
                   pl.BlockSpec((1, D_DIFF, tm), lambda i: (lat(i) // per_seq, 0, lat(i) % per_seq))],
        out_shape=[ctx_shape, ctx_shape, ctx_shape, cache_shape, cache_shape, lat_shape, lat_shape,
                   jax.ShapeDtypeStruct((DEC_BATCH, D_DIFF, DEC_SEQ), BF16)],
        input_output_aliases={7 + j: 3 + j for j in range(n_prev)},
        compiler_params=_cparams(("arbitrary",)),
        name="proj_odd",
    )(x, g.reshape(1, D_MODEL), shift, scale, w_bf16, cos_t, sin_t, *prev_caches)
    return outs[:3], outs[3:5], outs[5:]


def _diff_lat_kernel(lq_ref, lk_ref, sub_ref, q_ref, kl_ref, vlt_ref, kc_ref, vct_ref, o_ref, *,
                     lam_init, sub_q, n_sub):
    lam = _lam_value(lq_ref, lk_ref, lam_init)
    kl = kl_ref[...]
    kc = kc_ref[0].astype(BF16)
    vlt = vlt_ref[0]
    vct = vct_ref[0].astype(BF16)
    work = []
    for t in range(n_sub):
        q = q_ref[t * sub_q:(t + 1) * sub_q, :]
        for m in _half_masks():
            qm = jnp.where(m, q, jnp.zeros_like(q))
            work.append(dict(sl=_bdot_nt(kl, qm), sc=_bdot_nt(kc, qm)))
    for w in work:
        mx = jnp.maximum(jnp.max(w["sl"], axis=0, keepdims=True), jnp.max(w["sc"], axis=0, keepdims=True))
        el = jnp.exp2(w["sl"] - mx)
        ec = jnp.exp2(w["sc"] - mx)
        w["inv_l"] = 1.0 / (jnp.sum(el, axis=0, keepdims=True) + jnp.sum(ec, axis=0, keepdims=True))
        w["el"] = el.astype(BF16)
        w["ec"] = ec.astype(BF16)
    for w in work:
        w["o"] = (jnp.dot(vlt, w["el"], preferred_element_type=F32)
                  + jnp.dot(vct, w["ec"], preferred_element_type=F32)) * w["inv_l"]
    for t in range(n_sub):
        ot = work[2 * t]["o"] - lam * work[2 * t + 1]["o"]
        r = lax.rsqrt(jnp.mean(ot * ot, axis=0, keepdims=True) + NORM_EPS)
        on = ((ot * r) * sub_ref[...]) * (1.0 - lam_init)
        o_ref[t * sub_q:(t + 1) * sub_q, :] = on.T.astype(BF16)


def _diff_lat(q_r, k_r, v_t, kc, vc_t, lam_q, lam_k, subln, lam_init):
    sub_q, n_sub = 256, 2
    tq = sub_q * n_sub
    per_seq = DEC_SEQ // tq
    return pl.pallas_call(
        functools.partial(_diff_lat_kernel, lam_init=lam_init, sub_q=sub_q, n_sub=n_sub),
        grid=(DEC_BATCH, N_HEADS, per_seq),
        in_specs=[
            _full_spec((2, HEAD_DIM)),
            _full_spec((2, HEAD_DIM)),
            _full_spec((LANES, 1)),
            pl.BlockSpec((tq, LANES), lambda b, h, i: (b * per_seq + i, h)),
            pl.BlockSpec((DEC_SEQ, LANES), lambda b, h, i: (b, h)),
            pl.BlockSpec((1, LANES, DEC_SEQ), lambda b, h, i: (b, h, 0)),
            pl.BlockSpec((1, PAST_LEN, LANES), lambda b, h, i: (b, 0, h)),
            pl.BlockSpec((1, LANES, PAST_LEN), lambda b, h, i: (b, h, 0)),
        ],
        out_specs=pl.BlockSpec((tq, LANES), lambda b, h, i: (b * per_seq + i, h)),
        out_shape=jax.ShapeDtypeStruct((N_LAT_ROWS, D_DIFF), BF16),
        compiler_params=_cparams(("parallel", "parallel", "arbitrary")),
        name="diff_lat",
    )(lam_q, lam_k, subln.reshape(LANES, 1), q_r, k_r, v_t, kc, vc_t)


def _token_shift(u, prev_row, next_row, mu):
    n = u.shape[0]
    rows = lax.broadcasted_iota(jnp.int32, (n, 1), 0)
    up = jnp.where(rows == 0, prev_row, pltpu.roll(u, 1, 0))
    un = jnp.where(rows == n - 1, next_row, pltpu.roll(u, n - 1, 0))
    return u + mu[0:1] * (up - u) + mu[1:2] * (un - u)


def _icl_rate_and_key(us, a0, a2p, k_a):
    lo = us[:, 3 * D_RWKV:3 * D_RWKV + LANES]
    k = us[:, D_RWKV:2 * D_RWKV]
    a = _sigmoid(a0 + _bdot(lo, a2p))
    return a, k * (1.0 + (a - 1.0) * k_a)


def _stack_blockdiag(x):
    shape = (2 * x.shape[0], x.shape[1])
    row = lax.broadcasted_iota(jnp.int32, shape, 0)
    col = lax.broadcasted_iota(jnp.int32, shape, 1)
    keep = (row // HEAD_DIM) == ((col % LANES) // HEAD_DIM)
    return jnp.where(keep, jnp.concatenate([x, x], axis=0), 0.0).astype(BF16)


def _rwkv_chunk_inputs(u, prev_row, next_row, reverse, mu, w0, w2p, a0, a2p, k_k, k_a, bd_ones):
    us = _token_shift(u, prev_row, next_row, mu)
    r = us[:, 0:D_RWKV]
    k = us[:, D_RWKV:2 * D_RWKV]
    v = us[:, 2 * D_RWKV:3 * D_RWKV]
    lo = us[:, 3 * D_RWKV:3 * D_RWKV + LANES]
    w_raw = w0 + _bdot(jnp.tanh(lo), w2p)
    logw = -jnp.exp(-_softplus(-w_raw) - 0.5)
    a, kt = _icl_rate_and_key(us, a0, a2p, k_a)
    kk_raw = k * k_k
    ss = _head_sums(kk_raw * kk_raw, bd_ones)
    kk = kk_raw * lax.rsqrt(jnp.maximum(ss, 1e-12))
    row = lax.broadcasted_iota(jnp.int32, (CHUNK, CHUNK), 0)
    col = lax.broadcasted_iota(jnp.int32, (CHUNK, CHUNK), 1)
    seen = (col >= row) if reverse else (col <= row)
    cum = _dot3_lhs_exact(seen.astype(F32).astype(BF16), logw)
    p_inv = jnp.exp(-cum)
    tot = cum[0:1, :] if reverse else cum[CHUNK - 1:CHUNK, :]
    return dict(a_hat=jnp.exp(cum - logw) * kk, b_hat=(kk * a) * p_inv, k_hat=kt * p_inv,
                r_hat=r * jnp.exp(cum), v=v, p_tot=jnp.exp(tot))


def _rwkv_scan_kernel(*refs, n_chunks, n_par, has_state):
    u_refs = (refs[0:3], refs[3:6])
    pos = 6
    s0_ref = None
    if has_state:
        s0_ref = refs[pos]
        pos += 1
    mu_ref, w0_ref, w2_ref, a0_ref, a2_ref, kk_ref, ka_ref, bd_ref = refs[pos:pos + 8]
    y_refs = refs[pos + 8:pos + 10]
    sfin_ref, s_ref = refs[pos + 10], refs[pos + 11]
    c = pl.program_id(1)
    n_pairs = D_RWKV // LANES

    @pl.when(c == 0)
    def _():
        if has_state:
            s_ref[...] = s0_ref[...]
        else:
            s_ref[...] = jnp.zeros(s_ref.shape, F32)

    at_start = jnp.where(c == 0, 0.0, 1.0)
    at_end = jnp.where(c == n_chunks - 1, 0.0, 1.0)
    has_prev = (at_start, at_end)
    has_next = (at_end, at_start)

    row = lax.broadcasted_iota(jnp.int32, (CHUNK, LANES), 0)
    colm = lax.broadcasted_iota(jnp.int32, (CHUNK, LANES), 1) % CHUNK
    strict = (colm < row, colm > row)
    incl = (colm <= row, colm >= row)
    same_blk = (row // INV_BLOCK) == (colm // INV_BLOCK)
    eye = (row == colm).astype(F32)
    prow = lax.broadcasted_iota(jnp.int32, (LANES, LANES), 0)
    pcol = lax.broadcasted_iota(jnp.int32, (LANES, LANES), 1)
    blockdiag = (prow // HEAD_DIM) == (pcol // HEAD_DIM)
    eye_pair = (prow == pcol).astype(F32)

    def stream_chains(j, d):
        u_ref, up_ref, un_ref = u_refs[d]
        q = _rwkv_chunk_inputs(
            u_ref[j], up_ref[j, 7:8, :] * has_prev[d], un_ref[j, 0:1, :] * has_next[d], d == 1,
            mu_ref[...], w0_ref[d], w2_ref[d], a0_ref[d], a2_ref[d], kk_ref[...], ka_ref[...], bd_ref[...])
        chains = []
        for p in range(n_pairs):
            sl = slice(p * LANES, (p + 1) * LANES)
            chains.append(dict(j=j, d=d, p=p, sl=sl, ah=q["a_hat"][:, sl], bh=q["b_hat"][:, sl],
                               kh=q["k_hat"][:, sl], rh=q["r_hat"][:, sl], vv=q["v"][:, sl],
                               pt=q["p_tot"][:, sl]))
        return chains

    def s_scores(ch):
        ar = jnp.concatenate([ch["ah"], ch["rh"]], axis=0)
        x = _bdot_nt(ar, jnp.concatenate([_stack_blockdiag(ch["bh"]), _stack_blockdiag(ch["kh"])], axis=0))
        xb, xk = x[:, :LANES], x[:, LANES:]
        d = ch["d"]
        l_ab = jnp.where(strict[d], xb[:CHUNK], 0.0)
        ch["l_ak"] = jnp.where(strict[d], xk[:CHUNK], 0.0)
        ch["m_rb"] = jnp.where(incl[d], xb[CHUNK:], 0.0)
        ch["m_rk"] = jnp.where(incl[d], xk[CHUNK:], 0.0)
        ch["x1"] = -jnp.where(same_blk, l_ab, 0.0)
        ch["l_off"] = jnp.where(same_blk, 0.0, l_ab)
        ch["vs"] = _stack_blockdiag(ch["vv"])

    def s_x2(ch):
        ch["x2"] = _bdot(ch["x1"], _stack_blockdiag(ch["x1"]))
        ch["x2s"] = _stack_blockdiag(ch["x2"])
        ch["u"] = _bdot(ch["l_ak"], ch["vs"])

    def s_x4(ch):
        td = eye + ch["x1"]
        both = _bdot(jnp.concatenate([ch["x2"], td], axis=0), ch["x2s"])
        ch["x4"] = both[:CHUNK]
        ch["td"] = td + both[CHUNK:]

    def s_x8(ch):
        both = _bdot(jnp.concatenate([ch["x4"], ch["td"]], axis=0), _stack_blockdiag(ch["x4"]))
        ch["x8s"] = _stack_blockdiag(both[:CHUNK])
        ch["td"] = ch["td"] + both[CHUNK:]

    def s_td(ch):
        ch["td"] = ch["td"] + _bdot(ch["td"], ch["x8s"])

    def s_mm(ch):
        ch["mm"] = _bdot(ch["td"], _stack_blockdiag(ch["l_off"]))

    def s_m2(ch):
        ch["m2s"] = _stack_blockdiag(_bdot(ch["mm"], _stack_blockdiag(ch["mm"])))

    def s_n2(ch):
        n1 = eye - ch["mm"]
        ch["n2"] = n1 + _bdot(n1, ch["m2s"])

    def s_tinv(ch):
        ch["t_inv"] = _bdot(ch["n2"], _stack_blockdiag(ch["td"]))

    def s_tx(ch):
        ch["tx"] = _bdot(ch["t_inv"], _stack_blockdiag(jnp.concatenate([ch["ah"], ch["u"]], axis=1)))

    def s_local(ch):
        tx = ch["tx"]
        mx = _bdot(ch["m_rb"], _stack_blockdiag(tx))
        ch["q_eff"] = ch["rh"] - mx[:, :LANES]
        ch["y_loc"] = _bdot(ch["m_rk"], ch["vs"]) - mx[:, LANES:]
        ch["g"] = jnp.where(blockdiag, eye_pair - _bdot_tn(tx[:, :LANES], ch["bh"]), 0.0) * ch["pt"]
        ch["h"] = jnp.where(
            blockdiag,
            _bdot_tn(jnp.concatenate([ch["vv"], -tx[:, LANES:]], axis=0),
                     jnp.concatenate([ch["kh"], ch["bh"]], axis=0)),
            0.0) * ch["pt"]

    def s_state(ch):
        j, d, p = ch["j"], ch["d"], ch["p"]
        s_old = s_ref[j, d, p]
        y_refs[d][j, :, ch["sl"]] = _bdot_nt(ch["q_eff"], s_old) + ch["y_loc"]
        s_ref[j, d, p] = _bdot(s_old, ch["g"]) + ch["h"]

    chains = [ch for j in range(n_par) for d in range(2) for ch in stream_chains(j, d)]
    for stage in (s_scores, s_x2, s_x4, s_x8, s_td, s_mm, s_m2, s_n2, s_tinv, s_tx, s_local, s_state):
        for ch in chains:
            stage(ch)

    @pl.when(c == n_chunks - 1)
    def _():
        sfin_ref[...] = s_ref[...]


def _rwkv_scan(u, s0p, prm, row_base, n_seq, seq_len):
    n_par = 2
    n_chunks = seq_len // CHUNK
    n_pairs = D_RWKV // LANES
    base_blk = row_base // (seq_len * n_par)
    n_blk8 = seq_len // 8
    has_state = s0p is not None
    u3 = u.reshape(M_ALL // seq_len, seq_len, P_RWKV)

    def chunk_pos(d, c):
        return c if d == 0 else n_chunks - 1 - c

    in_specs, args = [], []
    for d in range(2):
        in_specs += [
            pl.BlockSpec((n_par, CHUNK, P_RWKV), lambda g, c, d=d: (base_blk + g, chunk_pos(d, c), 0)),
            pl.BlockSpec((n_par, 8, P_RWKV),
                         lambda g, c, d=d: (base_blk + g, jnp.maximum(chunk_pos(d, c) * (CHUNK // 8) - 1, 0), 0)),
            pl.BlockSpec((n_par, 8, P_RWKV),
                         lambda g, c, d=d: (base_blk + g,
                                            jnp.minimum((chunk_pos(d, c) + 1) * (CHUNK // 8), n_blk8 - 1), 0)),
        ]
        args += [u3, u3, u3]
    state_spec = pl.BlockSpec((n_par, 2, n_pairs, LANES, LANES), lambda g, c: (g, 0, 0, 0, 0))
    if has_state:
        in_specs.append(state_spec)
        args.append(s0p)
    in_specs += [
        _full_spec((2, P_RWKV)),
        _full_spec((2, 1, D_RWKV)),
        _full_spec((2, LANES, D_RWKV)),
        _full_spec((2, 1, D_RWKV)),
        _full_spec((2, LANES, D_RWKV)),
        _full_spec((1, D_RWKV)),
        _full_spec((1, D_RWKV)),
        _full_spec((MXU_WIDTH, MXU_WIDTH)),
    ]
    args += [prm["mu"], prm["w0"], prm["w2p"], prm["a0"], prm["a2p"], prm["kk"], prm["ka"], prm["bd_ones"]]
    y_shape = jax.ShapeDtypeStruct((n_seq, seq_len, D_RWKV), F32)
    y_f, y_b, s_fin = pl.pallas_call(
        functools.partial(_rwkv_scan_kernel, n_chunks=n_chunks, n_par=n_par, has_state=has_state),
        grid=(n_seq // n_par, n_chunks),
        in_specs=in_specs,
        out_specs=[
            pl.BlockSpec((n_par, CHUNK, D_RWKV), lambda g, c: (g, chunk_pos(0, c), 0)),
            pl.BlockSpec((n_par, CHUNK, D_RWKV), lambda g, c: (g, chunk_pos(1, c), 0)),
            state_spec,
        ],
        out_shape=[y_shape, y_shape, jax.ShapeDtypeStruct((n_seq, 2, n_pairs, LANES, LANES), F32)],
        scratch_shapes=[pltpu.VMEM((n_par, 2, n_pairs, LANES, LANES), F32)],
        compiler_params=_cparams(("parallel", "arbitrary")),
        name="rwkv_scan_" + ("lat" if has_state else "ctx"),
    )(*args)
    n_rows = n_seq * seq_len
    return (y_f.reshape(n_rows, D_RWKV), y_b.reshape(n_rows, D_RWKV)), s_fin


def _rwkv_fin_kernel(u_ref, up_ref, un_ref, ycf_ref, ylf_ref, ycb_ref, ylb_ref, mu_ref, a0_ref, a2_ref,
                     ka_ref, bonus_ref, g2_ref, lnw_ref, lnb_ref, bd1_ref, bdm_ref, o_ref, *, tm):
    i = pl.program_id(0)
    n_ctx_tiles = N_CTX_ROWS // tm
    per_seq = DEC_SEQ // tm
    j = (i - n_ctx_tiles) % per_seq
    is_ctx = i < n_ctx_tiles
    not_first = jnp.where(is_ctx | (j == 0), 0.0, 1.0)
    not_last = jnp.where(is_ctx | (j == per_seq - 1), 0.0, 1.0)
    us = _token_shift(u_ref[...], up_ref[7:8, :] * not_first, un_ref[0:1, :] * not_last, mu_ref[...])
    r = us[:, 0:D_RWKV]
    v = us[:, 2 * D_RWKV:3 * D_RWKV]
    g_lo = us[:, 3 * D_RWKV + LANES:]
    rk = jnp.zeros((tm, D_RWKV), F32)
    for d in range(2):
        _, kt = _icl_rate_and_key(us, a0_ref[d], a2_ref[d], ka_ref[...])
        rk = rk + r * kt * bonus_ref[d]
    bon = _head_sums(rk, bd1_ref[...]) * v
    y = _pick_part(i, tm, ycf_ref, ylf_ref) + _pick_part(i, tm, ycb_ref, ylb_ref)
    mean = _head_sums(y, bdm_ref[...])
    yc = y - mean
    var = _head_sums(yc * yc, bdm_ref[...])
    yn = (yc * lax.rsqrt(var + GN_EPS)) * lnw_ref[...] + lnb_ref[...]
    gate = _bdot(_sigmoid(g_lo), g2_ref[...])
    o_ref[...] = ((yn + bon) * gate).astype(BF16)


def _rwkv_finish(u, y_ctx, y_lat, prm):
    tm = 256
    n_blk8 = M_ALL // 8
    return pl.pallas_call(
        functools.partial(_rwkv_fin_kernel, tm=tm),
        grid=(M_ALL // tm,),
        in_specs=[
            pl.BlockSpec((tm, P_RWKV), lambda i: (i, 0)),
            pl.BlockSpec((8, P_RWKV), lambda i: (jnp.maximum(i * (tm // 8) - 1, 0), 0)),
            pl.BlockSpec((8, P_RWKV), lambda i: (jnp.minimum((i + 1) * (tm // 8), n_blk8 - 1), 0)),
            *_part_specs(tm, D_RWKV),
            *_part_specs(tm, D_RWKV),
            _full_spec((2, P_RWKV)),
            _full_spec((2, 1, D_RWKV)),
            _full_spec((2, LANES, D_RWKV)),
            _full_spec((1, D_RWKV)),
            _full_spec((2, 1, D_RWKV)),
            _full_spec((GATE_LORA, D_RWKV)),
            _full_spec((1, D_RWKV)),
            _full_spec((1, D_RWKV)),
            _full_spec((MXU_WIDTH, MXU_WIDTH)),
            _full_spec((MXU_WIDTH, MXU_WIDTH)),
        ],
        out_specs=pl.BlockSpec((tm, D_RWKV), lambda i: (i, 0)),
        out_shape=jax.ShapeDtypeStruct((M_ALL, D_RWKV), BF16),
        compiler_params=_cparams(("parallel",)),
        name="rwkv_finish",
    )(u, u, u, y_ctx[0], y_lat[0], y_ctx[1], y_lat[1], prm["mu"], prm["a0"], prm["a2p"], prm["ka"],
      prm["bonus"], prm["g2"], prm["lnw"], prm["lnb"], prm["bd_ones"], prm["bd_mean"])


def _pack_state_pairs(s):
    lead = s.shape[:-3]
    s = s.reshape(lead + (N_HEADS // 2, 2, HEAD_DIM, HEAD_DIM))
    z = jnp.zeros_like(s[..., 0, :, :])
    top = jnp.concatenate([s[..., 0, :, :], z], axis=-1)
    bot = jnp.concatenate([z, s[..., 1, :, :]], axis=-1)
    return jnp.concatenate([top, bot], axis=-2)


def _unpack_state_pairs(sp):
    lead = sp.shape[:-3]
    a = sp[..., :HEAD_DIM, :HEAD_DIM]
    b = sp[..., HEAD_DIM:, HEAD_DIM:]
    return jnp.stack([a, b], axis=-3).reshape(lead + (N_HEADS, HEAD_DIM, HEAD_DIM))


def kernel(x_prompt, x_sample, c, cache_na_k, cache_na_v, state_rwkv, cache_diff_k, cache_diff_v, c_ctx, w_ada, b_ada, norm_mix, norm_ffn, norm_final, w_in_even, w_out_even, na_rpb, rw_mu, rw_w0, rw_w2, rw_a0, rw_a2, rw_kk, rw_ka, rw_bonus, rw_g2, rw_lnw, rw_lnb, w_qkv_diff, w_out_diff, diff_lam_q, diff_lam_k, diff_subln, ffn_w1, ffn_w3, ffn_w2):
    x = jnp.concatenate([x_prompt.reshape(N_CTX_ROWS, D_MODEL), x_sample.reshape(N_LAT_ROWS, D_MODEL)], axis=0)

    cv8 = jnp.concatenate([c_ctx[None, :], c, jnp.zeros((8 - 1 - DEC_BATCH, D_MODEL), F32)], axis=0)
    mods = _ada_all(cv8, w_ada, b_ada)
    mods = mods.reshape(DEPTH, 8, 6, D_MODEL).transpose(0, 2, 1, 3)[:, :, :1 + DEC_BATCH, None, :]

    hd_idx = jnp.arange(MXU_WIDTH) // HEAD_DIM
    bd_ones = (hd_idx[:, None] == hd_idx[None, :]).astype(BF16)
    bd_mean = (bd_ones.astype(F32) / HEAD_DIM).astype(BF16)
    cos_t, sin_t = _rope_tables()

    rw_out = []
    na_caches, diff_caches = (), ()
    for l in range(DEPTH):
        md = mods[l]
        if l % 2 == 0:
            e = l // 2
            (q, k, v, u), na_caches = _norm_proj(
                x, norm_mix[l], md[0], md[1], w_in_even[e].astype(BF16), (D_NA, D_NA, D_NA, P_RWKV),
                "proj_even", cache_cols=(1, 2), slot=e, prev_caches=na_caches)
            o_ctx = _na_ctx(q, k, v)
            o_lat = _na_lat(q, k, v, cache_na_k[:, e].reshape(DEC_BATCH, PAST_LEN, D_NA),
                            cache_na_v[:, e].reshape(DEC_BATCH, PAST_LEN, D_NA), _na_bias_table(na_rpb[e]))

            zpad = jnp.zeros((2, LANES - DECAY_LORA, D_RWKV), F32)
            prm = {
                "mu": rw_mu[e],
                "w0": rw_w0[e].reshape(2, 1, D_RWKV),
                "w2p": jnp.concatenate([rw_w2[e], zpad], axis=1),
                "a0": rw_a0[e].reshape(2, 1, D_RWKV),
                "a2p": jnp.concatenate([zpad, rw_a2[e]], axis=1),
                "kk": rw_kk[e].reshape(1, D_RWKV),
                "ka": rw_ka[e].reshape(1, D_RWKV),
                "bonus": rw_bonus[e].reshape(2, 1, D_RWKV),
                "g2": rw_g2[e],
                "lnw": rw_lnw[e].reshape(1, D_RWKV),
                "lnb": rw_lnb[e].reshape(1, D_RWKV),
                "bd_ones": bd_ones,
                "bd_mean": bd_mean,
            }
            y_ctx, s_ctx = _rwkv_scan(u, None, prm, 0, BATCH, SEQ)
            y_lat, _ = _rwkv_scan(u, _pack_state_pairs(state_rwkv[:, e]), prm, N_CTX_ROWS, DEC_BATCH, DEC_SEQ)
            o_rw = _rwkv_finish(u, y_ctx, y_lat, prm)

            w_out = w_out_even[e].astype(BF16)
            x = _out_proj(x, md[2], [(o_ctx, o_lat), o_rw], [w_out[:D_NA], w_out[D_NA:]], "out_even")

            rw_out.append(_unpack_state_pairs(s_ctx))
        else:
            o = l // 2
            lam_init = 0.8 - 0.6 * math.exp(-0.3 * l)
            (q, k, v), diff_caches, (q_r, k_r, v_t) = _proj_diff(
                x, norm_mix[l], md[0], md[1], w_qkv_diff[o].astype(BF16), cos_t, sin_t, o, diff_caches)
            o_ctx = _diff_ctx(q, k, v, diff_lam_q[o], diff_lam_k[o], diff_subln[o], lam_init)
            o_lat = _diff_lat(q_r, k_r, v_t, cache_diff_k[:, o].reshape(DEC_BATCH, PAST_LEN, D_DIFF),
                              cache_diff_v[:, o].reshape(DEC_BATCH, PAST_LEN, D_DIFF).swapaxes(1, 2),
                              diff_lam_q[o], diff_lam_k[o], diff_subln[o], lam_init)
            x = _out_proj(x, md[2], [(o_ctx, o_lat)], [w_out_diff[o].astype(BF16)], "out_odd")
        x = _ffn(x, norm_ffn[l], md[3], md[4], md[5], ffn_w1[l].astype(BF16), ffn_w3[l].astype(BF16),
                 ffn_w2[l].astype(BF16), "ffn")

    y_ctx, y_lat = _final_norm(x, norm_final)
    y_prompt = y_ctx.reshape(BATCH, SEQ, D_MODEL)
    y_sample = y_lat.reshape(DEC_BATCH, DEC_SEQ, D_MODEL)
    na_shape = (BATCH, DEPTH // 2, SEQ, N_HEADS, HEAD_DIM)
    diff_shape = (BATCH, DEPTH // 2, SEQ, N_HEADS, 2 * HEAD_DIM)
    return (y_prompt, y_sample, na_caches[0].reshape(na_shape), na_caches[1].reshape(na_shape),
            jnp.stack(rw_out, axis=1), diff_caches[0].reshape(diff_shape), diff_caches[1].reshape(diff_shape))
```

```python
import functools
import math

import jax
import jax.numpy as jnp
from jax import lax
from jax.experimental import pallas as pl
from jax.experimental.pallas import tpu as pltpu

F32 = jnp.float32
BF16 = jnp.bfloat16

D_MODEL = 1024
BATCH = 32
SEQ = 256
DEPTH = 4
DEC_BATCH = 2
DEC_SEQ = 4096
PAST_LEN = 512
GRID_W = 64
GRID_ROWS = DEC_SEQ // GRID_W
HEAD_DIM = 64
N_HEADS = 8
D_NA = 512
D_RWKV = 512
D_DIFF = 1024
NA_KH = 8
NA_KW = 16
DECAY_LORA = 64
ICL_LORA = 64
GATE_LORA = 128
P_RWKV = 3 * D_RWKV + DECAY_LORA + ICL_LORA + GATE_LORA
P_EVEN = 3 * D_NA + P_RWKV
D_FF = 2816
ROPE_F = HEAD_DIM // 4
ROPE_BASE = 10000.0
NORM_EPS = 1e-6
GN_EPS = 64e-5
QK_SCALE = HEAD_DIM ** -0.5
LOG2_E = math.log2(math.e)

N_CTX_ROWS = BATCH * SEQ
N_LAT_ROWS = DEC_BATCH * DEC_SEQ
M_ALL = N_CTX_ROWS + N_LAT_ROWS

LANES = 128
MXU_WIDTH = 256
CHUNK = 64
INV_BLOCK = 16
NEG_BIG = -1e30
VMEM_LIMIT = 56 * 1024 * 1024


def _cparams(sem):
    return pltpu.CompilerParams(dimension_semantics=sem, vmem_limit_bytes=VMEM_LIMIT)


def _bdot(a, b):
    return jnp.dot(a.astype(BF16), b.astype(BF16), preferred_element_type=F32)


def _bdot_nt(a, b):
    return lax.dot_general(a.astype(BF16), b.astype(BF16), (((1,), (1,)), ((), ())),
                           preferred_element_type=F32)


def _bdot_tn(a, b):
    return lax.dot_general(a.astype(BF16), b.astype(BF16), (((0,), (0,)), ((), ())),
                           preferred_element_type=F32)


def _split3(x):
    hi = x.astype(BF16)
    r1 = x - hi.astype(F32)
    mid = r1.astype(BF16)
    lo = (r1 - mid.astype(F32)).astype(BF16)
    return hi, mid, lo


def _head_sums(x, bd):
    half = bd.shape[0]
    d = functools.partial(jnp.dot, preferred_element_type=F32)
    outs = []
    for j in range(x.shape[1] // half):
        xs = x[:, j * half:(j + 1) * half]
        hi = xs.astype(BF16)
        lo = (xs - hi.astype(F32)).astype(BF16)
        outs.append(d(hi, bd) + d(lo, bd))
    return jnp.concatenate(outs, axis=1)


def _dot3_lhs_exact(e, x):
    hi, mid, lo = _split3(x)
    d = functools.partial(jnp.dot, preferred_element_type=F32)
    return d(e, hi) + d(e, mid) + d(e, lo)


def _sigmoid(x):
    return 1.0 / (1.0 + jnp.exp(-x))


def _softplus(x):
    return jnp.maximum(x, 0.0) + jnp.log(1.0 + jnp.exp(-jnp.abs(x)))


def _norm_mod(x, g, shift, scale):
    r = lax.rsqrt(jnp.mean(x * x, axis=-1, keepdims=True) + NORM_EPS)
    return ((x * r) * g) * (1.0 + scale) + shift


def _group_of_tile(i, tm):
    n_ctx_tiles = N_CTX_ROWS // tm
    tiles_per_lat = DEC_SEQ // tm
    return jnp.where(i < n_ctx_tiles, 0, 1 + (i - n_ctx_tiles) // tiles_per_lat)


def _mod_spec(tm):
    return pl.BlockSpec((1, 1, D_MODEL), lambda i, *_: (_group_of_tile(i, tm), 0, 0))


def _full_spec(shape):
    n = len(shape)
    return pl.BlockSpec(shape, lambda *_: (0,) * n)


def _ada_kernel(cv_ref, w_ref, b_ref, o_ref):
    cv = cv_ref[...]
    s = cv * _sigmoid(cv)
    o_ref[0] = _bdot(s, w_ref[0]) + b_ref[0]


def _ada_all(cv8, w_ada, b_ada):
    tn = 512
    return pl.pallas_call(
        _ada_kernel,
        grid=(DEPTH, 6 * D_MODEL // tn),
        in_specs=[
            pl.BlockSpec((8, D_MODEL), lambda l, j: (0, 0)),
            pl.BlockSpec((1, D_MODEL, tn), lambda l, j: (l, 0, j)),
            pl.BlockSpec((1, 1, tn), lambda l, j: (l, 0, j)),
        ],
        out_specs=pl.BlockSpec((1, 8, tn), lambda l, j: (l, 0, j)),
        out_shape=jax.ShapeDtypeStruct((DEPTH, 8, 6 * D_MODEL), F32),
        compiler_params=_cparams(("parallel", "parallel")),
        name="adaln",
    )(cv8, w_ada, b_ada.reshape(DEPTH, 1, 6 * D_MODEL))


def _proj_kernel(*refs, splits, cache_cols, n_prev, slot, tm):
    x_ref, g_ref, sh_ref, sc_ref, w_ref = refs[:5]
    o_refs = refs[5 + n_prev:5 + n_prev + len(splits)]
    c_refs = refs[5 + n_prev + len(splits):]
    i = pl.program_id(0)
    h = _norm_mod(x_ref[...], g_ref[...], sh_ref[0], sc_ref[0]).astype(BF16)
    off = 0
    ys = []
    for o_ref, n in zip(o_refs, splits):
        y = jnp.dot(h, w_ref[:, off:off + n], preferred_element_type=F32)
        o_ref[...] = y
        ys.append(y)
        off += n

    @pl.when(i < N_CTX_ROWS // tm)
    def _():
        for c_ref, col in zip(c_refs, cache_cols):
            for s in range(tm // SEQ):
                c_ref[s, 0] = ys[col][s * SEQ:(s + 1) * SEQ]
                if slot == 0:
                    c_ref[s, 1] = jnp.zeros((SEQ, splits[col]), F32)


def _norm_proj(x, g, shift, scale, w_bf16, splits, name, cache_cols, slot, prev_caches):
    tm = 512
    n = w_bf16.shape[1]
    n_ctx_tiles = N_CTX_ROWS // tm
    seq_per_tile = tm // SEQ
    n_prev = len(prev_caches)
    n_slots = 2
    if slot == 0:
        cache_specs = [pl.BlockSpec((seq_per_tile, n_slots, SEQ, splits[c]),
                                    lambda i: (jnp.minimum(i, n_ctx_tiles - 1), 0, 0, 0)) for c in cache_cols]
    else:
        cache_specs = [pl.BlockSpec((seq_per_tile, 1, SEQ, splits[c]),
                                    lambda i: (jnp.minimum(i, n_ctx_tiles - 1), slot, 0, 0)) for c in cache_cols]
    outs = pl.pallas_call(
        functools.partial(_proj_kernel, splits=splits, cache_cols=tuple(cache_cols), n_prev=n_prev,
                          slot=slot, tm=tm),
        grid=(M_ALL // tm,),
        in_specs=[
            pl.BlockSpec((tm, D_MODEL), lambda i: (i, 0)),
            _full_spec((1, D_MODEL)),
            _mod_spec(tm),
            _mod_spec(tm),
            _full_spec((D_MODEL, n)),
        ] + [pl.BlockSpec(memory_space=pl.ANY)] * n_prev,
        out_specs=[pl.BlockSpec((tm, s), lambda i: (i, 0)) for s in splits] + cache_specs,
        out_shape=[jax.ShapeDtypeStruct((M_ALL, s), F32) for s in splits]
        + [jax.ShapeDtypeStruct((BATCH, n_slots, SEQ, splits[c]), F32) for c in cache_cols],
        input_output_aliases={5 + j: len(splits) + j for j in range(n_prev)},
        compiler_params=_cparams(("arbitrary",)),
        name=name,
    )(x, g.reshape(1, D_MODEL), shift, scale, w_bf16, *prev_caches)
    return outs[:len(splits)], outs[len(splits):]


def _part_specs(tm, ncols):
    n_ctx_tiles = N_CTX_ROWS // tm
    return [pl.BlockSpec((tm, ncols), lambda i, *_: (jnp.minimum(i, n_ctx_tiles - 1), 0)),
            pl.BlockSpec((tm, ncols), lambda i, *_: (jnp.maximum(i - n_ctx_tiles, 0), 0))]


def _pick_part(i, tm, ctx_ref, lat_ref):
    return jnp.where(i < N_CTX_ROWS // tm, ctx_ref[...], lat_ref[...])


def _resident_spec(shape):
    n = len(shape)
    return pl.BlockSpec(shape, lambda *_: (0,) * n, pipeline_mode=pl.Buffered(1))


def _mix_ffn_kernel(x_ref, gmix_ref, *refs, split, tm):
    i = pl.program_id(0)
    n_act = sum(2 if s else 1 for s in split)
    a_refs, wo_refs = refs[:n_act], refs[n_act:n_act + len(split)]
    g_ref, sh_ref, sc_ref, gate_ref, w1_ref, w3_ref, w2_ref, o_ref = refs[n_act + len(split):]
    acc = None
    pos = 0
    for is_split, w_ref in zip(split, wo_refs):
        if is_split:
            a = _pick_part(i, tm, a_refs[pos], a_refs[pos + 1])
            pos += 2
        else:
            a = a_refs[pos][...]
            pos += 1
        t = jnp.dot(a, w_ref[...], preferred_element_type=F32)
        acc = t if acc is None else acc + t
    x = x_ref[...] + gmix_ref[0] * acc
    h = _norm_mod(x, g_ref[...], sh_ref[0], sc_ref[0]).astype(BF16)
    a = jnp.dot(h, w1_ref[...], preferred_element_type=F32)
    b = jnp.dot(h, w3_ref[...], preferred_element_type=F32)
    gated = ((a * _sigmoid(a)) * b).astype(BF16)
    o_ref[...] = x + gate_ref[0] * jnp.dot(gated, w2_ref[...], preferred_element_type=F32)


def _mix_ffn(x, gate_mix, acts, w_outs, g, shift, scale, gate, w1, w3, w2, name):
    tm = 512
    in_specs = [pl.BlockSpec((tm, D_MODEL), lambda i: (i, 0)), _mod_spec(tm)]
    flat, split = [], []
    for a in acts:
        if isinstance(a, tuple):
            in_specs += _part_specs(tm, a[0].shape[1])
            flat += list(a)
            split.append(True)
        else:
            in_specs.append(pl.BlockSpec((tm, a.shape[1]), lambda i: (i, 0)))
            flat.append(a)
            split.append(False)
    in_specs += [_resident_spec(w.shape) for w in w_outs]
    in_specs += [
        _full_spec((1, D_MODEL)),
        _mod_spec(tm),
        _mod_spec(tm),
        _mod_spec(tm),
        _resident_spec((D_MODEL, D_FF)),
        _resident_spec((D_MODEL, D_FF)),
        _resident_spec((D_FF, D_MODEL)),
    ]
    return pl.pallas_call(
        functools.partial(_mix_ffn_kernel, split=tuple(split), tm=tm),
        grid=(M_ALL // tm,),
        in_specs=in_specs,
        out_specs=pl.BlockSpec((tm, D_MODEL), lambda i: (i, 0)),
        out_shape=jax.ShapeDtypeStruct((M_ALL, D_MODEL), F32),
        compiler_params=_cparams(("parallel",)),
        name=name,
    )(x, gate_mix, *flat, *w_outs, g.reshape(1, D_MODEL), shift, scale, gate, w1, w3, w2)


def _final_norm_kernel(x_ref, g_ref, oc_ref, ol_ref, *, tm):
    i = pl.program_id(0)
    x = x_ref[...]
    r = lax.rsqrt(jnp.mean(x * x, axis=-1, keepdims=True) + NORM_EPS)
    y = (x * r) * g_ref[...]

    @pl.when(i < N_CTX_ROWS // tm)
    def _():
        oc_ref[...] = y

    @pl.when(i >= N_CTX_ROWS // tm)
    def _():
        ol_ref[...] = y


def _final_norm(x, g):
    tm = 1024
    return pl.pallas_call(
        functools.partial(_final_norm_kernel, tm=tm),
        grid=(M_ALL // tm,),
        in_specs=[pl.BlockSpec((tm, D_MODEL), lambda i: (i, 0)), _full_spec((1, D_MODEL))],
        out_specs=_part_specs(tm, D_MODEL),
        out_shape=[jax.ShapeDtypeStruct((N_CTX_ROWS, D_MODEL), F32),
                   jax.ShapeDtypeStruct((N_LAT_ROWS, D_MODEL), F32)],
        compiler_params=_cparams(("arbitrary",)),
        name="final_norm",
    )(x, g.reshape(1, D_MODEL))


def _half_masks():
    lane = lax.broadcasted_iota(jnp.int32, (1, LANES), 1)
    return (lane < HEAD_DIM, lane >= HEAD_DIM)


def _na_ctx_kernel(q_ref, k_ref, v_ref, o_ref):
    masks = _half_masks()
    work = []
    for p in range(D_NA // LANES):
        sl = slice(p * LANES, (p + 1) * LANES)
        q = q_ref[:, sl] * (QK_SCALE * LOG2_E)
        k = k_ref[:, sl].astype(BF16)
        v = v_ref[:, sl]
        for m in masks:
            work.append(dict(s=_bdot_nt(jnp.where(m, q, 0.0), k), vm=jnp.where(m, v, 0.0).astype(BF16)))
    for w in work:
        e = jnp.exp2(w["s"] - jnp.max(w["s"], axis=-1, keepdims=True))
        w["inv_l"] = 1.0 / jnp.sum(e, axis=-1, keepdims=True)
        w["e"] = e.astype(BF16)
    for w in work:
        w["o"] = jnp.dot(w["e"], w["vm"], preferred_element_type=F32) * w["inv_l"]
    for p in range(D_NA // LANES):
        o_ref[:, p * LANES:(p + 1) * LANES] = (work[2 * p]["o"] + work[2 * p + 1]["o"]).astype(BF16)


def _na_ctx(q, k, v):
    spec = pl.BlockSpec((SEQ, D_NA), lambda b: (b, 0))
    return pl.pallas_call(
        _na_ctx_kernel,
        grid=(BATCH,),
        in_specs=[spec, spec, spec],
        out_specs=spec,
        out_shape=jax.ShapeDtypeStruct((N_CTX_ROWS, D_NA), BF16),
        compiler_params=_cparams(("parallel",)),
        name="na_ctx",
    )(q, k, v)


def _na_lat_kernel(q_ref, k_ref, v_ref, kc_ref, vc_ref, bias_ref, o_ref, *, rows_per_step):
    n_loc = NA_KH * GRID_W
    masks = _half_masks()
    kc = kc_ref[0].astype(BF16)
    vc = vc_ref[0]
    vcm = [jnp.where(m, vc, 0.0).astype(BF16) for m in masks]
    work = []
    for rr in range(rows_per_step):
        i = pl.program_id(2) * rows_per_step + rr
        r0 = jnp.clip(i - NA_KH // 2, 0, GRID_ROWS - NA_KH)
        start = pl.multiple_of(r0 * GRID_W, GRID_W)
        q = q_ref[rr * GRID_W:(rr + 1) * GRID_W, :] * QK_SCALE
        kl = k_ref[pl.ds(start, n_loc), :].astype(BF16)
        vl = v_ref[pl.ds(start, n_loc), :]
        for hh, m in enumerate(masks):
            qm = jnp.where(m, q, 0.0).astype(BF16)
            work.append(dict(rr=rr, hh=hh, vlm=jnp.where(m, vl, 0.0).astype(BF16),
                             sl=_bdot_nt(qm, kl) + bias_ref[r0 - i + NA_KH - 1, hh], sc=_bdot_nt(qm, kc)))
    for w in work:
        mx = jnp.maximum(jnp.max(w["sl"], axis=-1, keepdims=True), jnp.max(w["sc"], axis=-1, keepdims=True))
        w["el"] = jnp.exp(w["sl"] - mx)
        w["ec"] = jnp.exp(w["sc"] - mx)
    for w in work:
        inv_l = 1.0 / (jnp.sum(w["el"], axis=-1, keepdims=True) + jnp.sum(w["ec"], axis=-1, keepdims=True))
        w["o"] = (_bdot(w["el"], w["vlm"]) + _bdot(w["ec"], vcm[w["hh"]])) * inv_l
    for rr in range(rows_per_step):
        a, b = [w["o"] for w in work if w["rr"] == rr]
        o_ref[rr * GRID_W:(rr + 1) * GRID_W, :] = (a + b).astype(BF16)


def _na_bias_table(rpb):
    jj = jnp.arange(GRID_W)[:, None]
    cc = jnp.arange(GRID_W)[None, :]
    c0 = jnp.clip(jj - NA_KW // 2, 0, GRID_W - NA_KW)
    inwin = (cc >= c0) & (cc < c0 + NA_KW)
    idx = jnp.clip(cc - jj + NA_KW - 1, 0, 2 * NA_KW - 2)
    full = rpb[:, :, idx]
    full = jnp.where(inwin[None, None], full, NEG_BIG)
    tabs = []
    for v in range(NA_KH):
        t = full[:, v:v + NA_KH]
        tabs.append(t.transpose(0, 2, 1, 3).reshape(N_HEADS, GRID_W, NA_KH * GRID_W))
    return jnp.stack(tabs, axis=0)


def _na_lat(q, k, v, kc, vc, bias_tab):
    rows_per_step = 4
    tq = rows_per_step * GRID_W
    steps = GRID_ROWS // rows_per_step
    lat_blk = N_CTX_ROWS // DEC_SEQ
    row_blk = N_CTX_ROWS // tq
    return pl.pallas_call(
        functools.partial(_na_lat_kernel, rows_per_step=rows_per_step),
        grid=(DEC_BATCH, D_NA // LANES, steps),
        in_specs=[
            pl.BlockSpec((tq, LANES), lambda b, p, i: (row_blk + b * steps + i, p)),
            pl.BlockSpec((DEC_SEQ, LANES), lambda b, p, i: (lat_blk + b, p)),
            pl.BlockSpec((DEC_SEQ, LANES), lambda b, p, i: (lat_blk + b, p)),
            pl.BlockSpec((1, PAST_LEN, LANES), lambda b, p, i: (b, 0, p)),
            pl.BlockSpec((1, PAST_LEN, LANES), lambda b, p, i: (b, 0, p)),
            pl.BlockSpec((NA_KH, 2, GRID_W, NA_KH * GRID_W), lambda b, p, i: (0, p, 0, 0)),
        ],
        out_specs=pl.BlockSpec((tq, LANES), lambda b, p, i: (b * steps + i, p)),
        out_shape=jax.ShapeDtypeStruct((N_LAT_ROWS, D_NA), BF16),
        compiler_params=_cparams(("parallel", "parallel", "arbitrary")),
        name="na_lat",
    )(q, k, v, kc, vc, bias_tab)


def _lam_value(lq_ref, lk_ref, lam_init):
    s = jnp.sum(lq_ref[...] * lk_ref[...], axis=-1, keepdims=True)
    return jnp.exp(s[0:1]) - jnp.exp(s[1:2]) + lam_init


def _subln(o, sub, lam_init):
    r = lax.rsqrt(jnp.mean(o * o, axis=-1, keepdims=True) + NORM_EPS)
    return ((o * r) * sub) * (1.0 - lam_init)


def _diff_ctx_kernel(lq_ref, lk_ref, sub_ref, q_ref, k_ref, v_ref, o_ref, *, lam_init):
    lam = _lam_value(lq_ref, lk_ref, lam_init)
    masks = _half_masks()
    work = []
    for h in range(N_HEADS):
        sl = slice(h * LANES, (h + 1) * LANES)
        q = q_ref[:, sl] * (QK_SCALE * LOG2_E)
        k = k_ref[:, sl].astype(BF16)
        for m in masks:
            work.append(dict(s=_bdot_nt(jnp.where(m, q, 0.0), k)))
    for w in work:
        e = jnp.exp2(w["s"] - jnp.max(w["s"], axis=-1, keepdims=True))
        w["p"] = e * (1.0 / jnp.sum(e, axis=-1, keepdims=True))
    outs = []
    for h in range(N_HEADS):
        att = work[2 * h]["p"] - lam * work[2 * h + 1]["p"]
        outs.append(_bdot(att, v_ref[:, h * LANES:(h + 1) * LANES]))
    for h in range(N_HEADS):
        o_ref[:, h * LANES:(h + 1) * LANES] = _subln(outs[h], sub_ref[...], lam_init).astype(BF16)


def _diff_ctx(q, k, v, lam_q, lam_k, subln, lam_init):
    spec = pl.BlockSpec((SEQ, D_DIFF), lambda b: (b, 0))
    return pl.pallas_call(
        functools.partial(_diff_ctx_kernel, lam_init=lam_init),
        grid=(BATCH,),
        in_specs=[_full_spec((2, HEAD_DIM)), _full_spec((2, HEAD_DIM)), _full_spec((1, LANES)),
                  spec, spec, spec],
        out_specs=spec,
        out_shape=jax.ShapeDtypeStruct((N_CTX_ROWS, D_DIFF), BF16),
        compiler_params=_cparams(("parallel",)),
        name="diff_ctx",
    )(lam_q, lam_k, subln.reshape(1, LANES), q, k, v)


def _proj_diff_kernel(*refs, n_prev, slot, tm):
    x_ref, g_ref, sh_ref, sc_ref, w_ref, cos_ref, sin_ref = refs[:7]
    q_ref, k_ref, v_ref, kc_ref, vc_ref, qr_ref, kr_ref, vt_ref = refs[7 + n_prev:]
    i = pl.program_id(0)
    n_ctx_tiles = N_CTX_ROWS // tm

    def project():
        h = _norm_mod(x_ref[...], g_ref[...], sh_ref[0], sc_ref[0]).astype(BF16)
        return [jnp.dot(h, w_ref[:, j * D_DIFF:(j + 1) * D_DIFF], preferred_element_type=F32) for j in range(3)]

    @pl.when(i < n_ctx_tiles)
    def _():
        ys = project()
        for o_ref, y in zip((q_ref, k_ref, v_ref), ys):
            o_ref[...] = y
        for c_ref, y in ((kc_ref, ys[1]), (vc_ref, ys[2])):
            for s in range(tm // SEQ):
                c_ref[s, 0] = y[s * SEQ:(s + 1) * SEQ]
                if slot == 0:
                    c_ref[s, 1] = jnp.zeros((SEQ, D_DIFF), F32)

    @pl.when(i >= n_ctx_tiles)
    def _():
        ys = project()
        cos = cos_ref[...]
        sin = sin_ref[...]
        lane = lax.broadcasted_iota(jnp.int32, (1, LANES), 1)
        first = (lane % (2 * ROPE_F)) < ROPE_F
        for y, o_ref, scale in ((ys[0], qr_ref, QK_SCALE * LOG2_E), (ys[1], kr_ref, None)):
            for j in range(D_DIFF // LANES):
                x = y[:, j * LANES:(j + 1) * LANES]
                partner = jnp.where(first, pltpu.roll(x, LANES - ROPE_F, 1), pltpu.roll(x, ROPE_F, 1))
                r = x * cos + partner * sin
                if scale is not None:
                    r = r * scale
                o_ref[:, j * LANES:(j + 1) * LANES] = r.astype(BF16)
        vt_ref[0] = ys[2].T.astype(BF16)


def _rope_tables():
    t = jnp.arange(DEC_SEQ)
    pos = jnp.stack([t // GRID_W, t % GRID_W], -1).astype(F32)
    inv = ROPE_BASE ** (-jnp.arange(ROPE_F, dtype=F32) / ROPE_F)
    ang = pos[:, :, None] * inv
    cos, sin = jnp.cos(ang), jnp.sin(ang)
    cos64 = jnp.concatenate([cos, cos], axis=-1).reshape(DEC_SEQ, HEAD_DIM)
    sin64 = jnp.concatenate([-sin, sin], axis=-1).reshape(DEC_SEQ, HEAD_DIM)
    return jnp.tile(cos64, (1, 2)), jnp.tile(sin64, (1, 2))


def _proj_diff(x, g, shift, scale, w_bf16, cos_t, sin_t, slot, prev_caches):
    tm = 512
    n_ctx_tiles = N_CTX_ROWS // tm
    per_seq = DEC_SEQ // tm
    seq_per_tile = tm // SEQ
    n_prev = len(prev_caches)

    def lat(i):
        return jnp.maximum(i - n_ctx_tiles, 0)

    ctx_spec, lat_spec = _part_specs(tm, D_DIFF)
    tab_spec = pl.BlockSpec((tm, LANES), lambda i: (lat(i) % per_seq, 0))
    if slot == 0:
        cache_spec = pl.BlockSpec((seq_per_tile, 2, SEQ, D_DIFF),
                                  lambda i: (jnp.minimum(i, n_ctx_tiles - 1), 0, 0, 0))
    else:
        cache_spec = pl.BlockSpec((seq_per_tile, 1, SEQ, D_DIFF),
                                  lambda i: (jnp.minimum(i, n_ctx_tiles - 1), slot, 0, 0))
    ctx_shape = jax.ShapeDtypeStruct((N_CTX_ROWS, D_DIFF), F32)
    cache_shape = jax.ShapeDtypeStruct((BATCH, 2, SEQ, D_DIFF), F32)
    lat_shape = jax.ShapeDtypeStruct((N_LAT_ROWS, D_DIFF), BF16)
    outs = pl.pallas_call(
        functools.partial(_proj_diff_kernel, n_prev=n_prev, slot=slot, tm=tm),
        grid=(M_ALL // tm,),
        in_specs=[
            pl.BlockSpec((tm, D_MODEL), lambda i: (i, 0)),
            _full_spec((1, D_MODEL)),
            _mod_spec(tm),
            _mod_spec(tm),
            _resident_spec((D_MODEL, 3 * D_DIFF)),
            tab_spec,
            tab_spec,
        ] + [pl.BlockSpec(memory_space=pl.ANY)] * n_prev,
        out_specs=[ctx_spec, ctx_spec, ctx_spec, cache_spec, cache_spec, lat_spec, lat_spec,
                   pl.BlockSpec((1, D_DIFF, tm), lambda i: (lat(i) // per_seq, 0, lat(i) % per_seq))],
        out_shape=[ctx_shape, ctx_shape, ctx_shape, cache_shape, cache_shape, lat_shape, lat_shape,
                   jax.ShapeDtypeStruct((DEC_BATCH, D_DIFF, DEC_SEQ), BF16)],
        input_output_aliases={7 + j: 3 + j for j in range(n_prev)},
        compiler_params=_cparams(("arbitrary",)),
        name="proj_odd",
    )(x, g.reshape(1, D_MODEL), shift, scale, w_bf16, cos_t, sin_t, *prev_caches)
    return outs[:3], outs[3:5], outs[5:]


def _diff_lat_kernel(lq_ref, lk_ref, sub_ref, q_ref, kl_ref, vlt_ref, kc_ref, vct_ref, o_ref, *,
                     lam_init, sub_q, n_sub):
    lam = _lam_value(lq_ref, lk_ref, lam_init)
    kl = kl_ref[...]
    kc = kc_ref[0].astype(BF16)
    vlt = vlt_ref[0]
    vct = vct_ref[0].astype(BF16)
    work = []
    for t in range(n_sub):
        q = q_ref[t * sub_q:(t + 1) * sub_q, :]
        for m in _half_masks():
            qm = jnp.where(m, q, jnp.zeros_like(q))
            work.append(dict(sl=_bdot_nt(kl, qm), sc=_bdot_nt(kc, qm)))
    for w in work:
        mx = jnp.maximum(jnp.max(w["sl"], axis=0, keepdims=True), jnp.max(w["sc"], axis=0, keepdims=True))
        el = jnp.exp2(w["sl"] - mx)
        ec = jnp.exp2(w["sc"] - mx)
        w["inv_l"] = 1.0 / (jnp.sum(el, axis=0, keepdims=True) + jnp.sum(ec, axis=0, keepdims=True))
        w["el"] = el.astype(BF16)
        w["ec"] = ec.astype(BF16)
    for w in work:
        w["o"] = (jnp.dot(vlt, w["el"], preferred_element_type=F32)
                  + jnp.dot(vct, w["ec"], preferred_element_type=F32)) * w["inv_l"]
    for t in range(n_sub):
        ot = work[2 * t]["o"] - lam * work[2 * t + 1]["o"]
        r = lax.rsqrt(jnp.mean(ot * ot, axis=0, keepdims=True) + NORM_EPS)
        on = ((ot * r) * sub_ref[...]) * (1.0 - lam_init)
        o_ref[t * sub_q:(t + 1) * sub_q, :] = on.T.astype(BF16)


def _diff_lat(q_r, k_r, v_t, kc, vc_t, lam_q, lam_k, subln, lam_init):
    sub_q, n_sub = 256, 2
    tq = sub_q * n_sub
    per_seq = DEC_SEQ // tq
    return pl.pallas_call(
        functools.partial(_diff_lat_kernel, lam_init=lam_init, sub_q=sub_q, n_sub=n_sub),
        grid=(DEC_BATCH, N_HEADS, per_seq),
        in_specs=[
            _full_spec((2, HEAD_DIM)),
            _full_spec((2, HEAD_DIM)),
            _full_spec((LANES, 1)),
            pl.BlockSpec((tq, LANES), lambda b, h, i: (b * per_seq + i, h)),
            pl.BlockSpec((DEC_SEQ, LANES), lambda b, h, i: (b, h)),
            pl.BlockSpec((1, LANES, DEC_SEQ), lambda b, h, i: (b, h, 0)),
            pl.BlockSpec((1, PAST_LEN, LANES), lambda b, h, i: (b, 0, h)),
            pl.BlockSpec((1, LANES, PAST_LEN), lambda b, h, i: (b, h, 0)),
        ],
        out_specs=pl.BlockSpec((tq, LANES), lambda b, h, i: (b * per_seq + i, h)),
        out_shape=jax.ShapeDtypeStruct((N_LAT_ROWS, D_DIFF), BF16),
        compiler_params=_cparams(("parallel", "parallel", "arbitrary")),
        name="diff_lat",
    )(lam_q, lam_k, subln.reshape(LANES, 1), q_r, k_r, v_t, kc, vc_t)


def _token_shift(u, prev_row, next_row, mu):
    n = u.shape[0]
    rows = lax.broadcasted_iota(jnp.int32, (n, 1), 0)
    up = jnp.where(rows == 0, prev_row, pltpu.roll(u, 1, 0))
    un = jnp.where(rows == n - 1, next_row, pltpu.roll(u, n - 1, 0))
    return u + mu[0:1] * (up - u) + mu[1:2] * (un - u)


def _icl_rate_and_key(us, a0, a2p, k_a):
    lo = us[:, 3 * D_RWKV:3 * D_RWKV + LANES]
    k = us[:, D_RWKV:2 * D_RWKV]
    a = _sigmoid(a0 + _bdot(lo, a2p))
    return a, k * (1.0 + (a - 1.0) * k_a)


def _stack_blockdiag(x):
    shape = (2 * x.shape[0], x.shape[1])
    row = lax.broadcasted_iota(jnp.int32, shape, 0)
    col = lax.broadcasted_iota(jnp.int32, shape, 1)
    keep = (row // HEAD_DIM) == ((col % LANES) // HEAD_DIM)
    return jnp.where(keep, jnp.concatenate([x, x], axis=0), 0.0).astype(BF16)


def _rwkv_chunk_inputs(u, prev_row, next_row, reverse, mu, w0, w2p, a0, a2p, k_k, k_a, bd_ones):
    us = _token_shift(u, prev_row, next_row, mu)
    r = us[:, 0:D_RWKV]
    k = us[:, D_RWKV:2 * D_RWKV]
    v = us[:, 2 * D_RWKV:3 * D_RWKV]
    lo = us[:, 3 * D_RWKV:3 * D_RWKV + LANES]
    w_raw = w0 + _bdot(jnp.tanh(lo), w2p)
    logw = -jnp.exp(-_softplus(-w_raw) - 0.5)
    a, kt = _icl_rate_and_key(us, a0, a2p, k_a)
    kk_raw = k * k_k
    ss = _head_sums(kk_raw * kk_raw, bd_ones)
    kk = kk_raw * lax.rsqrt(jnp.maximum(ss, 1e-12))
    row = lax.broadcasted_iota(jnp.int32, (CHUNK, CHUNK), 0)
    col = lax.broadcasted_iota(jnp.int32, (CHUNK, CHUNK), 1)
    seen = (col >= row) if reverse else (col <= row)
    cum = _dot3_lhs_exact(seen.astype(F32).astype(BF16), logw)
    p_inv = jnp.exp(-cum)
    tot = cum[0:1, :] if reverse else cum[CHUNK - 1:CHUNK, :]
    return dict(a_hat=jnp.exp(cum - logw) * kk, b_hat=(kk * a) * p_inv, k_hat=kt * p_inv,
                r_hat=r * jnp.exp(cum), v=v, p_tot=jnp.exp(tot))


def _rwkv_scan_kernel(*refs, n_chunks, n_par, has_state):
    u_refs = (refs[0:3], refs[3:6])
    pos = 6
    s0_ref = None
    if has_state:
        s0_ref = refs[pos]
        pos += 1
    mu_ref, w0_ref, w2_ref, a0_ref, a2_ref, kk_ref, ka_ref, bd_ref = refs[pos:pos + 8]
    y_refs = refs[pos + 8:pos + 10]
    sfin_ref, s_ref = refs[pos + 10], refs[pos + 11]
    c = pl.program_id(1)
    n_pairs = D_RWKV // LANES

    @pl.when(c == 0)
    def _():
        if has_state:
            s_ref[...] = s0_ref[...]
        else:
            s_ref[...] = jnp.zeros(s_ref.shape, F32)

    at_start = jnp.where(c == 0, 0.0, 1.0)
    at_end = jnp.where(c == n_chunks - 1, 0.0, 1.0)
    has_prev = (at_start, at_end)
    has_next = (at_end, at_start)

    row = lax.broadcasted_iota(jnp.int32, (CHUNK, LANES), 0)
    colm = lax.broadcasted_iota(jnp.int32, (CHUNK, LANES), 1) % CHUNK
    strict = (colm < row, colm > row)
    incl = (colm <= row, colm >= row)
    same_blk = (row // INV_BLOCK) == (colm // INV_BLOCK)
    eye = (row == colm).astype(F32)
    prow = lax.broadcasted_iota(jnp.int32, (LANES, LANES), 0)
    pcol = lax.broadcasted_iota(jnp.int32, (LANES, LANES), 1)
    blockdiag = (prow // HEAD_DIM) == (pcol // HEAD_DIM)
    eye_pair = (prow == pcol).astype(F32)

    def stream_chains(j, d):
        u_ref, up_ref, un_ref = u_refs[d]
        q = _rwkv_chunk_inputs(
            u_ref[j], up_ref[j, 7:8, :] * has_prev[d], un_ref[j, 0:1, :] * has_next[d], d == 1,
            mu_ref[...], w0_ref[d], w2_ref[d], a0_ref[d], a2_ref[d], kk_ref[...], ka_ref[...], bd_ref[...])
        chains = []
        for p in range(n_pairs):
            sl = slice(p * LANES, (p + 1) * LANES)
            chains.append(dict(j=j, d=d, p=p, sl=sl, ah=q["a_hat"][:, sl], bh=q["b_hat"][:, sl],
                               kh=q["k_hat"][:, sl], rh=q["r_hat"][:, sl], vv=q["v"][:, sl],
                               pt=q["p_tot"][:, sl]))
        return chains

    def s_scores(ch):
        ar = jnp.concatenate([ch["ah"], ch["rh"]], axis=0)
        x = _bdot_nt(ar, jnp.concatenate([_stack_blockdiag(ch["bh"]), _stack_blockdiag(ch["kh"])], axis=0))
        xb, xk = x[:, :LANES], x[:, LANES:]
        d = ch["d"]
        l_ab = jnp.where(strict[d], xb[:CHUNK], 0.0)
        ch["l_ak"] = jnp.where(strict[d], xk[:CHUNK], 0.0)
        ch["m_rb"] = jnp.where(incl[d], xb[CHUNK:], 0.0)
        ch["m_rk"] = jnp.where(incl[d], xk[CHUNK:], 0.0)
        ch["x1"] = -jnp.where(same_blk, l_ab, 0.0)
        ch["l_off"] = jnp.where(same_blk, 0.0, l_ab)
        ch["vs"] = _stack_blockdiag(ch["vv"])

    def s_x2(ch):
        ch["x2"] = _bdot(ch["x1"], _stack_blockdiag(ch["x1"]))
        ch["x2s"] = _stack_blockdiag(ch["x2"])
        ch["u"] = _bdot(ch["l_ak"], ch["vs"])

    def s_x4(ch):
        td = eye + ch["x1"]
        both = _bdot(jnp.concatenate([ch["x2"], td], axis=0), ch["x2s"])
        ch["x4"] = both[:CHUNK]
        ch["td"] = td + both[CHUNK:]

    def s_x8(ch):
        both = _bdot(jnp.concatenate([ch["x4"], ch["td"]], axis=0), _stack_blockdiag(ch["x4"]))
        ch["x8s"] = _stack_blockdiag(both[:CHUNK])
        ch["td"] = ch["td"] + both[CHUNK:]

    def s_td(ch):
        ch["td"] = ch["td"] + _bdot(ch["td"], ch["x8s"])

    def s_mm(ch):
        ch["mm"] = _bdot(ch["td"], _stack_blockdiag(ch["l_off"]))

    def s_m2(ch):
        ch["m2s"] = _stack_blockdiag(_bdot(ch["mm"], _stack_blockdiag(ch["mm"])))

    def s_n2(ch):
        n1 = eye - ch["mm"]
        ch["n2"] = n1 + _bdot(n1, ch["m2s"])

    def s_tinv(ch):
        ch["t_inv"] = _bdot(ch["n2"], _stack_blockdiag(ch["td"]))

    def s_tx(ch):
        ch["tx"] = _bdot(ch["t_inv"], _stack_blockdiag(jnp.concatenate([ch["ah"], ch["u"]], axis=1)))

    def s_local(ch):
        tx = ch["tx"]
        mx = _bdot(ch["m_rb"], _stack_blockdiag(tx))
        ch["q_eff"] = ch["rh"] - mx[:, :LANES]
        ch["y_loc"] = _bdot(ch["m_rk"], ch["vs"]) - mx[:, LANES:]
        ch["g"] = jnp.where(blockdiag, eye_pair - _bdot_tn(tx[:, :LANES], ch["bh"]), 0.0) * ch["pt"]
        ch["h"] = jnp.where(
            blockdiag,
            _bdot_tn(jnp.concatenate([ch["vv"], -tx[:, LANES:]], axis=0),
                     jnp.concatenate([ch["kh"], ch["bh"]], axis=0)),
            0.0) * ch["pt"]

    def s_state(ch):
        j, d, p = ch["j"], ch["d"], ch["p"]
        s_old = s_ref[j, d, p]
        y_refs[d][j, :, ch["sl"]] = _bdot_nt(ch["q_eff"], s_old) + ch["y_loc"]
        s_ref[j, d, p] = _bdot(s_old, ch["g"]) + ch["h"]

    chains = [ch for j in range(n_par) for d in range(2) for ch in stream_chains(j, d)]
    for stage in (s_scores, s_x2, s_x4, s_x8, s_td, s_mm, s_m2, s_n2, s_tinv, s_tx, s_local, s_state):
        for ch in chains:
            stage(ch)

    @pl.when(c == n_chunks - 1)
    def _():
        sfin_ref[...] = s_ref[...]


def _rwkv_scan(u, s0p, prm, row_base, n_seq, seq_len):
    n_par = 2
    n_chunks = seq_len // CHUNK
    n_pairs = D_RWKV // LANES
    base_blk = row_base // (seq_len * n_par)
    n_blk8 = seq_len // 8
    has_state = s0p is not None
    u3 = u.reshape(M_ALL // seq_len, seq_len, P_RWKV)

    def chunk_pos(d, c):
        return c if d == 0 else n_chunks - 1 - c

    in_specs, args = [], []
    for d in range(2):
        in_specs += [
            pl.BlockSpec((n_par, CHUNK, P_RWKV), lambda g, c, d=d: (base_blk + g, chunk_pos(d, c), 0)),
            pl.BlockSpec((n_par, 8, P_RWKV),
                         lambda g, c, d=d: (base_blk + g, jnp.maximum(chunk_pos(d, c) * (CHUNK // 8) - 1, 0), 0)),
            pl.BlockSpec((n_par, 8, P_RWKV),
                         lambda g, c, d=d: (base_blk + g,
                                            jnp.minimum((chunk_pos(d, c) + 1) * (CHUNK // 8), n_blk8 - 1), 0)),
        ]
        args += [u3, u3, u3]
    state_spec = pl.BlockSpec((n_par, 2, n_pairs, LANES, LANES), lambda g, c: (g, 0, 0, 0, 0))
    if has_state:
        in_specs.append(state_spec)
        args.append(s0p)
    in_specs += [
        _full_spec((2, P_RWKV)),
        _full_spec((2, 1, D_RWKV)),
        _full_spec((2, LANES, D_RWKV)),
        _full_spec((2, 1, D_RWKV)),
        _full_spec((2, LANES, D_RWKV)),
        _full_spec((1, D_RWKV)),
        _full_spec((1, D_RWKV)),
        _full_spec((MXU_WIDTH, MXU_WIDTH)),
    ]
    args += [prm["mu"], prm["w0"], prm["w2p"], prm["a0"], prm["a2p"], prm["kk"], prm["ka"], prm["bd_ones"]]
    y_shape = jax.ShapeDtypeStruct((n_seq, seq_len, D_RWKV), F32)
    y_f, y_b, s_fin = pl.pallas_call(
        functools.partial(_rwkv_scan_kernel, n_chunks=n_chunks, n_par=n_par, has_state=has_state),
        grid=(n_seq // n_par, n_chunks),
        in_specs=in_specs,
        out_specs=[
            pl.BlockSpec((n_par, CHUNK, D_RWKV), lambda g, c: (g, chunk_pos(0, c), 0)),
            pl.BlockSpec((n_par, CHUNK, D_RWKV), lambda g, c: (g, chunk_pos(1, c), 0)),
            state_spec,
        ],
        out_shape=[y_shape, y_shape, jax.ShapeDtypeStruct((n_seq, 2, n_pairs, LANES, LANES), F32)],
        scratch_shapes=[pltpu.VMEM((n_par, 2, n_pairs, LANES, LANES), F32)],
        compiler_params=_cparams(("parallel", "arbitrary")),
        name="rwkv_scan_" + ("lat" if has_state else "ctx"),
    )(*args)
    n_rows = n_seq * seq_len
    return (y_f.reshape(n_rows, D_RWKV), y_b.reshape(n_rows, D_RWKV)), s_fin


def _rwkv_fin_kernel(u_ref, up_ref, un_ref, ycf_ref, ylf_ref, ycb_ref, ylb_ref, mu_ref, a0_ref, a2_ref,
                     ka_ref, bonus_ref, g2_ref, lnw_ref, lnb_ref, bd1_ref, bdm_ref, o_ref, *, tm):
    i = pl.program_id(0)
    n_ctx_tiles = N_CTX_ROWS // tm
    per_seq = DEC_SEQ // tm
    j = (i - n_ctx_tiles) % per_seq
    is_ctx = i < n_ctx_tiles
    not_first = jnp.where(is_ctx | (j == 0), 0.0, 1.0)
    not_last = jnp.where(is_ctx | (j == per_seq - 1), 0.0, 1.0)
    us = _token_shift(u_ref[...], up_ref[7:8, :] * not_first, un_ref[0:1, :] * not_last, mu_ref[...])
    r = us[:, 0:D_RWKV]
    v = us[:, 2 * D_RWKV:3 * D_RWKV]
    g_lo = us[:, 3 * D_RWKV + LANES:]
    rk = jnp.zeros((tm, D_RWKV), F32)
    for d in range(2):
        _, kt = _icl_rate_and_key(us, a0_ref[d], a2_ref[d], ka_ref[...])
        rk = rk + r * kt * bonus_ref[d]
    bon = _head_sums(rk, bd1_ref[...]) * v
    y = _pick_part(i, tm, ycf_ref, ylf_ref) + _pick_part(i, tm, ycb_ref, ylb_ref)
    mean = _head_sums(y, bdm_ref[...])
    yc = y - mean
    var = _head_sums(yc * yc, bdm_ref[...])
    yn = (yc * lax.rsqrt(var + GN_EPS)) * lnw_ref[...] + lnb_ref[...]
    gate = _bdot(_sigmoid(g_lo), g2_ref[...])
    o_ref[...] = ((yn + bon) * gate).astype(BF16)


def _rwkv_finish(u, y_ctx, y_lat, prm):
    tm = 256
    n_blk8 = M_ALL // 8
    return pl.pallas_call(
        functools.partial(_rwkv_fin_kernel, tm=tm),
        grid=(M_ALL // tm,),
        in_specs=[
            pl.BlockSpec((tm, P_RWKV), lambda i: (i, 0)),
            pl.BlockSpec((8, P_RWKV), lambda i: (jnp.maximum(i * (tm // 8) - 1, 0), 0)),
            pl.BlockSpec((8, P_RWKV), lambda i: (jnp.minimum((i + 1) * (tm // 8), n_blk8 - 1), 0)),
            *_part_specs(tm, D_RWKV),
            *_part_specs(tm, D_RWKV),
            _full_spec((2, P_RWKV)),
            _full_spec((2, 1, D_RWKV)),
            _full_spec((2, LANES, D_RWKV)),
            _full_spec((1, D_RWKV)),
            _full_spec((2, 1, D_RWKV)),
            _full_spec((GATE_LORA, D_RWKV)),
            _full_spec((1, D_RWKV)),
            _full_spec((1, D_RWKV)),
            _full_spec((MXU_WIDTH, MXU_WIDTH)),
            _full_spec((MXU_WIDTH, MXU_WIDTH)),
        ],
        out_specs=pl.BlockSpec((tm, D_RWKV), lambda i: (i, 0)),
        out_shape=jax.ShapeDtypeStruct((M_ALL, D_RWKV), BF16),
        compiler_params=_cparams(("parallel",)),
        name="rwkv_finish",
    )(u, u, u, y_ctx[0], y_lat[0], y_ctx[1], y_lat[1], prm["mu"], prm["a0"], prm["a2p"], prm["ka"],
      prm["bonus"], prm["g2"], prm["lnw"], prm["lnb"], prm["bd_ones"], prm["bd_mean"])


def _pack_state_pairs(s):
    lead = s.shape[:-3]
    s = s.reshape(lead + (N_HEADS // 2, 2, HEAD_DIM, HEAD_DIM))
    z = jnp.zeros_like(s[..., 0, :, :])
    top = jnp.concatenate([s[..., 0, :, :], z], axis=-1)
    bot = jnp.concatenate([z, s[..., 1, :, :]], axis=-1)
    return jnp.concatenate([top, bot], axis=-2)


def _unpack_state_pairs(sp):
    lead = sp.shape[:-3]
    a = sp[..., :HEAD_DIM, :HEAD_DIM]
    b = sp[..., HEAD_DIM:, HEAD_DIM:]
    return jnp.stack([a, b], axis=-3).reshape(lead + (N_HEADS, HEAD_DIM, HEAD_DIM))


def kernel(x_prompt, x_sample, c, cache_na_k, cache_na_v, state_rwkv, cache_diff_k, cache_diff_v, c_ctx, w_ada, b_ada, norm_mix, norm_ffn, norm_final, w_in_even, w_out_even, na_rpb, rw_mu, rw_w0, rw_w2, rw_a0, rw_a2, rw_kk, rw_ka, rw_bonus, rw_g2, rw_lnw, rw_lnb, w_qkv_diff, w_out_diff, diff_lam_q, diff_lam_k, diff_subln, ffn_w1, ffn_w3, ffn_w2):
    x = jnp.concatenate([x_prompt.reshape(N_CTX_ROWS, D_MODEL), x_sample.reshape(N_LAT_ROWS, D_MODEL)], axis=0)

    cv8 = jnp.concatenate([c_ctx[None, :], c, jnp.zeros((8 - 1 - DEC_BATCH, D_MODEL), F32)], axis=0)
    mods = _ada_all(cv8, w_ada, b_ada)
    mods = mods.reshape(DEPTH, 8, 6, D_MODEL).transpose(0, 2, 1, 3)[:, :, :1 + DEC_BATCH, None, :]

    hd_idx = jnp.arange(MXU_WIDTH) // HEAD_DIM
    bd_ones = (hd_idx[:, None] == hd_idx[None, :]).astype(BF16)
    bd_mean = (bd_ones.astype(F32) / HEAD_DIM).astype(BF16)
    cos_t, sin_t = _rope_tables()

    rw_out = []
    na_caches, diff_caches = (), ()
    for l in range(DEPTH):
        md = mods[l]
        if l % 2 == 0:
            e = l // 2
            (q, k, v, u), na_caches = _norm_proj(
                x, norm_mix[l], md[0], md[1], w_in_even[e].astype(BF16), (D_NA, D_NA, D_NA, P_RWKV),
                "proj_even", cache_cols=(1, 2), slot=e, prev_caches=na_caches)
            o_ctx = _na_ctx(q, k, v)
            o_lat = _na_lat(q, k, v, cache_na_k[:, e].reshape(DEC_BATCH, PAST_LEN, D_NA),
                            cache_na_v[:, e].reshape(DEC_BATCH, PAST_LEN, D_NA), _na_bias_table(na_rpb[e]))

            zpad = jnp.zeros((2, LANES - DECAY_LORA, D_RWKV), F32)
            prm = {
                "mu": rw_mu[e],
                "w0": rw_w0[e].reshape(2, 1, D_RWKV),
                "w2p": jnp.concatenate([rw_w2[e], zpad], axis=1),
                "a0": rw_a0[e].reshape(2, 1, D_RWKV),
                "a2p": jnp.concatenate([zpad, rw_a2[e]], axis=1),
                "kk": rw_kk[e].reshape(1, D_RWKV),
                "ka": rw_ka[e].reshape(1, D_RWKV),
                "bonus": rw_bonus[e].reshape(2, 1, D_RWKV),
                "g2": rw_g2[e],
                "lnw": rw_lnw[e].reshape(1, D_RWKV),
                "lnb": rw_lnb[e].reshape(1, D_RWKV),
                "bd_ones": bd_ones,
                "bd_mean": bd_mean,
            }
            y_ctx, s_ctx = _rwkv_scan(u, None, prm, 0, BATCH, SEQ)
            y_lat, _ = _rwkv_scan(u, _pack_state_pairs(state_rwkv[:, e]), prm, N_CTX_ROWS, DEC_BATCH, DEC_SEQ)
            o_rw = _rwkv_finish(u, y_ctx, y_lat, prm)

            w_out = w_out_even[e].astype(BF16)
            mixed, w_outs = [(o_ctx, o_lat), o_rw], [w_out[:D_NA], w_out[D_NA:]]

            rw_out.append(_unpack_state_pairs(s_ctx))
        else:
            o = l // 2
            lam_init = 0.8 - 0.6 * math.exp(-0.3 * l)
            (q, k, v), diff_caches, (q_r, k_r, v_t) = _proj_diff(
                x, norm_mix[l], md[0], md[1], w_qkv_diff[o].astype(BF16), cos_t, sin_t, o, diff_caches)
            o_ctx = _diff_ctx(q, k, v, diff_lam_q[o], diff_lam_k[o], diff_subln[o], lam_init)
            o_lat = _diff_lat(q_r, k_r, v_t, cache_diff_k[:, o].reshape(DEC_BATCH, PAST_LEN, D_DIFF),
                              cache_diff_v[:, o].reshape(DEC_BATCH, PAST_LEN, D_DIFF).swapaxes(1, 2),
                              diff_lam_q[o], diff_lam_k[o], diff_subln[o], lam_init)
            mixed, w_outs = [(o_ctx, o_lat)], [w_out_diff[o].astype(BF16)]
        x = _mix_ffn(x, md[2], mixed, w_outs, norm_ffn[l], md[3], md[4], md[5], ffn_w1[l].astype(BF16),
                     ffn_w3[l].astype(BF16), ffn_w2[l].astype(BF16), "mix_ffn")

    y_ctx, y_lat = _final_norm(x, norm_final)
    y_prompt = y_ctx.reshape(BATCH, SEQ, D_MODEL)
    y_sample = y_lat.reshape(DEC_BATCH, DEC_SEQ, D_MODEL)
    na_shape = (BATCH, DEPTH // 2, SEQ, N_HEADS, HEAD_DIM)
    diff_shape = (BATCH, DEPTH // 2, SEQ, N_HEADS, 2 * HEAD_DIM)
    return (y_prompt, y_sample, na_caches[0].reshape(na_shape), na_caches[1].reshape(na_shape),
            jnp.stack(rw_out, axis=1), diff_caches[0].reshape(diff_shape), diff_caches[1].reshape(diff_shape))
```

```python
import functools
import math

import jax
import jax.numpy as jnp
from jax import lax
from jax.experimental import pallas as pl
from jax.experimental.pallas import tpu as pltpu

F32 = jnp.float32
BF16 = jnp.bfloat16

D_MODEL = 1024
BATCH = 32
SEQ = 256
DEPTH = 4
DEC_BATCH = 2
DEC_SEQ = 4096
PAST_LEN = 512
GRID_W = 64
GRID_ROWS = DEC_SEQ // GRID_W
HEAD_DIM = 64
N_HEADS = 8
D_NA = 512
D_RWKV = 512
D_DIFF = 1024
NA_KH = 8
NA_KW = 16
DECAY_LORA = 64
ICL_LORA = 64
GATE_LORA = 128
P_RWKV = 3 * D_RWKV + DECAY_LORA + ICL_LORA + GATE_LORA
P_EVEN = 3 * D_NA + P_RWKV
D_FF = 2816
ROPE_F = HEAD_DIM // 4
ROPE_BASE = 10000.0
NORM_EPS = 1e-6
GN_EPS = 64e-5
QK_SCALE = HEAD_DIM ** -0.5
LOG2_E = math.log2(math.e)

N_CTX_ROWS = BATCH * SEQ
N_LAT_ROWS = DEC_BATCH * DEC_SEQ
M_ALL = N_CTX_ROWS + N_LAT_ROWS

LANES = 128
MXU_WIDTH = 256
CHUNK = 64
INV_BLOCK = 16
NEG_BIG = -1e30
VMEM_LIMIT = 56 * 1024 * 1024


def _cparams(sem):
    return pltpu.CompilerParams(dimension_semantics=sem, vmem_limit_bytes=VMEM_LIMIT)


def _bdot(a, b):
    return jnp.dot(a.astype(BF16), b.astype(BF16), preferred_element_type=F32)


def _bdot_nt(a, b):
    return lax.dot_general(a.astype(BF16), b.astype(BF16), (((1,), (1,)), ((), ())),
                           preferred_element_type=F32)


def _bdot_tn(a, b):
    return lax.dot_general(a.astype(BF16), b.astype(BF16), (((0,), (0,)), ((), ())),
                           preferred_element_type=F32)


def _split3(x):
    hi = x.astype(BF16)
    r1 = x - hi.astype(F32)
    mid = r1.astype(BF16)
    lo = (r1 - mid.astype(F32)).astype(BF16)
    return hi, mid, lo


def _head_sums(x, bd):
    half = bd.shape[0]
    d = functools.partial(jnp.dot, preferred_element_type=F32)
    outs = []
    for j in range(x.shape[1] // half):
        xs = x[:, j * half:(j + 1) * half]
        hi = xs.astype(BF16)
        lo = (xs - hi.astype(F32)).astype(BF16)
        outs.append(d(hi, bd) + d(lo, bd))
    return jnp.concatenate(outs, axis=1)


def _dot3_lhs_exact(e, x):
    hi, mid, lo = _split3(x)
    d = functools.partial(jnp.dot, preferred_element_type=F32)
    return d(e, hi) + d(e, mid) + d(e, lo)


def _sigmoid(x):
    return 1.0 / (1.0 + jnp.exp(-x))


def _softplus(x):
    return jnp.maximum(x, 0.0) + jnp.log(1.0 + jnp.exp(-jnp.abs(x)))


def _norm_mod(x, g, shift, scale):
    r = lax.rsqrt(jnp.mean(x * x, axis=-1, keepdims=True) + NORM_EPS)
    return ((x * r) * g) * (1.0 + scale) + shift


def _group_of_tile(i, tm):
    n_ctx_tiles = N_CTX_ROWS // tm
    tiles_per_lat = DEC_SEQ // tm
    return jnp.where(i < n_ctx_tiles, 0, 1 + (i - n_ctx_tiles) // tiles_per_lat)


def _mod_spec(tm):
    return pl.BlockSpec((1, 1, D_MODEL), lambda i, *_: (_group_of_tile(i, tm), 0, 0))


def _full_spec(shape):
    n = len(shape)
    return pl.BlockSpec(shape, lambda *_: (0,) * n)


def _ada_kernel(cv_ref, w_ref, b_ref, o_ref):
    cv = cv_ref[...]
    s = cv * _sigmoid(cv)
    o_ref[0] = _bdot(s, w_ref[0]) + b_ref[0]


def _ada_all(cv8, w_ada, b_ada):
    tn = 512
    return pl.pallas_call(
        _ada_kernel,
        grid=(DEPTH, 6 * D_MODEL // tn),
        in_specs=[
            pl.BlockSpec((8, D_MODEL), lambda l, j: (0, 0)),
            pl.BlockSpec((1, D_MODEL, tn), lambda l, j: (l, 0, j)),
            pl.BlockSpec((1, 1, tn), lambda l, j: (l, 0, j)),
        ],
        out_specs=pl.BlockSpec((1, 8, tn), lambda l, j: (l, 0, j)),
        out_shape=jax.ShapeDtypeStruct((DEPTH, 8, 6 * D_MODEL), F32),
        compiler_params=_cparams(("parallel", "parallel")),
        name="adaln",
    )(cv8, w_ada, b_ada.reshape(DEPTH, 1, 6 * D_MODEL))


def _proj_kernel(*refs, splits, cache_cols, n_prev, slot, tm):
    x_ref, g_ref, sh_ref, sc_ref, w_ref = refs[:5]
    o_refs = refs[5 + n_prev:5 + n_prev + len(splits)]
    c_refs = refs[5 + n_prev + len(splits):]
    i = pl.program_id(0)
    h = _norm_mod(x_ref[...], g_ref[...], sh_ref[0], sc_ref[0]).astype(BF16)
    off = 0
    ys = []
    for o_ref, n in zip(o_refs, splits):
        y = jnp.dot(h, w_ref[:, off:off + n], preferred_element_type=F32)
        o_ref[...] = y
        ys.append(y)
        off += n

    @pl.when(i < N_CTX_ROWS // tm)
    def _():
        for c_ref, col in zip(c_refs, cache_cols):
            for s in range(tm // SEQ):
                c_ref[s, 0] = ys[col][s * SEQ:(s + 1) * SEQ]
                if slot == 0:
                    c_ref[s, 1] = jnp.zeros((SEQ, splits[col]), F32)


def _norm_proj(x, g, shift, scale, w_bf16, splits, name, cache_cols, slot, prev_caches):
    tm = 512
    n = w_bf16.shape[1]
    n_ctx_tiles = N_CTX_ROWS // tm
    seq_per_tile = tm // SEQ
    n_prev = len(prev_caches)
    n_slots = 2
    if slot == 0:
        cache_specs = [pl.BlockSpec((seq_per_tile, n_slots, SEQ, splits[c]),
                                    lambda i: (jnp.minimum(i, n_ctx_tiles - 1), 0, 0, 0)) for c in cache_cols]
    else:
        cache_specs = [pl.BlockSpec((seq_per_tile, 1, SEQ, splits[c]),
                                    lambda i: (jnp.minimum(i, n_ctx_tiles - 1), slot, 0, 0)) for c in cache_cols]
    outs = pl.pallas_call(
        functools.partial(_proj_kernel, splits=splits, cache_cols=tuple(cache_cols), n_prev=n_prev,
                          slot=slot, tm=tm),
        grid=(M_ALL // tm,),
        in_specs=[
            pl.BlockSpec((tm, D_MODEL), lambda i: (i, 0)),
            _full_spec((1, D_MODEL)),
            _mod_spec(tm),
            _mod_spec(tm),
            _full_spec((D_MODEL, n)),
        ] + [pl.BlockSpec(memory_space=pl.ANY)] * n_prev,
        out_specs=[pl.BlockSpec((tm, s), lambda i: (i, 0)) for s in splits] + cache_specs,
        out_shape=[jax.ShapeDtypeStruct((M_ALL, s), F32) for s in splits]
        + [jax.ShapeDtypeStruct((BATCH, n_slots, SEQ, splits[c]), F32) for c in cache_cols],
        input_output_aliases={5 + j: len(splits) + j for j in range(n_prev)},
        compiler_params=_cparams(("arbitrary",)),
        name=name,
    )(x, g.reshape(1, D_MODEL), shift, scale, w_bf16, *prev_caches)
    return outs[:len(splits)], outs[len(splits):]


def _part_specs(tm, ncols):
    n_ctx_tiles = N_CTX_ROWS // tm
    return [pl.BlockSpec((tm, ncols), lambda i, *_: (jnp.minimum(i, n_ctx_tiles - 1), 0)),
            pl.BlockSpec((tm, ncols), lambda i, *_: (jnp.maximum(i - n_ctx_tiles, 0), 0))]


def _pick_part(i, tm, ctx_ref, lat_ref):
    return jnp.where(i < N_CTX_ROWS // tm, ctx_ref[...], lat_ref[...])


def _resident_spec(shape):
    n = len(shape)
    return pl.BlockSpec(shape, lambda *_: (0,) * n, pipeline_mode=pl.Buffered(1))


def _mix_ffn_kernel(x_ref, gmix_ref, *refs, split, tm):
    i = pl.program_id(0)
    n_act = sum(2 if s else 1 for s in split)
    a_refs, wo_refs = refs[:n_act], refs[n_act:n_act + len(split)]
    g_ref, sh_ref, sc_ref, gate_ref, w1_ref, w3_ref, w2_ref, o_ref = refs[n_act + len(split):]
    acc = None
    pos = 0
    for is_split, w_ref in zip(split, wo_refs):
        if is_split:
            a = _pick_part(i, tm, a_refs[pos], a_refs[pos + 1])
            pos += 2
        else:
            a = a_refs[pos][...]
            pos += 1
        t = jnp.dot(a, w_ref[...], preferred_element_type=F32)
        acc = t if acc is None else acc + t
    x = x_ref[...] + gmix_ref[0] * acc
    h = _norm_mod(x, g_ref[...], sh_ref[0], sc_ref[0]).astype(BF16)
    a = jnp.dot(h, w1_ref[...], preferred_element_type=F32)
    b = jnp.dot(h, w3_ref[...], preferred_element_type=F32)
    gated = ((a * _sigmoid(a)) * b).astype(BF16)
    o_ref[...] = x + gate_ref[0] * jnp.dot(gated, w2_ref[...], preferred_element_type=F32)


def _mix_ffn(x, gate_mix, acts, w_outs, g, shift, scale, gate, w1, w3, w2, name):
    tm = 512
    in_specs = [pl.BlockSpec((tm, D_MODEL), lambda i: (i, 0)), _mod_spec(tm)]
    flat, split = [], []
    for a in acts:
        if isinstance(a, tuple):
            in_specs += _part_specs(tm, a[0].shape[1])
            flat += list(a)
            split.append(True)
        else:
            in_specs.append(pl.BlockSpec((tm, a.shape[1]), lambda i: (i, 0)))
            flat.append(a)
            split.append(False)
    in_specs += [_resident_spec(w.shape) for w in w_outs]
    in_specs += [
        _full_spec((1, D_MODEL)),
        _mod_spec(tm),
        _mod_spec(tm),
        _mod_spec(tm),
        _resident_spec((D_MODEL, D_FF)),
        _resident_spec((D_MODEL, D_FF)),
        _resident_spec((D_FF, D_MODEL)),
    ]
    return pl.pallas_call(
        functools.partial(_mix_ffn_kernel, split=tuple(split), tm=tm),
        grid=(M_ALL // tm,),
        in_specs=in_specs,
        out_specs=pl.BlockSpec((tm, D_MODEL), lambda i: (i, 0)),
        out_shape=jax.ShapeDtypeStruct((M_ALL, D_MODEL), F32),
        compiler_params=_cparams(("parallel",)),
        name=name,
    )(x, gate_mix, *flat, *w_outs, g.reshape(1, D_MODEL), shift, scale, gate, w1, w3, w2)


def _final_norm_kernel(x_ref, g_ref, oc_ref, ol_ref, *, tm):
    i = pl.program_id(0)
    x = x_ref[...]
    r = lax.rsqrt(jnp.mean(x * x, axis=-1, keepdims=True) + NORM_EPS)
    y = (x * r) * g_ref[...]

    @pl.when(i < N_CTX_ROWS // tm)
    def _():
        oc_ref[...] = y

    @pl.when(i >= N_CTX_ROWS // tm)
    def _():
        ol_ref[...] = y


def _final_norm(x, g):
    tm = 1024
    return pl.pallas_call(
        functools.partial(_final_norm_kernel, tm=tm),
        grid=(M_ALL // tm,),
        in_specs=[pl.BlockSpec((tm, D_MODEL), lambda i: (i, 0)), _full_spec((1, D_MODEL))],
        out_specs=_part_specs(tm, D_MODEL),
        out_shape=[jax.ShapeDtypeStruct((N_CTX_ROWS, D_MODEL), F32),
                   jax.ShapeDtypeStruct((N_LAT_ROWS, D_MODEL), F32)],
        compiler_params=_cparams(("arbitrary",)),
        name="final_norm",
    )(x, g.reshape(1, D_MODEL))


def _half_masks():
    lane = lax.broadcasted_iota(jnp.int32, (1, LANES), 1)
    return (lane < HEAD_DIM, lane >= HEAD_DIM)


def _na_ctx_kernel(q_ref, k_ref, v_ref, o_ref):
    masks = _half_masks()
    work = []
    for p in range(D_NA // LANES):
        sl = slice(p * LANES, (p + 1) * LANES)
        q = q_ref[:, sl] * (QK_SCALE * LOG2_E)
        k = k_ref[:, sl].astype(BF16)
        v = v_ref[:, sl]
        for m in masks:
            work.append(dict(s=_bdot_nt(jnp.where(m, q, 0.0), k), vm=jnp.where(m, v, 0.0).astype(BF16)))
    for w in work:
        e = jnp.exp2(w["s"] - jnp.max(w["s"], axis=-1, keepdims=True))
        w["inv_l"] = 1.0 / jnp.sum(e, axis=-1, keepdims=True)
        w["e"] = e.astype(BF16)
    for w in work:
        w["o"] = jnp.dot(w["e"], w["vm"], preferred_element_type=F32) * w["inv_l"]
    for p in range(D_NA // LANES):
        o_ref[:, p * LANES:(p + 1) * LANES] = (work[2 * p]["o"] + work[2 * p + 1]["o"]).astype(BF16)


def _na_ctx(q, k, v):
    spec = pl.BlockSpec((SEQ, D_NA), lambda b: (b, 0))
    return pl.pallas_call(
        _na_ctx_kernel,
        grid=(BATCH,),
        in_specs=[spec, spec, spec],
        out_specs=spec,
        out_shape=jax.ShapeDtypeStruct((N_CTX_ROWS, D_NA), BF16),
        compiler_params=_cparams(("parallel",)),
        name="na_ctx",
    )(q, k, v)


def _na_lat_kernel(q_ref, k_ref, v_ref, kc_ref, vc_ref, bias_ref, o_ref, *, rows_per_step):
    n_loc = NA_KH * GRID_W
    masks = _half_masks()
    kc = kc_ref[0].astype(BF16)
    vc = vc_ref[0]
    vcm = [jnp.where(m, vc, 0.0).astype(BF16) for m in masks]
    work = []
    for rr in range(rows_per_step):
        i = pl.program_id(2) * rows_per_step + rr
        r0 = jnp.clip(i - NA_KH // 2, 0, GRID_ROWS - NA_KH)
        start = pl.multiple_of(r0 * GRID_W, GRID_W)
        q = q_ref[rr * GRID_W:(rr + 1) * GRID_W, :] * QK_SCALE
        kl = k_ref[pl.ds(start, n_loc), :].astype(BF16)
        vl = v_ref[pl.ds(start, n_loc), :]
        for hh, m in enumerate(masks):
            qm = jnp.where(m, q, 0.0).astype(BF16)
            work.append(dict(rr=rr, hh=hh, vlm=jnp.where(m, vl, 0.0).astype(BF16),
                             sl=_bdot_nt(qm, kl) + bias_ref[r0 - i + NA_KH - 1, hh], sc=_bdot_nt(qm, kc)))
    for w in work:
        mx = jnp.maximum(jnp.max(w["sl"], axis=-1, keepdims=True), jnp.max(w["sc"], axis=-1, keepdims=True))
        w["el"] = jnp.exp(w["sl"] - mx)
        w["ec"] = jnp.exp(w["sc"] - mx)
    for w in work:
        inv_l = 1.0 / (jnp.sum(w["el"], axis=-1, keepdims=True) + jnp.sum(w["ec"], axis=-1, keepdims=True))
        w["o"] = (_bdot(w["el"], w["vlm"]) + _bdot(w["ec"], vcm[w["hh"]])) * inv_l
    for rr in range(rows_per_step):
        a, b = [w["o"] for w in work if w["rr"] == rr]
        o_ref[rr * GRID_W:(rr + 1) * GRID_W, :] = (a + b).astype(BF16)


def _na_bias_table(rpb):
    jj = jnp.arange(GRID_W)[:, None]
    cc = jnp.arange(GRID_W)[None, :]
    c0 = jnp.clip(jj - NA_KW // 2, 0, GRID_W - NA_KW)
    inwin = (cc >= c0) & (cc < c0 + NA_KW)
    idx = jnp.clip(cc - jj + NA_KW - 1, 0, 2 * NA_KW - 2)
    full = rpb[:, :, idx]
    full = jnp.where(inwin[None, None], full, NEG_BIG)
    tabs = []
    for v in range(NA_KH):
        t = full[:, v:v + NA_KH]
        tabs.append(t.transpose(0, 2, 1, 3).reshape(N_HEADS, GRID_W, NA_KH * GRID_W))
    return jnp.stack(tabs, axis=0)


def _na_lat(q, k, v, kc, vc, bias_tab):
    rows_per_step = 8
    tq = rows_per_step * GRID_W
    steps = GRID_ROWS // rows_per_step
    lat_blk = N_CTX_ROWS // DEC_SEQ
    row_blk = N_CTX_ROWS // tq
    return pl.pallas_call(
        functools.partial(_na_lat_kernel, rows_per_step=rows_per_step),
        grid=(DEC_BATCH, D_NA // LANES, steps),
        in_specs=[
            pl.BlockSpec((tq, LANES), lambda b, p, i: (row_blk + b * steps + i, p)),
            pl.BlockSpec((DEC_SEQ, LANES), lambda b, p, i: (lat_blk + b, p)),
            pl.BlockSpec((DEC_SEQ, LANES), lambda b, p, i: (lat_blk + b, p)),
            pl.BlockSpec((1, PAST_LEN, LANES), lambda b, p, i: (b, 0, p)),
            pl.BlockSpec((1, PAST_LEN, LANES), lambda b, p, i: (b, 0, p)),
            pl.BlockSpec((NA_KH, 2, GRID_W, NA_KH * GRID_W), lambda b, p, i: (0, p, 0, 0)),
        ],
        out_specs=pl.BlockSpec((tq, LANES), lambda b, p, i: (b * steps + i, p)),
        out_shape=jax.ShapeDtypeStruct((N_LAT_ROWS, D_NA), BF16),
        compiler_params=_cparams(("parallel", "parallel", "arbitrary")),
        name="na_lat",
    )(q, k, v, kc, vc, bias_tab)


def _lam_value(lq_ref, lk_ref, lam_init):
    s = jnp.sum(lq_ref[...] * lk_ref[...], axis=-1, keepdims=True)
    return jnp.exp(s[0:1]) - jnp.exp(s[1:2]) + lam_init


def _subln(o, sub, lam_init):
    r = lax.rsqrt(jnp.mean(o * o, axis=-1, keepdims=True) + NORM_EPS)
    return ((o * r) * sub) * (1.0 - lam_init)


def _diff_ctx_kernel(lq_ref, lk_ref, sub_ref, q_ref, k_ref, v_ref, o_ref, *, lam_init):
    lam = _lam_value(lq_ref, lk_ref, lam_init)
    masks = _half_masks()
    work = []
    for h in range(N_HEADS):
        sl = slice(h * LANES, (h + 1) * LANES)
        q = q_ref[:, sl] * (QK_SCALE * LOG2_E)
        k = k_ref[:, sl].astype(BF16)
        for m in masks:
            work.append(dict(s=_bdot_nt(jnp.where(m, q, 0.0), k)))
    for w in work:
        e = jnp.exp2(w["s"] - jnp.max(w["s"], axis=-1, keepdims=True))
        w["p"] = e * (1.0 / jnp.sum(e, axis=-1, keepdims=True))
    outs = []
    for h in range(N_HEADS):
        att = work[2 * h]["p"] - lam * work[2 * h + 1]["p"]
        outs.append(_bdot(att, v_ref[:, h * LANES:(h + 1) * LANES]))
    for h in range(N_HEADS):
        o_ref[:, h * LANES:(h + 1) * LANES] = _subln(outs[h], sub_ref[...], lam_init).astype(BF16)


def _diff_ctx(q, k, v, lam_q, lam_k, subln, lam_init):
    spec = pl.BlockSpec((SEQ, D_DIFF), lambda b: (b, 0))
    return pl.pallas_call(
        functools.partial(_diff_ctx_kernel, lam_init=lam_init),
        grid=(BATCH,),
        in_specs=[_full_spec((2, HEAD_DIM)), _full_spec((2, HEAD_DIM)), _full_spec((1, LANES)),
                  spec, spec, spec],
        out_specs=spec,
        out_shape=jax.ShapeDtypeStruct((N_CTX_ROWS, D_DIFF), BF16),
        compiler_params=_cparams(("parallel",)),
        name="diff_ctx",
    )(lam_q, lam_k, subln.reshape(1, LANES), q, k, v)


def _proj_diff_kernel(*refs, n_prev, slot, tm):
    x_ref, g_ref, sh_ref, sc_ref, w_ref, cos_ref, sin_ref = refs[:7]
    q_ref, k_ref, v_ref, kc_ref, vc_ref, qr_ref, kr_ref, vt_ref = refs[7 + n_prev:]
    i = pl.program_id(0)
    n_ctx_tiles = N_CTX_ROWS // tm

    def project():
        h = _norm_mod(x_ref[...], g_ref[...], sh_ref[0], sc_ref[0]).astype(BF16)
        return [jnp.dot(h, w_ref[:, j * D_DIFF:(j + 1) * D_DIFF], preferred_element_type=F32) for j in range(3)]

    @pl.when(i < n_ctx_tiles)
    def _():
        ys = project()
        for o_ref, y in zip((q_ref, k_ref, v_ref), ys):
            o_ref[...] = y
        for c_ref, y in ((kc_ref, ys[1]), (vc_ref, ys[2])):
            for s in range(tm // SEQ):
                c_ref[s, 0] = y[s * SEQ:(s + 1) * SEQ]
                if slot == 0:
                    c_ref[s, 1] = jnp.zeros((SEQ, D_DIFF), F32)

    @pl.when(i >= n_ctx_tiles)
    def _():
        ys = project()
        cos = cos_ref[...]
        sin = sin_ref[...]
        lane = lax.broadcasted_iota(jnp.int32, (1, LANES), 1)
        first = (lane % (2 * ROPE_F)) < ROPE_F
        for y, o_ref, scale in ((ys[0], qr_ref, QK_SCALE * LOG2_E), (ys[1], kr_ref, None)):
            for j in range(D_DIFF // LANES):
                x = y[:, j * LANES:(j + 1) * LANES]
                partner = jnp.where(first, pltpu.roll(x, LANES - ROPE_F, 1), pltpu.roll(x, ROPE_F, 1))
                r = x * cos + partner * sin
                if scale is not None:
                    r = r * scale
                o_ref[:, j * LANES:(j + 1) * LANES] = r.astype(BF16)
        vt_ref[0] = ys[2].T.astype(BF16)


def _rope_tables():
    t = jnp.arange(DEC_SEQ)
    pos = jnp.stack([t // GRID_W, t % GRID_W], -1).astype(F32)
    inv = ROPE_BASE ** (-jnp.arange(ROPE_F, dtype=F32) / ROPE_F)
    ang = pos[:, :, None] * inv
    cos, sin = jnp.cos(ang), jnp.sin(ang)
    cos64 = jnp.concatenate([cos, cos], axis=-1).reshape(DEC_SEQ, HEAD_DIM)
    sin64 = jnp.concatenate([-sin, sin], axis=-1).reshape(DEC_SEQ, HEAD_DIM)
    return jnp.tile(cos64, (1, 2)), jnp.tile(sin64, (1, 2))


def _proj_diff(x, g, shift, scale, w_bf16, cos_t, sin_t, slot, prev_caches):
    tm = 512
    n_ctx_tiles = N_CTX_ROWS // tm
    per_seq = DEC_SEQ // tm
    seq_per_tile = tm // SEQ
    n_prev = len(prev_caches)

    def lat(i):
        return jnp.maximum(i - n_ctx_tiles, 0)

    ctx_spec, lat_spec = _part_specs(tm, D_DIFF)
    tab_spec = pl.BlockSpec((tm, LANES), lambda i: (lat(i) % per_seq, 0))
    if slot == 0:
        cache_spec = pl.BlockSpec((seq_per_tile, 2, SEQ, D_DIFF),
                                  lambda i: (jnp.minimum(i, n_ctx_tiles - 1), 0, 0, 0))
    else:
        cache_spec = pl.BlockSpec((seq_per_tile, 1, SEQ, D_DIFF),
                                  lambda i: (jnp.minimum(i, n_ctx_tiles - 1), slot, 0, 0))
    ctx_shape = jax.ShapeDtypeStruct((N_CTX_ROWS, D_DIFF), F32)
    cache_shape = jax.ShapeDtypeStruct((BATCH, 2, SEQ, D_DIFF), F32)
    lat_shape = jax.ShapeDtypeStruct((N_LAT_ROWS, D_DIFF), BF16)
    outs = pl.pallas_call(
        functools.partial(_proj_diff_kernel, n_prev=n_prev, slot=slot, tm=tm),
        grid=(M_ALL // tm,),
        in_specs=[
            pl.BlockSpec((tm, D_MODEL), lambda i: (i, 0)),
            _full_spec((1, D_MODEL)),
            _mod_spec(tm),
            _mod_spec(tm),
            _resident_spec((D_MODEL, 3 * D_DIFF)),
            tab_spec,
            tab_spec,
        ] + [pl.BlockSpec(memory_space=pl.ANY)] * n_prev,
        out_specs=[ctx_spec, ctx_spec, ctx_spec, cache_spec, cache_spec, lat_spec, lat_spec,
                   pl.BlockSpec((1, D_DIFF, tm), lambda i: (lat(i) // per_seq, 0, lat(i) % per_seq))],
        out_shape=[ctx_shape, ctx_shape, ctx_shape, cache_shape, cache_shape, lat_shape, lat_shape,
                   jax.ShapeDtypeStruct((DEC_BATCH, D_DIFF, DEC_SEQ), BF16)],
        input_output_aliases={7 + j: 3 + j for j in range(n_prev)},
        compiler_params=_cparams(("arbitrary",)),
        name="proj_odd",
    )(x, g.reshape(1, D_MODEL), shift, scale, w_bf16, cos_t, sin_t, *prev_caches)
    return outs[:3], outs[3:5], outs[5:]


def _diff_lat_kernel(lq_ref, lk_ref, sub_ref, q_ref, kl_ref, vlt_ref, kc_ref, vct_ref, o_ref, *,
                     lam_init, sub_q, n_sub):
    lam = _lam_value(lq_ref, lk_ref, lam_init)
    kl = kl_ref[...]
    kc = kc_ref[0].astype(BF16)
    vlt = vlt_ref[0]
    vct = vct_ref[0].astype(BF16)
    work = []
    for t in range(n_sub):
        q = q_ref[t * sub_q:(t + 1) * sub_q, :]
        for m in _half_masks():
            qm = jnp.where(m, q, jnp.zeros_like(q))
            work.append(dict(sl=_bdot_nt(kl, qm), sc=_bdot_nt(kc, qm)))
    for w in work:
        mx = jnp.maximum(jnp.max(w["sl"], axis=0, keepdims=True), jnp.max(w["sc"], axis=0, keepdims=True))
        el = jnp.exp2(w["sl"] - mx)
        ec = jnp.exp2(w["sc"] - mx)
        w["inv_l"] = 1.0 / (jnp.sum(el, axis=0, keepdims=True) + jnp.sum(ec, axis=0, keepdims=True))
        w["el"] = el.astype(BF16)
        w["ec"] = ec.astype(BF16)
    for w in work:
        w["o"] = (jnp.dot(vlt, w["el"], preferred_element_type=F32)
                  + jnp.dot(vct, w["ec"], preferred_element_type=F32)) * w["inv_l"]
    for t in range(n_sub):
        ot = work[2 * t]["o"] - lam * work[2 * t + 1]["o"]
        r = lax.rsqrt(jnp.mean(ot * ot, axis=0, keepdims=True) + NORM_EPS)
        on = ((ot * r) * sub_ref[...]) * (1.0 - lam_init)
        o_ref[t * sub_q:(t + 1) * sub_q, :] = on.T.astype(BF16)


def _diff_lat(q_r, k_r, v_t, kc, vc_t, lam_q, lam_k, subln, lam_init):
    sub_q, n_sub = 256, 2
    tq = sub_q * n_sub
    per_seq = DEC_SEQ // tq
    return pl.pallas_call(
        functools.partial(_diff_lat_kernel, lam_init=lam_init, sub_q=sub_q, n_sub=n_sub),
        grid=(DEC_BATCH, N_HEADS, per_seq),
        in_specs=[
            _full_spec((2, HEAD_DIM)),
            _full_spec((2, HEAD_DIM)),
            _full_spec((LANES, 1)),
            pl.BlockSpec((tq, LANES), lambda b, h, i: (b * per_seq + i, h)),
            pl.BlockSpec((DEC_SEQ, LANES), lambda b, h, i: (b, h)),
            pl.BlockSpec((1, LANES, DEC_SEQ), lambda b, h, i: (b, h, 0)),
            pl.BlockSpec((1, PAST_LEN, LANES), lambda b, h, i: (b, 0, h)),
            pl.BlockSpec((1, LANES, PAST_LEN), lambda b, h, i: (b, h, 0)),
        ],
        out_specs=pl.BlockSpec((tq, LANES), lambda b, h, i: (b * per_seq + i, h)),
        out_shape=jax.ShapeDtypeStruct((N_LAT_ROWS, D_DIFF), BF16),
        compiler_params=_cparams(("parallel", "parallel", "arbitrary")),
        name="diff_lat",
    )(lam_q, lam_k, subln.reshape(LANES, 1), q_r, k_r, v_t, kc, vc_t)


def _token_shift(u, prev_row, next_row, mu):
    n = u.shape[0]
    rows = lax.broadcasted_iota(jnp.int32, (n, 1), 0)
    up = jnp.where(rows == 0, prev_row, pltpu.roll(u, 1, 0))
    un = jnp.where(rows == n - 1, next_row, pltpu.roll(u, n - 1, 0))
    return u + mu[0:1] * (up - u) + mu[1:2] * (un - u)


def _icl_rate_and_key(us, a0, a2p, k_a):
    lo = us[:, 3 * D_RWKV:3 * D_RWKV + LANES]
    k = us[:, D_RWKV:2 * D_RWKV]
    a = _sigmoid(a0 + _bdot(lo, a2p))
    return a, k * (1.0 + (a - 1.0) * k_a)


def _stack_blockdiag(x):
    shape = (2 * x.shape[0], x.shape[1])
    row = lax.broadcasted_iota(jnp.int32, shape, 0)
    col = lax.broadcasted_iota(jnp.int32, shape, 1)
    keep = (row // HEAD_DIM) == ((col % LANES) // HEAD_DIM)
    return jnp.where(keep, jnp.concatenate([x, x], axis=0), 0.0).astype(BF16)


def _rwkv_chunk_inputs(u, prev_row, next_row, reverse, mu, w0, w2p, a0, a2p, k_k, k_a, bd_ones):
    us = _token_shift(u, prev_row, next_row, mu)
    r = us[:, 0:D_RWKV]
    k = us[:, D_RWKV:2 * D_RWKV]
    v = us[:, 2 * D_RWKV:3 * D_RWKV]
    lo = us[:, 3 * D_RWKV:3 * D_RWKV + LANES]
    w_raw = w0 + _bdot(jnp.tanh(lo), w2p)
    logw = -jnp.exp(-_softplus(-w_raw) - 0.5)
    a, kt = _icl_rate_and_key(us, a0, a2p, k_a)
    kk_raw = k * k_k
    ss = _head_sums(kk_raw * kk_raw, bd_ones)
    kk = kk_raw * lax.rsqrt(jnp.maximum(ss, 1e-12))
    row = lax.broadcasted_iota(jnp.int32, (CHUNK, CHUNK), 0)
    col = lax.broadcasted_iota(jnp.int32, (CHUNK, CHUNK), 1)
    seen = (col >= row) if reverse else (col <= row)
    cum = _dot3_lhs_exact(seen.astype(F32).astype(BF16), logw)
    p_inv = jnp.exp(-cum)
    tot = cum[0:1, :] if reverse else cum[CHUNK - 1:CHUNK, :]
    return dict(a_hat=jnp.exp(cum - logw) * kk, b_hat=(kk * a) * p_inv, k_hat=kt * p_inv,
                r_hat=r * jnp.exp(cum), v=v, p_tot=jnp.exp(tot))


def _rwkv_scan_kernel(*refs, n_chunks, n_par, has_state):
    u_refs = (refs[0:3], refs[3:6])
    pos = 6
    s0_ref = None
    if has_state:
        s0_ref = refs[pos]
        pos += 1
    mu_ref, w0_ref, w2_ref, a0_ref, a2_ref, kk_ref, ka_ref, bd_ref = refs[pos:pos + 8]
    y_refs = refs[pos + 8:pos + 10]
    sfin_ref, s_ref = refs[pos + 10], refs[pos + 11]
    c = pl.program_id(1)
    n_pairs = D_RWKV // LANES

    @pl.when(c == 0)
    def _():
        if has_state:
            s_ref[...] = s0_ref[...]
        else:
            s_ref[...] = jnp.zeros(s_ref.shape, F32)

    at_start = jnp.where(c == 0, 0.0, 1.0)
    at_end = jnp.where(c == n_chunks - 1, 0.0, 1.0)
    has_prev = (at_start, at_end)
    has_next = (at_end, at_start)

    row = lax.broadcasted_iota(jnp.int32, (CHUNK, LANES), 0)
    colm = lax.broadcasted_iota(jnp.int32, (CHUNK, LANES), 1) % CHUNK
    strict = (colm < row, colm > row)
    incl = (colm <= row, colm >= row)
    same_blk = (row // INV_BLOCK) == (colm // INV_BLOCK)
    eye = (row == colm).astype(F32)
    prow = lax.broadcasted_iota(jnp.int32, (LANES, LANES), 0)
    pcol = lax.broadcasted_iota(jnp.int32, (LANES, LANES), 1)
    blockdiag = (prow // HEAD_DIM) == (pcol // HEAD_DIM)
    eye_pair = (prow == pcol).astype(F32)

    def stream_chains(j, d):
        u_ref, up_ref, un_ref = u_refs[d]
        q = _rwkv_chunk_inputs(
            u_ref[j], up_ref[j, 7:8, :] * has_prev[d], un_ref[j, 0:1, :] * has_next[d], d == 1,
            mu_ref[...], w0_ref[d], w2_ref[d], a0_ref[d], a2_ref[d], kk_ref[...], ka_ref[...], bd_ref[...])
        chains = []
        for p in range(n_pairs):
            sl = slice(p * LANES, (p + 1) * LANES)
            chains.append(dict(j=j, d=d, p=p, sl=sl, ah=q["a_hat"][:, sl], bh=q["b_hat"][:, sl],
                               kh=q["k_hat"][:, sl], rh=q["r_hat"][:, sl], vv=q["v"][:, sl],
                               pt=q["p_tot"][:, sl]))
        return chains

    def s_scores(ch):
        ar = jnp.concatenate([ch["ah"], ch["rh"]], axis=0)
        x = _bdot_nt(ar, jnp.concatenate([_stack_blockdiag(ch["bh"]), _stack_blockdiag(ch["kh"])], axis=0))
        xb, xk = x[:, :LANES], x[:, LANES:]
        d = ch["d"]
        l_ab = jnp.where(strict[d], xb[:CHUNK], 0.0)
        ch["l_ak"] = jnp.where(strict[d], xk[:CHUNK], 0.0)
        ch["m_rb"] = jnp.where(incl[d], xb[CHUNK:], 0.0)
        ch["m_rk"] = jnp.where(incl[d], xk[CHUNK:], 0.0)
        ch["x1"] = -jnp.where(same_blk, l_ab, 0.0)
        ch["l_off"] = jnp.where(same_blk, 0.0, l_ab)
        ch["vs"] = _stack_blockdiag(ch["vv"])

    def s_x2(ch):
        ch["x2"] = _bdot(ch["x1"], _stack_blockdiag(ch["x1"]))
        ch["x2s"] = _stack_blockdiag(ch["x2"])
        ch["u"] = _bdot(ch["l_ak"], ch["vs"])

    def s_x4(ch):
        td = eye + ch["x1"]
        both = _bdot(jnp.concatenate([ch["x2"], td], axis=0), ch["x2s"])
        ch["x4"] = both[:CHUNK]
        ch["td"] = td + both[CHUNK:]

    def s_x8(ch):
        both = _bdot(jnp.concatenate([ch["x4"], ch["td"]], axis=0), _stack_blockdiag(ch["x4"]))
        ch["x8s"] = _stack_blockdiag(both[:CHUNK])
        ch["td"] = ch["td"] + both[CHUNK:]

    def s_td(ch):
        ch["td"] = ch["td"] + _bdot(ch["td"], ch["x8s"])

    def s_mm(ch):
        ch["mm"] = _bdot(ch["td"], _stack_blockdiag(ch["l_off"]))

    def s_m2(ch):
        ch["m2s"] = _stack_blockdiag(_bdot(ch["mm"], _stack_blockdiag(ch["mm"])))

    def s_n2(ch):
        n1 = eye - ch["mm"]
        ch["n2"] = n1 + _bdot(n1, ch["m2s"])

    def s_tinv(ch):
        ch["t_inv"] = _bdot(ch["n2"], _stack_blockdiag(ch["td"]))

    def s_tx(ch):
        ch["tx"] = _bdot(ch["t_inv"], _stack_blockdiag(jnp.concatenate([ch["ah"], ch["u"]], axis=1)))

    def s_local(ch):
        tx = ch["tx"]
        mx = _bdot(ch["m_rb"], _stack_blockdiag(tx))
        ch["q_eff"] = ch["rh"] - mx[:, :LANES]
        ch["y_loc"] = _bdot(ch["m_rk"], ch["vs"]) - mx[:, LANES:]
        ch["g"] = jnp.where(blockdiag, eye_pair - _bdot_tn(tx[:, :LANES], ch["bh"]), 0.0) * ch["pt"]
        ch["h"] = jnp.where(
            blockdiag,
            _bdot_tn(jnp.concatenate([ch["vv"], -tx[:, LANES:]], axis=0),
                     jnp.concatenate([ch["kh"], ch["bh"]], axis=0)),
            0.0) * ch["pt"]

    def s_state(ch):
        j, d, p = ch["j"], ch["d"], ch["p"]
        s_old = s_ref[j, d, p]
        y_refs[d][j, :, ch["sl"]] = _bdot_nt(ch["q_eff"], s_old) + ch["y_loc"]
        s_ref[j, d, p] = _bdot(s_old, ch["g"]) + ch["h"]

    chains = [ch for j in range(n_par) for d in range(2) for ch in stream_chains(j, d)]
    for stage in (s_scores, s_x2, s_x4, s_x8, s_td, s_mm, s_m2, s_n2, s_tinv, s_tx, s_local, s_state):
        for ch in chains:
            stage(ch)

    @pl.when(c == n_chunks - 1)
    def _():
        sfin_ref[...] = s_ref[...]


def _rwkv_scan(u, s0p, prm, row_base, n_seq, seq_len):
    n_par = 2
    n_chunks = seq_len // CHUNK
    n_pairs = D_RWKV // LANES
    base_blk = row_base // (seq_len * n_par)
    n_blk8 = seq_len // 8
    has_state = s0p is not None
    u3 = u.reshape(M_ALL // seq_len, seq_len, P_RWKV)

    def chunk_pos(d, c):
        return c if d == 0 else n_chunks - 1 - c

    in_specs, args = [], []
    for d in range(2):
        in_specs += [
            pl.BlockSpec((n_par, CHUNK, P_RWKV), lambda g, c, d=d: (base_blk + g, chunk_pos(d, c), 0)),
            pl.BlockSpec((n_par, 8, P_RWKV),
                         lambda g, c, d=d: (base_blk + g, jnp.maximum(chunk_pos(d, c) * (CHUNK // 8) - 1, 0), 0)),
            pl.BlockSpec((n_par, 8, P_RWKV),
                         lambda g, c, d=d: (base_blk + g,
                                            jnp.minimum((chunk_pos(d, c) + 1) * (CHUNK // 8), n_blk8 - 1), 0)),
        ]
        args += [u3, u3, u3]
    state_spec = pl.BlockSpec((n_par, 2, n_pairs, LANES, LANES), lambda g, c: (g, 0, 0, 0, 0))
    if has_state:
        in_specs.append(state_spec)
        args.append(s0p)
    in_specs += [
        _full_spec((2, P_RWKV)),
        _full_spec((2, 1, D_RWKV)),
        _full_spec((2, LANES, D_RWKV)),
        _full_spec((2, 1, D_RWKV)),
        _full_spec((2, LANES, D_RWKV)),
        _full_spec((1, D_RWKV)),
        _full_spec((1, D_RWKV)),
        _full_spec((MXU_WIDTH, MXU_WIDTH)),
    ]
    args += [prm["mu"], prm["w0"], prm["w2p"], prm["a0"], prm["a2p"], prm["kk"], prm["ka"], prm["bd_ones"]]
    y_shape = jax.ShapeDtypeStruct((n_seq, seq_len, D_RWKV), F32)
    y_f, y_b, s_fin = pl.pallas_call(
        functools.partial(_rwkv_scan_kernel, n_chunks=n_chunks, n_par=n_par, has_state=has_state),
        grid=(n_seq // n_par, n_chunks),
        in_specs=in_specs,
        out_specs=[
            pl.BlockSpec((n_par, CHUNK, D_RWKV), lambda g, c: (g, chunk_pos(0, c), 0)),
            pl.BlockSpec((n_par, CHUNK, D_RWKV), lambda g, c: (g, chunk_pos(1, c), 0)),
            state_spec,
        ],
        out_shape=[y_shape, y_shape, jax.ShapeDtypeStruct((n_seq, 2, n_pairs, LANES, LANES), F32)],
        scratch_shapes=[pltpu.VMEM((n_par, 2, n_pairs, LANES, LANES), F32)],
        compiler_params=_cparams(("parallel", "arbitrary")),
        name="rwkv_scan_" + ("lat" if has_state else "ctx"),
    )(*args)
    n_rows = n_seq * seq_len
    return (y_f.reshape(n_rows, D_RWKV), y_b.reshape(n_rows, D_RWKV)), s_fin


def _rwkv_fin_kernel(u_ref, up_ref, un_ref, ycf_ref, ylf_ref, ycb_ref, ylb_ref, mu_ref, a0_ref, a2_ref,
                     ka_ref, bonus_ref, g2_ref, lnw_ref, lnb_ref, bd1_ref, bdm_ref, o_ref, *, tm):
    i = pl.program_id(0)
    n_ctx_tiles = N_CTX_ROWS // tm
    per_seq = DEC_SEQ // tm
    j = (i - n_ctx_tiles) % per_seq
    is_ctx = i < n_ctx_tiles
    not_first = jnp.where(is_ctx | (j == 0), 0.0, 1.0)
    not_last = jnp.where(is_ctx | (j == per_seq - 1), 0.0, 1.0)
    us = _token_shift(u_ref[...], up_ref[7:8, :] * not_first, un_ref[0:1, :] * not_last, mu_ref[...])
    r = us[:, 0:D_RWKV]
    v = us[:, 2 * D_RWKV:3 * D_RWKV]
    g_lo = us[:, 3 * D_RWKV + LANES:]
    rk = jnp.zeros((tm, D_RWKV), F32)
    for d in range(2):
        _, kt = _icl_rate_and_key(us, a0_ref[d], a2_ref[d], ka_ref[...])
        rk = rk + r * kt * bonus_ref[d]
    bon = _head_sums(rk, bd1_ref[...]) * v
    y = _pick_part(i, tm, ycf_ref, ylf_ref) + _pick_part(i, tm, ycb_ref, ylb_ref)
    mean = _head_sums(y, bdm_ref[...])
    yc = y - mean
    var = _head_sums(yc * yc, bdm_ref[...])
    yn = (yc * lax.rsqrt(var + GN_EPS)) * lnw_ref[...] + lnb_ref[...]
    gate = _bdot(_sigmoid(g_lo), g2_ref[...])
    o_ref[...] = ((yn + bon) * gate).astype(BF16)


def _rwkv_finish(u, y_ctx, y_lat, prm):
    tm = 256
    n_blk8 = M_ALL // 8
    return pl.pallas_call(
        functools.partial(_rwkv_fin_kernel, tm=tm),
        grid=(M_ALL // tm,),
        in_specs=[
            pl.BlockSpec((tm, P_RWKV), lambda i: (i, 0)),
            pl.BlockSpec((8, P_RWKV), lambda i: (jnp.maximum(i * (tm // 8) - 1, 0), 0)),
            pl.BlockSpec((8, P_RWKV), lambda i: (jnp.minimum((i + 1) * (tm // 8), n_blk8 - 1), 0)),
            *_part_specs(tm, D_RWKV),
            *_part_specs(tm, D_RWKV),
            _full_spec((2, P_RWKV)),
            _full_spec((2, 1, D_RWKV)),
            _full_spec((2, LANES, D_RWKV)),
            _full_spec((1, D_RWKV)),
            _full_spec((2, 1, D_RWKV)),
            _full_spec((GATE_LORA, D_RWKV)),
            _full_spec((1, D_RWKV)),
            _full_spec((1, D_RWKV)),
            _full_spec((MXU_WIDTH, MXU_WIDTH)),
            _full_spec((MXU_WIDTH, MXU_WIDTH)),
        ],
        out_specs=pl.BlockSpec((tm, D_RWKV), lambda i: (i, 0)),
        out_shape=jax.ShapeDtypeStruct((M_ALL, D_RWKV), BF16),
        compiler_params=_cparams(("parallel",)),
        name="rwkv_finish",
    )(u, u, u, y_ctx[0], y_lat[0], y_ctx[1], y_lat[1], prm["mu"], prm["a0"], prm["a2p"], prm["ka"],
      prm["bonus"], prm["g2"], prm["lnw"], prm["lnb"], prm["bd_ones"], prm["bd_mean"])


def _pack_state_pairs(s):
    lead = s.shape[:-3]
    s = s.reshape(lead + (N_HEADS // 2, 2, HEAD_DIM, HEAD_DIM))
    z = jnp.zeros_like(s[..., 0, :, :])
    top = jnp.concatenate([s[..., 0, :, :], z], axis=-1)
    bot = jnp.concatenate([z, s[..., 1, :, :]], axis=-1)
    return jnp.concatenate([top, bot], axis=-2)


def _unpack_state_pairs(sp):
    lead = sp.shape[:-3]
    a = sp[..., :HEAD_DIM, :HEAD_DIM]
    b = sp[..., HEAD_DIM:, HEAD_DIM:]
    return jnp.stack([a, b], axis=-3).reshape(lead + (N_HEADS, HEAD_DIM, HEAD_DIM))


def kernel(x_prompt, x_sample, c, cache_na_k, cache_na_v, state_rwkv, cache_diff_k, cache_diff_v, c_ctx, w_ada, b_ada, norm_mix, norm_ffn, norm_final, w_in_even, w_out_even, na_rpb, rw_mu, rw_w0, rw_w2, rw_a0, rw_a2, rw_kk, rw_ka, rw_bonus, rw_g2, rw_lnw, rw_lnb, w_qkv_diff, w_out_diff, diff_lam_q, diff_lam_k, diff_subln, ffn_w1, ffn_w3, ffn_w2):
    x = jnp.concatenate([x_prompt.reshape(N_CTX_ROWS, D_MODEL), x_sample.reshape(N_LAT_ROWS, D_MODEL)], axis=0)

    cv8 = jnp.concatenate([c_ctx[None, :], c, jnp.zeros((8 - 1 - DEC_BATCH, D_MODEL), F32)], axis=0)
    mods = _ada_all(cv8, w_ada, b_ada)
    mods = mods.reshape(DEPTH, 8, 6, D_MODEL).transpose(0, 2, 1, 3)[:, :, :1 + DEC_BATCH, None, :]

    hd_idx = jnp.arange(MXU_WIDTH) // HEAD_DIM
    bd_ones = (hd_idx[:, None] == hd_idx[None, :]).astype(BF16)
    bd_mean = (bd_ones.astype(F32) / HEAD_DIM).astype(BF16)
    cos_t, sin_t = _rope_tables()

    rw_out = []
    na_caches, diff_caches = (), ()
    for l in range(DEPTH):
        md = mods[l]
        if l % 2 == 0:
            e = l // 2
            (q, k, v, u), na_caches = _norm_proj(
                x, norm_mix[l], md[0], md[1], w_in_even[e].astype(BF16), (D_NA, D_NA, D_NA, P_RWKV),
                "proj_even", cache_cols=(1, 2), slot=e, prev_caches=na_caches)
            o_ctx = _na_ctx(q, k, v)
            o_lat = _na_lat(q, k, v, cache_na_k[:, e].reshape(DEC_BATCH, PAST_LEN, D_NA),
                            cache_na_v[:, e].reshape(DEC_BATCH, PAST_LEN, D_NA), _na_bias_table(na_rpb[e]))

            zpad = jnp.zeros((2, LANES - DECAY_LORA, D_RWKV), F32)
            prm = {
                "mu": rw_mu[e],
                "w0": rw_w0[e].reshape(2, 1, D_RWKV),
                "w2p": jnp.concatenate([rw_w2[e], zpad], axis=1),
                "a0": rw_a0[e].reshape(2, 1, D_RWKV),
                "a2p": jnp.concatenate([zpad, rw_a2[e]], axis=1),
                "kk": rw_kk[e].reshape(1, D_RWKV),
                "ka": rw_ka[e].reshape(1, D_RWKV),
                "bonus": rw_bonus[e].reshape(2, 1, D_RWKV),
                "g2": rw_g2[e],
                "lnw": rw_lnw[e].reshape(1, D_RWKV),
                "lnb": rw_lnb[e].reshape(1, D_RWKV),
                "bd_ones": bd_ones,
                "bd_mean": bd_mean,
            }
            y_ctx, s_ctx = _rwkv_scan(u, None, prm, 0, BATCH, SEQ)
            y_lat, _ = _rwkv_scan(u, _pack_state_pairs(state_rwkv[:, e]), prm, N_CTX_ROWS, DEC_BATCH, DEC_SEQ)
            o_rw = _rwkv_finish(u, y_ctx, y_lat, prm)

            w_out = w_out_even[e].astype(BF16)
            mixed, w_outs = [(o_ctx, o_lat), o_rw], [w_out[:D_NA], w_out[D_NA:]]

            rw_out.append(_unpack_state_pairs(s_ctx))
        else:
            o = l // 2
            lam_init = 0.8 - 0.6 * math.exp(-0.3 * l)
            (q, k, v), diff_caches, (q_r, k_r, v_t) = _proj_diff(
                x, norm_mix[l], md[0], md[1], w_qkv_diff[o].astype(BF16), cos_t, sin_t, o, diff_caches)
            o_ctx = _diff_ctx(q, k, v, diff_lam_q[o], diff_lam_k[o], diff_subln[o], lam_init)
            o_lat = _diff_lat(q_r, k_r, v_t, cache_diff_k[:, o].reshape(DEC_BATCH, PAST_LEN, D_DIFF),
                              cache_diff_v[:, o].reshape(DEC_BATCH, PAST_LEN, D_DIFF).swapaxes(1, 2),
                              diff_lam_q[o], diff_lam_k[o], diff_subln[o], lam_init)
            mixed, w_outs = [(o_ctx, o_lat)], [w_out_diff[o].astype(BF16)]
        x = _mix_ffn(x, md[2], mixed, w_outs, norm_ffn[l], md[3], md[4], md[5], ffn_w1[l].astype(BF16),
                     ffn_w3[l].astype(BF16), ffn_w2[l].astype(BF16), "mix_ffn")

    y_ctx, y_lat = _final_norm(x, norm_final)
    y_prompt = y_ctx.reshape(BATCH, SEQ, D_MODEL)
    y_sample = y_lat.reshape(DEC_BATCH, DEC_SEQ, D_MODEL)
    na_shape = (BATCH, DEPTH // 2, SEQ, N_HEADS, HEAD_DIM)
    diff_shape = (BATCH, DEPTH // 2, SEQ, N_HEADS, 2 * HEAD_DIM)
    return (y_prompt, y_sample, na_caches[0].reshape(na_shape), na_caches[1].reshape(na_shape),
            jnp.stack(rw_out, axis=1), diff_caches[0].reshape(diff_shape), diff_caches[1].reshape(diff_shape))
```
